```python
import math
import jax, jax.numpy as jnp
from jax import lax
import numpy as np

D_MODEL = 1024
BATCH = 8
SEQ = 4096
DEPTH = 2
DEC_BATCH = 2
DEC_SEQ = 8192
PAST_LEN = 128

N_MIXERS = 2
N_A_LAYERS = (DEPTH + 1) // 2
N_B_LAYERS = DEPTH // 2
DILATION_PATTERNS = ((128, 1), (512, 4), (2048, 16))
N_GROUPS = len(DILATION_PATTERNS)
A_HEADS = 16
A_HEAD_DIM = D_MODEL // A_HEADS
A_INNER = A_HEADS * A_HEAD_DIM
ROPE_THETA = 10000.0
CHUNK = 128
SGU_HALF = 3 * D_MODEL
SGU_HEADS = 16
SGU_HEAD_CH = SGU_HALF // SGU_HEADS
N_EXPERTS = 16
CAPACITY_FACTOR = 2
D_EXPERT = 2 * D_MODEL
N_MOD = 6
NORM_EPS = 1e-6
NEG_BIG = -1e30

kernel_name = "hybrid_dilated_sgu_ec_moe_encoder"


def _rmsnorm(x, g):
    xf = x.astype(jnp.float32)
    y = xf * lax.rsqrt(jnp.mean(xf * xf, axis=-1, keepdims=True) + NORM_EPS)
    return (y * g.astype(jnp.float32)).astype(x.dtype)


def _layernorm(x, g, b):
    xf = x.astype(jnp.float32)
    mu = jnp.mean(xf, axis=-1, keepdims=True)
    var = jnp.mean(jnp.square(xf - mu), axis=-1, keepdims=True)
    y = (xf - mu) * lax.rsqrt(var + NORM_EPS)
    return (y * g.astype(jnp.float32) + b.astype(jnp.float32)).astype(x.dtype)


def _rope_tables(S):
    inv = ROPE_THETA ** (-jnp.arange(0, A_HEAD_DIM, 2, dtype=jnp.float32) / A_HEAD_DIM)
    ang = jnp.arange(S, dtype=jnp.float32)[:, None] * inv[None, :]
    return jnp.cos(ang), jnp.sin(ang)


def _apply_rope(x, cos, sin):
    xf = x.astype(jnp.float32)
    x1, x2 = xf[..., : A_HEAD_DIM // 2], xf[..., A_HEAD_DIM // 2:]
    c, s = cos[None, :, None, :], sin[None, :, None, :]
    return jnp.concatenate([x1 * c - x2 * s, x2 * c + x1 * s], axis=-1).astype(x.dtype)


def _dilated_band_attention(q, k, v, dilation, radius):
    B, S, H, Dh = q.shape
    L = S // dilation
    W = radius
    nb = -(-L // W)
    Lp = nb * W

    def to_blocks(a):
        a = a.reshape(B, L, dilation, H, Dh).transpose(0, 2, 1, 3, 4)
        a = jnp.pad(a, ((0, 0), (0, 0), (0, Lp - L), (0, 0), (0, 0)))
        return a.reshape(B, dilation, nb, W, H, Dh)

    def neighbours(a):
        ap = jnp.pad(a, ((0, 0), (0, 0), (1, 1), (0, 0), (0, 0), (0, 0)))
        return jnp.concatenate([ap[:, :, :-2], ap[:, :, 1:-1], ap[:, :, 2:]], axis=3)

    qb = to_blocks(q)
    kn = neighbours(to_blocks(k))
    vn = neighbours(to_blocks(v))
    qpos = jnp.arange(Lp).reshape(nb, W)
    kpos = (jnp.arange(nb)[:, None] - 1) * W + jnp.arange(3 * W)[None, :]
    valid = ((jnp.abs(qpos[:, :, None] - kpos[:, None, :]) <= radius)
             & (kpos[:, None, :] >= 0) & (kpos[:, None, :] < L))
    s = jnp.einsum('brnqhd,brnkhd->brnhqk', qb, kn,
                   preferred_element_type=jnp.float32) * (Dh ** -0.5)
    s = jnp.where(valid[None, None, :, None], s, NEG_BIG)
    m = jnp.max(s, axis=-1, keepdims=True)
    p = jnp.exp(s - m)
    den = jnp.sum(p, axis=-1, keepdims=True)
    o = jnp.einsum('brnhqk,brnkhd->brnqhd', (p / den).astype(v.dtype), vn)
    lse = (m + jnp.log(den))[..., 0]
    o = o.reshape(B, dilation, Lp, H, Dh)[:, :, :L].transpose(0, 2, 1, 3, 4).reshape(B, S, H, Dh)
    lse = lse.transpose(0, 1, 2, 4, 3).reshape(B, dilation, Lp, H)[:, :, :L]
    lse = lse.transpose(0, 2, 1, 3).reshape(B, S, H)
    return o, lse


def _mixer_dilated(h, w_in, w_out):
    B, S, D = h.shape
    qkv = (h @ w_in).reshape(B, S, N_GROUPS, 3, A_HEADS, A_HEAD_DIM)
    cos, sin = _rope_tables(S)
    outs, lses = [], []
    for g, (window, dil) in enumerate(DILATION_PATTERNS):
        q = _apply_rope(qkv[:, :, g, 0], cos, sin)
        k = _apply_rope(qkv[:, :, g, 1], cos, sin)
        o, lse = _dilated_band_attention(q, k, qkv[:, :, g, 2], dil, window // (2 * dil))
        outs.append(o)
        lses.append(lse)
    wts = jax.nn.softmax(jnp.stack(lses, axis=0), axis=0)
    o = jnp.einsum('gbsh,gbshd->bshd', wts.astype(h.dtype), jnp.stack(outs, axis=0))
    return o.reshape(B, S, A_INNER) @ w_out


def _mixer_sgu(h, w_in, ln_g, ln_b, w_s, b_s, w_out):
    B, S, D = h.shape
    z = jax.nn.gelu(h @ w_in)
    u, v = z[..., :SGU_HALF], z[..., SGU_HALF:]
    v = _layernorm(v, ln_g, ln_b)
    vc = v.reshape(B, S // CHUNK, CHUNK, SGU_HEADS, SGU_HEAD_CH)
    sv = jnp.einsum('hpq,bnqhc->bnphc', w_s, vc) + b_s.T[None, None, :, :, None]
    return (u * sv.reshape(B, S, SGU_HALF)) @ w_out


def _ec_moe(h, w_router, w_gate, w_up, w_down):
    B, S, D = h.shape
    n = B * S
    t = h.reshape(n, D)
    aff = jax.nn.softmax((t @ w_router).astype(jnp.float32), axis=-1)
    cap = CAPACITY_FACTOR * n // N_EXPERTS
    gate, idx = lax.top_k(aff.T, cap)
    xe = t[idx]
    hg = jnp.einsum('ecd,edf->ecf', xe, w_gate)
    hu = jnp.einsum('ecd,edf->ecf', xe, w_up)
    ye = jnp.einsum('ecf,efd->ecd', jax.nn.silu(hg) * hu, w_down) * gate[..., None].astype(h.dtype)
    out = jnp.zeros_like(t).at[idx.reshape(-1)].add(ye.reshape(-1, D))
    return out.reshape(B, S, D)


def _trunk(x, c, w_mod, b_mod, g_mix, g_ffn, a_w_in, a_w_out, b_w_in, b_ln_g, b_ln_b, b_w_s, b_b_s,
           b_w_out, moe_w_router, moe_w_gate, moe_w_up, moe_w_down, g_final):
    for i in range(DEPTH):
        mod = (jax.nn.silu(c) @ w_mod[i] + b_mod[i])[:, None, :]
        sh1, sc1, ga1, sh2, sc2, ga2 = jnp.split(mod, N_MOD, axis=-1)
        h = _rmsnorm(x, g_mix[i]) * (1 + sc1) + sh1
        j = i // N_MIXERS
        if i % N_MIXERS == 0:
            y = _mixer_dilated(h, a_w_in[j], a_w_out[j])
        else:
            y = _mixer_sgu(h, b_w_in[j], b_ln_g[j], b_ln_b[j], b_w_s[j], b_b_s[j], b_w_out[j])
        x = x + ga1 * y
        h = _rmsnorm(x, g_ffn[i]) * (1 + sc2) + sh2
        x = x + ga2 * _ec_moe(h, moe_w_router[i], moe_w_gate[i], moe_w_up[i], moe_w_down[i])
    return _rmsnorm(x, g_final)


def setup_inputs(seed: int = 0) -> dict:
    key = jax.random.key(seed)
    ks = jax.random.split(key, 24)
    D = D_MODEL
    nrm = lambda k, shape, s: jax.random.normal(k, shape, jnp.float32) * s
    return {
        "x_prompt": nrm(ks[0], (BATCH, SEQ, D), 1.0),
        "x_sample": nrm(ks[1], (DEC_BATCH, DEC_SEQ, D), 1.0),
        "c_prompt": nrm(ks[2], (BATCH, D), 1.0),
        "c_sample": nrm(ks[3], (DEC_BATCH, D), 1.0),
        "w_mod": nrm(ks[4], (DEPTH, D, N_MOD * D), 0.5 * D ** -0.5),
        "b_mod": nrm(ks[5], (DEPTH, N_MOD * D), 0.02),
        "g_mix": 1.0 + nrm(ks[6], (DEPTH, D), 0.02),
        "g_ffn": 1.0 + nrm(ks[7], (DEPTH, D), 0.02),
        "a_w_in": nrm(ks[8], (N_A_LAYERS, D, N_GROUPS * 3 * A_INNER), D ** -0.5),
        "a_w_out": nrm(ks[9], (N_A_LAYERS, A_INNER, D), A_INNER ** -0.5),
        "b_w_in": nrm(ks[10], (N_B_LAYERS, D, 2 * SGU_HALF), D ** -0.5),
        "b_ln_g": 1.0 + nrm(ks[11], (N_B_LAYERS, SGU_HALF), 0.02),
        "b_ln_b": nrm(ks[12], (N_B_LAYERS, SGU_HALF), 0.02),
        "b_w_s": nrm(ks[13], (N_B_LAYERS, SGU_HEADS, CHUNK, CHUNK), 0.5 * CHUNK ** -0.5),
        "b_b_s": 1.0 + nrm(ks[14], (N_B_LAYERS, SGU_HEADS, CHUNK), 0.01),
        "b_w_out": nrm(ks[15], (N_B_LAYERS, SGU_HALF, D), SGU_HALF ** -0.5),
        "moe_w_router": nrm(ks[16], (DEPTH, D, N_EXPERTS), D ** -0.5),
        "moe_w_gate": nrm(ks[17], (DEPTH, N_EXPERTS, D, D_EXPERT), D ** -0.5),
        "moe_w_up": nrm(ks[18], (DEPTH, N_EXPERTS, D, D_EXPERT), D ** -0.5),
        "moe_w_down": nrm(ks[19], (DEPTH, N_EXPERTS, D_EXPERT, D), D_EXPERT ** -0.5),
        "g_final": 1.0 + nrm(ks[20], (D,), 0.02),
    }


def reference(x_prompt, x_sample, c_prompt, c_sample, w_mod, b_mod, g_mix, g_ffn, a_w_in, a_w_out,
              b_w_in, b_ln_g, b_ln_b, b_w_s, b_b_s, b_w_out, moe_w_router, moe_w_gate, moe_w_up,
              moe_w_down, g_final):
    y_prompt = _trunk(x_prompt, c_prompt, w_mod, b_mod, g_mix, g_ffn, a_w_in, a_w_out, b_w_in, b_ln_g,
                      b_ln_b, b_w_s, b_b_s, b_w_out, moe_w_router, moe_w_gate, moe_w_up, moe_w_down, g_final)
    y_sample = _trunk(x_sample, c_sample, w_mod, b_mod, g_mix, g_ffn, a_w_in, a_w_out, b_w_in, b_ln_g,
                      b_ln_b, b_w_s, b_b_s, b_w_out, moe_w_router, moe_w_gate, moe_w_up, moe_w_down, g_final)
    return (y_prompt, y_sample)
```

```python
import functools

import numpy as np
import jax
import jax.numpy as jnp
from jax import lax
from jax.experimental import pallas as pl
from jax.experimental.pallas import tpu as pltpu

F32 = jnp.float32
BF16 = jnp.bfloat16

D = 1024
N_HEADS = 16
HEAD_DIM = 64
DILATIONS = (1, 4, 16)
WINDOWS = (128, 512, 2048)
RADIUS = 64
N_GROUPS = 3
QKV_COLS = N_GROUPS * 3 * D
ROPE_THETA = 10000.0
CHUNK = 128
SGU_HALF = 3 * D
SGU_HEADS = 16
SGU_HEAD_CH = SGU_HALF // SGU_HEADS
N_EXPERTS = 16
CAPACITY_FACTOR = 2
D_EXPERT = 2 * D
NORM_EPS = 1e-6
NEG_BIG = -1e30

LANES = 128
VMEM_LIMIT = 56 * 1024 * 1024

TM_PROJ = 512
TM_SGU = 256
TM_COMB = 256
FFN_TILE = 256
F_CHUNK = 512
WIN = 64
SLOT_CHUNK = 512


def _cparams(sem):
    return pltpu.CompilerParams(dimension_semantics=sem, vmem_limit_bytes=VMEM_LIMIT)


def _split2(x):
    hi = x.astype(BF16)
    lo = (x - hi.astype(F32)).astype(BF16)
    return hi, lo


def _dot(a, b):
    return jnp.dot(a, b, preferred_element_type=F32)


def _dot_nt(a, b):
    return lax.dot_general(a, b, (((1,), (1,)), ((), ())), preferred_element_type=F32)


def _dot3(a, b):
    ah, al = _split2(a)
    bh, bl = _split2(b)
    return _dot(ah, bh) + (_dot(ah, bl) + _dot(al, bh))


def _dot3_nt(a, b):
    ah, al = _split2(a)
    bh, bl = _split2(b)
    return _dot_nt(ah, bh) + (_dot_nt(ah, bl) + _dot_nt(al, bh))


def _sigmoid(x):
    return 1.0 / (1.0 + jnp.exp(-x))


def _gelu_tanh(x):
    c = np.float32(np.sqrt(2.0 / np.pi))
    return 0.5 * x * (1.0 + jnp.tanh(c * (x + np.float32(0.044715) * (x * x * x))))


def _norm_mod(x, g, shift, scale):
    ms = jnp.mean(x * x, axis=-1, keepdims=True)
    return (x * lax.rsqrt(ms + NORM_EPS) * g) * (1.0 + scale) + shift


def _tile_tables(groups, tm):
    tb, tp = [], []
    b0 = 0
    for (B, S) in groups:
        assert S % tm == 0
        for b in range(B):
            for k in range(S // tm):
                tb.append(b0 + b)
                tp.append(k)
        b0 += B
    return jnp.asarray(np.array(tb, np.int32)), jnp.asarray(np.array(tp, np.int32))


def _mod_kernel(c_ref, w_ref, b_ref, o_ref):
    c = c_ref[...]
    o_ref[...] = _dot3(c * _sigmoid(c), w_ref[...]) + b_ref[...]


def _modulation(c_pad, w_mod, b_mod):
    depth, _, n = w_mod.shape
    tn = 2048
    rows = c_pad.shape[0]
    return pl.pallas_call(
        _mod_kernel,
        grid=(depth, n // tn),
        in_specs=[
            pl.BlockSpec((rows, D), lambda l, j: (0, 0)),
            pl.BlockSpec((None, D, tn), lambda l, j: (l, 0, j)),
            pl.BlockSpec((None, 1, tn), lambda l, j: (l, 0, j)),
        ],
        out_specs=pl.BlockSpec((None, rows, tn), lambda l, j: (l, 0, j)),
        out_shape=jax.ShapeDtypeStruct((depth, rows, n), F32),
        compiler_params=_cparams(("parallel", "parallel")),
        name="modulation",
    )(c_pad, w_mod, b_mod.reshape(depth, 1, n))


def _qkv_kernel(tb_ref, tp_ref, x_ref, mod_ref, g_ref, w_ref, cos_ref, sin_ref, o_ref, h_scr):
    j = pl.program_id(1)

    @pl.when(j == 0)
    def _():
        m = mod_ref[...]
        h_scr[...] = _norm_mod(x_ref[...], g_ref[...], m[0:1], m[1:2]).astype(BF16)

    acc = _dot(h_scr[...], w_ref[...])
    kind = j % 3

    @pl.when(kind == 2)
    def _():
        o_ref[...] = acc.astype(BF16)

    @pl.when(kind != 2)
    def _():
        scale = jnp.where(kind == 0, np.float32(HEAD_DIM ** -0.5), np.float32(1.0))
        cos = cos_ref[...] * scale
        sin = sin_ref[...] * scale
        lane = lax.broadcasted_iota(jnp.int32, cos.shape, 1)
        first = (lane % HEAD_DIM) < (HEAD_DIM // 2)
        for c in range(D // LANES):
            xc = acc[:, c * LANES:(c + 1) * LANES]
            partner = jnp.where(first, pltpu.roll(xc, LANES - HEAD_DIM // 2, 1),
                                pltpu.roll(xc, HEAD_DIM // 2, 1))
            o_ref[:, c * LANES:(c + 1) * LANES] = (xc * cos + partner * sin).astype(BF16)


def _qkv_proj(x, mod, g, w_bf, cos_t, sin_t, groups):
    M = x.shape[0]
    tm = TM_PROJ
    tb, tp = _tile_tables(groups, tm)
    grid_spec = pltpu.PrefetchScalarGridSpec(
        num_scalar_prefetch=2,
        grid=(M // tm, QKV_COLS // D),
        in_specs=[
            pl.BlockSpec((tm, D), lambda i, j, tb, tp: (i, 0)),
            pl.BlockSpec((None, 6, D), lambda i, j, tb, tp: (tb[i], 0, 0)),
            pl.BlockSpec((1, D), lambda i, j, tb, tp: (0, 0)),
            pl.BlockSpec((D, D), lambda i, j, tb, tp: (0, j)),
            pl.BlockSpec((tm, LANES), lambda i, j, tb, tp: (tp[i], 0)),
            pl.BlockSpec((tm, LANES), lambda i, j, tb, tp: (tp[i], 0)),
        ],
        out_specs=pl.BlockSpec((tm, D), lambda i, j, tb, tp: (i, j)),
        scratch_shapes=[pltpu.VMEM((tm, D), BF16)],
    )
    return pl.pallas_call(
        _qkv_kernel,
        grid_spec=grid_spec,
        out_shape=jax.ShapeDtypeStruct((M, QKV_COLS), BF16),
        compiler_params=_cparams(("parallel", "arbitrary")),
        name="qkv_rope",
    )(tb, tp, x, mod, g, w_bf, cos_t, sin_t)


def _rope_tables(s_max):
    inv = ROPE_THETA ** (-jnp.arange(0, HEAD_DIM, 2, dtype=F32) / HEAD_DIM)
    ang = jnp.arange(s_max, dtype=F32)[:, None] * inv[None, :]
    cos, sin = jnp.cos(ang), jnp.sin(ang)
    cos_t = jnp.tile(cos, (1, LANES // (HEAD_DIM // 2)))
    sin_t = jnp.tile(jnp.concatenate([-sin, sin], axis=1), (1, LANES // HEAD_DIM))
    return cos_t, sin_t


def _attn_kernel(*refs, seq_blocks, merge, last):
    if merge:
        (q_ref, kp_ref, kc_ref, kn_ref, vp_ref, vc_ref, vn_ref, op_ref, lp_ref, o_ref, l_ref) = refs
    else:
        (q_ref, kp_ref, kc_ref, kn_ref, vp_ref, vc_ref, vn_ref, o_ref, l_ref) = refs
    i = pl.program_id(1)
    (n1, nb1), (n2, nb2) = seq_blocks
    first_grp = i < n1 * nb1
    pos = jnp.where(first_grp, i % nb1, (i - n1 * nb1) % nb2)
    nb = jnp.where(first_grp, nb1, nb2)
    has_prev = pos > 0
    has_next = pos < nb - 1

    W = RADIUS
    qq = lax.broadcasted_iota(jnp.int32, (W, 3 * W), 0)
    kk = lax.broadcasted_iota(jnp.int32, (W, 3 * W), 1)
    band = jnp.abs(qq - (kk - W)) <= RADIUS
    valid = band & ((kk >= W) | has_prev) & ((kk < 2 * W) | has_next)

    if last:
        pass
    l_ref[...] = jnp.zeros(l_ref.shape, F32)
    for h in range(N_HEADS):
        hs = slice(h * HEAD_DIM, (h + 1) * HEAD_DIM)
        qh = q_ref[:, hs]
        kh = jnp.concatenate([kp_ref[:, hs], kc_ref[:, hs], kn_ref[:, hs]], axis=0)
        vh = jnp.concatenate([vp_ref[:, hs], vc_ref[:, hs], vn_ref[:, hs]], axis=0)
        s = _dot_nt(qh, kh)
        s = jnp.where(valid, s, NEG_BIG)
        m = jnp.max(s, axis=-1, keepdims=True)
        p = jnp.exp(s - m)
        den = jnp.sum(p, axis=-1, keepdims=True)
        oh = _dot((p / den).astype(BF16), vh)
        lse = m + jnp.log(den)
        if merge:
            lp = lp_ref[:, h:h + 1]
            mx = jnp.maximum(lp, lse)
            a = jnp.exp(lp - mx)
            b = jnp.exp(lse - mx)
            tot = a + b
            oh = (a * op_ref[:, hs] + b * oh) / tot
            lse = mx + jnp.log(tot)
        o_ref[:, hs] = oh.astype(o_ref.dtype)
        l_ref[:, h:h + 1] = lse


def _attention_group(qkv, g, groups, prev):
    M = qkv.shape[0]
    r = DILATIONS[g]
    W = RADIUS
    rows = M // r
    nblk = rows // W
    for (B, S) in groups:
        assert S % (r * W) == 0
    seq_blocks = tuple((B, S // r // W) for (B, S) in groups)
    merge = prev is not None
    last = g == N_GROUPS - 1
    qkv_v = qkv.reshape(rows, r * QKV_COLS)
    ncol = QKV_COLS // D
    c0 = g * 3

    def cur(res, i):
        return i

    def prv(res, i):
        return jnp.maximum(i - 1, 0)

    def nxt(res, i):
        return jnp.minimum(i + 1, nblk - 1)

    def spec(rowf, col):
        return pl.BlockSpec((W, D), lambda res, i: (rowf(res, i), res * ncol + c0 + col))

    in_specs = [spec(cur, 0), spec(prv, 1), spec(cur, 1), spec(nxt, 1),
                spec(prv, 2), spec(cur, 2), spec(nxt, 2)]
    args = [qkv_v] * 7
    if merge:
        o_prev, l_prev = prev
        in_specs += [pl.BlockSpec((W, D), lambda res, i: (i, res)),
                     pl.BlockSpec((W, LANES), lambda res, i: (i, res))]
        args += [o_prev.reshape(rows, r * D), l_prev.reshape(rows, r * LANES)]
    o_dtype = BF16 if last else F32
    o, l = pl.pallas_call(
        functools.partial(_attn_kernel, seq_blocks=seq_blocks, merge=merge, last=last),
        grid=(r, nblk),
        in_specs=in_specs,
        out_specs=[pl.BlockSpec((W, D), lambda res, i: (i, res)),
                   pl.BlockSpec((W, LANES), lambda res, i: (i, res))],
        out_shape=[jax.ShapeDtypeStruct((rows, r * D), o_dtype),
                   jax.ShapeDtypeStruct((rows, r * LANES), F32)],
        compiler_params=_cparams(("parallel", "parallel")),
        name=f"attn_d{r}",
    )(*args)
    return o.reshape(M, D), l.reshape(M, LANES)


def _router(x1, m, g_ffn, wr_t):
    h2 = _norm_mod(x1, g_ffn, m[3:4], m[4:5])
    logits_t = _dot3_nt(wr_t, h2)
    mx = jnp.max(logits_t, axis=0, keepdims=True)
    e = jnp.exp(logits_t - mx)
    return h2, e / jnp.sum(e, axis=0, keepdims=True)


def _oproj_kernel(tb_ref, o_ref, x_ref, mod_ref, w_ref, gf_ref, wr_ref, x1_ref, h2_ref, aff_ref):
    m = mod_ref[...]
    y = _dot(o_ref[...], w_ref[...])
    x1 = x_ref[...] + m[2:3] * y
    x1_ref[...] = x1
    h2, aff_t = _router(x1, m, gf_ref[...], wr_ref[...])
    h2_ref[...] = h2
    aff_ref[...] = aff_t


def _out_proj(o, x, mod, w_bf, g_ffn, wr_t, groups):
    M = x.shape[0]
    tm = TM_PROJ
    tb, _ = _tile_tables(groups, tm)
    grid_spec = pltpu.PrefetchScalarGridSpec(
        num_scalar_prefetch=1,
        grid=(M // tm,),
        in_specs=[
            pl.BlockSpec((tm, D), lambda i, tb: (i, 0)),
            pl.BlockSpec((tm, D), lambda i, tb: (i, 0)),
            pl.BlockSpec((None, 6, D), lambda i, tb: (tb[i], 0, 0)),
            pl.BlockSpec((D, D), lambda i, tb: (0, 0)),
            pl.BlockSpec((1, D), lambda i, tb: (0, 0)),
            pl.BlockSpec((N_EXPERTS, D), lambda i, tb: (0, 0)),
        ],
        out_specs=[
            pl.BlockSpec((tm, D), lambda i, tb: (i, 0)),
            pl.BlockSpec((tm, D), lambda i, tb: (i, 0)),
            pl.BlockSpec((N_EXPERTS, tm), lambda i, tb: (0, i)),
        ],
    )
    return pl.pallas_call(
        _oproj_kernel,
        grid_spec=grid_spec,
        out_shape=[jax.ShapeDtypeStruct((M, D), F32), jax.ShapeDtypeStruct((M, D), F32),
                   jax.ShapeDtypeStruct((N_EXPERTS, M), F32)],
        compiler_params=_cparams(("parallel",)),
        name="attn_out_router",
    )(tb, o, x, mod, w_bf, g_ffn, wr_t)


def _sgu_kernel(tb_ref, x_ref, mod_ref, g_ref, win_ref, lng_ref, lnb_ref, ws_ref, bs_ref, wout_ref,
                gf_ref, wr_ref, x1_ref, h2_ref, aff_ref, h_scr, v_scr, g_scr):
    m = mod_ref[...]
    x = x_ref[...]
    h_scr[...] = _norm_mod(x, g_ref[...], m[0:1], m[1:2]).astype(BF16)
    tm = x.shape[0]
    nc = 512
    s1 = jnp.zeros((tm, 1), F32)
    s2 = jnp.zeros((tm, 1), F32)
    for c in range(SGU_HALF // nc):
        z = _gelu_tanh(_dot(h_scr[...], win_ref[:, SGU_HALF + c * nc:SGU_HALF + (c + 1) * nc]))
        v_scr[:, c * nc:(c + 1) * nc] = z
        s1 = s1 + jnp.sum(z, axis=-1, keepdims=True)
    mu = s1 * np.float32(1.0 / SGU_HALF)
    for c in range(SGU_HALF // nc):
        d = v_scr[:, c * nc:(c + 1) * nc] - mu
        s2 = s2 + jnp.sum(d * d, axis=-1, keepdims=True)
    rstd = lax.rsqrt(s2 * np.float32(1.0 / SGU_HALF) + NORM_EPS)
    for c in range(SGU_HALF // nc):
        cs = slice(c * nc, (c + 1) * nc)
        v_scr[:, cs] = (v_scr[:, cs] - mu) * rstd * lng_ref[:, cs] + lnb_ref[:, cs]
    for hd in range(SGU_HEADS):
        cs = slice(hd * SGU_HEAD_CH, (hd + 1) * SGU_HEAD_CH)
        u = _gelu_tanh(_dot(h_scr[...], win_ref[:, cs]))
        for k in range(tm // CHUNK):
            rs = slice(k * CHUNK, (k + 1) * CHUNK)
            sv = _dot(ws_ref[hd], v_scr[rs, cs].astype(BF16)) + bs_ref[hd]
            g_scr[rs, cs] = (u[rs] * sv).astype(BF16)
    y = _dot(g_scr[...], wout_ref[...])
    x1 = x + m[2:3] * y
    x1_ref[...] = x1
    h2, aff_t = _router(x1, m, gf_ref[...], wr_ref[...])
    h2_ref[...] = h2
    aff_ref[...] = aff_t


def _sgu_layer(x, mod, g_mix, w_in_bf, ln_g, ln_b, w_s_bf, b_s_col, w_out_bf, g_ffn, wr_t, groups):
    M = x.shape[0]
    tm = TM_SGU
    tb, _ = _tile_tables(groups, tm)
    const = lambda *shape: pl.BlockSpec(shape, lambda i, tb: (0,) * len(shape),
                                        pipeline_mode=pl.Buffered(1))
    grid_spec = pltpu.PrefetchScalarGridSpec(
        num_scalar_prefetch=1,
        grid=(M // tm,),
        in_specs=[
            pl.BlockSpec((tm, D), lambda i, tb: (i, 0)),
            pl.BlockSpec((None, 6, D), lambda i, tb: (tb[i], 0, 0)),
            const(1, D),
            const(D, 2 * SGU_HALF),
            const(1, SGU_HALF),
            const(1, SGU_HALF),
            const(SGU_HEADS, CHUNK, CHUNK),
            const(SGU_HEADS, CHUNK, 1),
            const(SGU_HALF, D),
            const(1, D),
            const(N_EXPERTS, D),
        ],
        out_specs=[
            pl.BlockSpec((tm, D), lambda i, tb: (i, 0)),
            pl.BlockSpec((tm, D), lambda i, tb: (i, 0)),
            pl.BlockSpec((N_EXPERTS, tm), lambda i, tb: (0, i)),
        ],
        scratch_shapes=[pltpu.VMEM((tm, D), BF16), pltpu.VMEM((tm, SGU_HALF), F32),
                        pltpu.VMEM((tm, SGU_HALF), BF16)],
    )
    return pl.pallas_call(
        _sgu_kernel,
        grid_spec=grid_spec,
        out_shape=[jax.ShapeDtypeStruct((M, D), F32), jax.ShapeDtypeStruct((M, D), F32),
                   jax.ShapeDtypeStruct((N_EXPERTS, M), F32)],
        compiler_params=_cparams(("parallel",)),
        name="sgu_router",
    )(tb, x, mod, g_mix, w_in_bf, ln_g, ln_b, w_s_bf, b_s_col, w_out_bf, g_ffn, wr_t)


def _topk_kernel(aff_ref, idx_ref, pos_ref, bst_ref, incl_scr, tot_scr, bex_scr, *, nb, cap,
                 tok_base, slot_base):
    E = N_EXPERTS
    a = aff_ref[...]
    bits = pltpu.bitcast(a, jnp.int32)

    def count(mask):
        c = jnp.sum(jnp.where(mask, 1.0, 0.0), axis=1, keepdims=True)
        return jnp.sum(c, axis=2, keepdims=True)

    def search(k, thr):
        cand = thr | jnp.left_shift(jnp.int32(1), 30 - k)
        return jnp.where(count(bits >= cand) >= cap, cand, thr)

    thr = lax.fori_loop(0, 31, search, jnp.zeros((E, 1, 1), jnp.int32))
    gt = bits > thr
    eq = bits == thr
    need = cap - count(gt)

    row = lax.broadcasted_iota(jnp.int32, (LANES, LANES), 0)
    col = lax.broadcasted_iota(jnp.int32, (LANES, LANES), 1)
    upper = jnp.where(row <= col, 1.0, 0.0).astype(BF16)
    ones = jnp.ones((LANES, LANES), BF16)
    brow = lax.broadcasted_iota(jnp.int32, (nb, nb), 0)
    bcol = lax.broadcasted_iota(jnp.int32, (nb, nb), 1)
    lower_strict = jnp.where(bcol < brow, 1.0, 0.0).astype(BF16)

    def prefix(mask):
        xb = jnp.where(mask, 1.0, 0.0).astype(BF16).reshape(E * nb, LANES)
        incl_scr[...] = _dot(xb, upper).reshape(E, nb, LANES)
        tot_scr[...] = _dot(xb, ones).reshape(E, nb, LANES)
        for e in range(E):
            bex_scr[e] = _dot(lower_strict, tot_scr[e].astype(BF16))

    prefix(eq)
    rank_eq = bex_scr[...] + incl_scr[...] - jnp.where(eq, 1.0, 0.0)
    sel = gt | (eq & (rank_eq < need))
    prefix(sel)
    self32 = jnp.where(sel, 1.0, 0.0)
    pos = bex_scr[...] + incl_scr[...] - self32
    pos_ref[...] = jnp.where(sel, pos + np.float32(slot_base), -1.0)
    bst_ref[...] = bex_scr[...] + np.float32(slot_base)

    sc = SLOT_CHUNK
    bidx = lax.broadcasted_iota(jnp.int32, (nb, sc), 0).astype(F32)
    for e in range(E):
        tot_e = tot_scr[e][:, 0:1]
        binc_e = bex_scr[e][:, 0:1] + tot_e
        incl_t = jnp.transpose(incl_scr[e]).astype(BF16)
        for c in range(cap // sc):
            s = (lax.broadcasted_iota(jnp.int32, (nb, sc), 1) + c * sc).astype(F32)
            before = binc_e <= s
            blk = jnp.sum(jnp.where(before, 1.0, 0.0), axis=0, keepdims=True)
            base = jnp.sum(jnp.where(before, tot_e, 0.0), axis=0, keepdims=True)
            local = s[0:1] - base
            onehot = jnp.where(bidx == blk, 1.0, 0.0).astype(BF16)
            rows = _dot(incl_t, onehot)
            cnt = jnp.sum(jnp.where(rows <= local, 1.0, 0.0), axis=0, keepdims=True)
            tok = blk * np.float32(LANES) + cnt + np.float32(tok_base)
            idx_ref[e:e + 1, c * sc:(c + 1) * sc] = tok.astype(jnp.int32)


def _topk_group(aff_t, tok_base, n, slot_base):
    E = N_EXPERTS
    M = aff_t.shape[1]
    nb = n // LANES
    cap = CAPACITY_FACTOR * n // E
    assert n % LANES == 0 and nb % 8 == 0 and cap % SLOT_CHUNK == 0 and tok_base % n == 0
    a3 = aff_t.reshape(E, M // LANES, LANES)
    blk = tok_base // n
    return pl.pallas_call(
        functools.partial(_topk_kernel, nb=nb, cap=cap, tok_base=tok_base, slot_base=slot_base),
        grid=(1,),
        in_specs=[pl.BlockSpec((E, nb, LANES), lambda i: (0, blk, 0))],
        out_specs=[pl.BlockSpec((E, cap), lambda i: (0, 0)),
                   pl.BlockSpec((E, nb, LANES), lambda i: (0, 0, 0)),
                   pl.BlockSpec((E, nb, LANES), lambda i: (0, 0, 0))],
        out_shape=[jax.ShapeDtypeStruct((E, cap), jnp.int32),
                   jax.ShapeDtypeStruct((E, nb, LANES), F32),
                   jax.ShapeDtypeStruct((E, nb, LANES), F32)],
        scratch_shapes=[pltpu.VMEM((E, nb, LANES), F32)] * 3,
        compiler_params=_cparams(("arbitrary",)),
        name=f"expert_choice_topk_{n}",
    )(a3)


def _ffn_kernel(idx_ref, h_hbm, wg_ref, wu_ref, wd_ref, ye_ref, xbuf, sem, *, tiles_per_expert):
    e = pl.program_id(0)
    j = pl.program_id(1)
    base = (e * tiles_per_expert + j) * FFN_TILE

    def row_copy(k):
        t = idx_ref[base + k]
        return pltpu.make_async_copy(h_hbm.at[pl.ds(t, 1), :], xbuf.at[pl.ds(k, 1), :], sem)

    def issue(k, carry):
        row_copy(k).start()
        return carry

    lax.fori_loop(0, FFN_TILE, issue, 0, unroll=8)

    def drain(k, carry):
        row_copy(k).wait()
        return carry

    lax.fori_loop(0, FFN_TILE, drain, 0, unroll=8)

    x = xbuf[...].astype(BF16)
    acc = jnp.zeros((FFN_TILE, D), F32)
    for c in range(D_EXPERT // F_CHUNK):
        cs = slice(c * F_CHUNK, (c + 1) * F_CHUNK)
        hg = _dot(x, wg_ref[:, cs])
        hu = _dot(x, wu_ref[:, cs])
        act = (hg * _sigmoid(hg) * hu).astype(BF16)
        acc = acc + _dot(act, wd_ref[cs, :])
    ye_ref[...] = acc.astype(BF16)


def _expert_ffn(idx_flat, h2, wg_bf, wu_bf, wd_bf, slots):
    E = N_EXPERTS
    tiles = slots // FFN_TILE
    grid_spec = pltpu.PrefetchScalarGridSpec(
        num_scalar_prefetch=1,
        grid=(E, tiles),
        in_specs=[
            pl.BlockSpec(memory_space=pl.ANY),
            pl.BlockSpec((None, D, D_EXPERT), lambda e, j, idx: (e, 0, 0)),
            pl.BlockSpec((None, D, D_EXPERT), lambda e, j, idx: (e, 0, 0)),
            pl.BlockSpec((None, D_EXPERT, D), lambda e, j, idx: (e, 0, 0)),
        ],
        out_specs=pl.BlockSpec((None, FFN_TILE, D), lambda e, j, idx: (e, j, 0)),
        scratch_shapes=[pltpu.VMEM((FFN_TILE, D), F32), pltpu.SemaphoreType.DMA(())],
    )
    return pl.pallas_call(
        functools.partial(_ffn_kernel, tiles_per_expert=tiles),
        grid_spec=grid_spec,
        out_shape=jax.ShapeDtypeStruct((E, slots, D), BF16),
        compiler_params=_cparams(("arbitrary", "arbitrary")),
        name="expert_ffn",
    )(idx_flat, h2, wg_bf, wu_bf, wd_bf)


def _combine_kernel(tb_ref, bst_ref, x_ref, pos_ref, aff_ref, mod_ref, gfin_ref, ye_hbm, o_ref,
                    win, acc, sem, *, slots, blocks, final_norm):
    E = N_EXPERTS
    i = pl.program_id(0)
    bpt = TM_COMB // LANES
    nblocks_total = blocks

    starts, ends = [], []
    npass = jnp.int32(1)
    for e in range(E):
        st = bst_ref[e * (nblocks_total + 1) + i * bpt]
        en = bst_ref[e * (nblocks_total + 1) + (i + 1) * bpt]
        st = (st // 16) * 16
        starts.append(st)
        ends.append(en)
        npass = jnp.maximum(npass, (en - st + (WIN - 1)) // WIN)

    lane_e = lax.broadcasted_iota(jnp.int32, (1, E), 1)
    erow = lax.broadcasted_iota(jnp.int32, (E, E * WIN), 0)
    ecol = lax.broadcasted_iota(jnp.int32, (E, E * WIN), 1)
    expand = jnp.where(ecol // WIN == erow, 1.0, 0.0).astype(BF16)
    sprime = (lax.broadcasted_iota(jnp.int32, (1, E * WIN), 1) % WIN).astype(F32)

    pos = pos_ref[...]
    gh, gl = _split2(aff_ref[...])
    gate = _dot(gh, expand) + _dot(gl, expand)

    acc[...] = jnp.zeros(acc.shape, F32)

    def one_pass(p, carry):
        lo_vec = jnp.zeros((1, E), F32)
        w0_vec = jnp.zeros((1, E), F32)
        copies = []
        for e in range(E):
            lo = starts[e] + p * WIN
            w0 = jnp.minimum(lo, slots - WIN)
            w0 = pl.multiple_of(w0, 16)
            lo_vec = jnp.where(lane_e == e, lo.astype(F32), lo_vec)
            w0_vec = jnp.where(lane_e == e, w0.astype(F32), w0_vec)
            cp = pltpu.make_async_copy(ye_hbm.at[e, pl.ds(w0, WIN), :],
                                       win.at[pl.ds(e * WIN, WIN), :], sem.at[e])
            cp.start()
            copies.append(cp)
        rel = jnp.where(pos >= lo_vec, pos - w0_vec, -1.0)
        rel = jnp.clip(rel, -1.0, np.float32(WIN))
        relx = _dot(rel.astype(BF16), expand)
        pm = jnp.where(relx == sprime, gate, 0.0)
        ph, plo = _split2(pm)
        for cp in copies:
            cp.wait()
        w = win[...]
        acc[...] += _dot(ph, w) + _dot(plo, w)
        return carry

    lax.fori_loop(0, npass, one_pass, 0)

    m = mod_ref[...]
    y = x_ref[...] + m[5:6] * acc[...]
    if final_norm:
        ms = jnp.mean(y * y, axis=-1, keepdims=True)
        y = y * lax.rsqrt(ms + NORM_EPS) * gfin_ref[...]
    o_ref[...] = y


def _combine(x1, pos_t, aff_tok, mod, g_final, ye, bst_flat, groups, final_norm):
    M = x1.shape[0]
    tm = TM_COMB
    E = N_EXPERTS
    slots = ye.shape[1]
    tb, _ = _tile_tables(groups, tm)
    grid_spec = pltpu.PrefetchScalarGridSpec(
        num_scalar_prefetch=2,
        grid=(M // tm,),
        in_specs=[
            pl.BlockSpec((tm, D), lambda i, tb, bst: (i, 0)),
            pl.BlockSpec((tm, E), lambda i, tb, bst: (i, 0)),
            pl.BlockSpec((tm, E), lambda i, tb, bst: (i, 0)),
            pl.BlockSpec((None, 6, D), lambda i, tb, bst: (tb[i], 0, 0)),
            pl.BlockSpec((1, D), lambda i, tb, bst: (0, 0)),
            pl.BlockSpec(memory_space=pl.ANY),
        ],
        out_specs=pl.BlockSpec((tm, D), lambda i, tb, bst: (i, 0)),
        scratch_shapes=[pltpu.VMEM((E * WIN, D), BF16), pltpu.VMEM((tm, D), F32),
                        pltpu.SemaphoreType.DMA((E,))],
    )
    return pl.pallas_call(
        functools.partial(_combine_kernel, slots=slots, blocks=M // LANES, final_norm=final_norm),
        grid_spec=grid_spec,
        out_shape=jax.ShapeDtypeStruct((M, D), F32),
        compiler_params=_cparams(("arbitrary",)),
        name="moe_combine",
    )(tb, bst_flat, x1, pos_t, aff_tok, mod, g_final, ye)


def _moe(x1, h2, aff_t, mod, g_final, wg_bf, wu_bf, wd_bf, groups, final_norm):
    E = N_EXPERTS
    idxs, poss, bsts = [], [], []
    tok_base = 0
    slot_base = 0
    for (B, S) in groups:
        n = B * S
        idx, pos, bst = _topk_group(aff_t, tok_base, n, slot_base)
        idxs.append(idx)
        poss.append(pos.reshape(E, n))
        bsts.append(bst[:, :, 0])
        tok_base += n
        slot_base += CAPACITY_FACTOR * n // E
    slots = slot_base
    idx_flat = jnp.concatenate(idxs, axis=1).reshape(-1)
    ye = _expert_ffn(idx_flat, h2, wg_bf, wu_bf, wd_bf, slots)
    pos_t = jnp.concatenate(poss, axis=1).T
    bst = jnp.concatenate(bsts + [jnp.full((E, 1), slots, F32)], axis=1)
    bst_flat = bst.astype(jnp.int32).reshape(-1)
    return _combine(x1, pos_t, aff_t.T, mod, g_final, ye, bst_flat, groups, final_norm)


def kernel(x_prompt, x_sample, c_prompt, c_sample, w_mod, b_mod, g_mix, g_ffn, a_w_in, a_w_out, b_w_in,
           b_ln_g, b_ln_b, b_w_s, b_b_s, b_w_out, moe_w_router, moe_w_gate, moe_w_up, moe_w_down, g_final):
    groups = (x_prompt.shape[:2], x_sample.shape[:2])
    assert x_prompt.shape[2] == D and x_sample.shape[2] == D
    x = jnp.concatenate([x_prompt.reshape(-1, D), x_sample.reshape(-1, D)], axis=0)
    n_seq = groups[0][0] + groups[1][0]
    rows = -(-n_seq // 8) * 8
    c_all = jnp.concatenate([c_prompt, c_sample, jnp.zeros((rows - n_seq, D), F32)], axis=0)
    mod = _modulation(c_all, w_mod, b_mod).reshape(w_mod.shape[0], rows, 6, D)
    cos_t, sin_t = _rope_tables(max(groups[0][1], groups[1][1]))
    wr_t = jnp.swapaxes(moe_w_router, 1, 2)
    gfin = g_final.reshape(1, D)

    qkv = _qkv_proj(x, mod[0], g_mix[0].reshape(1, D), a_w_in[0].astype(BF16), cos_t, sin_t, groups)
    prev = None
    for g in range(N_GROUPS):
        prev = _attention_group(qkv, g, groups, prev)
    x1, h2, aff_t = _out_proj(prev[0], x, mod[0], a_w_out[0].astype(BF16), g_ffn[0].reshape(1, D),
                              wr_t[0], groups)
    x = _moe(x1, h2, aff_t, mod[0], gfin, moe_w_gate[0].astype(BF16), moe_w_up[0].astype(BF16),
             moe_w_down[0].astype(BF16), groups, final_norm=False)

    x1, h2, aff_t = _sgu_layer(x, mod[1], g_mix[1].reshape(1, D), b_w_in[0].astype(BF16),
                               b_ln_g[0].reshape(1, SGU_HALF), b_ln_b[0].reshape(1, SGU_HALF),
                               b_w_s[0].astype(BF16), b_b_s[0].reshape(SGU_HEADS, CHUNK, 1),
                               b_w_out[0].astype(BF16), g_ffn[1].reshape(1, D), wr_t[1], groups)
    y = _moe(x1, h2, aff_t, mod[1], gfin, moe_w_gate[1].astype(BF16), moe_w_up[1].astype(BF16),
             moe_w_down[1].astype(BF16), groups, final_norm=True)

    n1 = groups[0][0] * groups[0][1]
    return (y[:n1].reshape(x_prompt.shape), y[n1:].reshape(x_sample.shape))
```

```python
import functools

import numpy as np
import jax
import jax.numpy as jnp
from jax import lax
from jax.experimental import pallas as pl
from jax.experimental.pallas import tpu as pltpu

F32 = jnp.float32
BF16 = jnp.bfloat16

D = 1024
N_HEADS = 16
HEAD_DIM = 64
DILATIONS = (1, 4, 16)
WINDOWS = (128, 512, 2048)
RADIUS = 64
N_GROUPS = 3
QKV_COLS = N_GROUPS * 3 * D
ROPE_THETA = 10000.0
CHUNK = 128
SGU_HALF = 3 * D
SGU_HEADS = 16
SGU_HEAD_CH = SGU_HALF // SGU_HEADS
N_EXPERTS = 16
CAPACITY_FACTOR = 2
D_EXPERT = 2 * D
NORM_EPS = 1e-6
NEG_BIG = -1e30

LANES = 128
VMEM_LIMIT = 56 * 1024 * 1024

TM_PROJ = 512
TM_SGU = 256
TM_COMB = 256
TQ = 256
UQ = 128
FFN_TILE = 256
F_CHUNK = 512
WIN = 64
SLOT_CHUNK = 512


def _cparams(sem):
    return pltpu.CompilerParams(dimension_semantics=sem, vmem_limit_bytes=VMEM_LIMIT)


def _split2(x):
    hi = x.astype(BF16)
    lo = (x - hi.astype(F32)).astype(BF16)
    return hi, lo


def _dot(a, b):
    return jnp.dot(a, b, preferred_element_type=F32)


def _dot_nt(a, b):
    return lax.dot_general(a, b, (((1,), (1,)), ((), ())), preferred_element_type=F32)


def _dot3(a, b):
    ah, al = _split2(a)
    bh, bl = _split2(b)
    return _dot(ah, bh) + (_dot(ah, bl) + _dot(al, bh))


def _dot3_nt(a, b):
    ah, al = _split2(a)
    bh, bl = _split2(b)
    return _dot_nt(ah, bh) + (_dot_nt(ah, bl) + _dot_nt(al, bh))


def _sigmoid(x):
    return 1.0 / (1.0 + jnp.exp(-x))


def _gelu_tanh(x):
    c = np.float32(np.sqrt(2.0 / np.pi))
    return 0.5 * x * (1.0 + jnp.tanh(c * (x + np.float32(0.044715) * (x * x * x))))


def _norm_mod(x, g, shift, scale):
    ms = jnp.mean(x * x, axis=-1, keepdims=True)
    return (x * lax.rsqrt(ms + NORM_EPS) * g) * (1.0 + scale) + shift


def _tile_tables(groups, tm):
    tb, tp = [], []
    b0 = 0
    for (B, S) in groups:
        assert S % tm == 0
        for b in range(B):
            for k in range(S // tm):
                tb.append(b0 + b)
                tp.append(k)
        b0 += B
    return jnp.asarray(np.array(tb, np.int32)), jnp.asarray(np.array(tp, np.int32))


def _mod_kernel(c_ref, w_ref, b_ref, o_ref):
    c = c_ref[...]
    o_ref[...] = _dot3(c * _sigmoid(c), w_ref[...]) + b_ref[...]


def _modulation(c_pad, w_mod, b_mod):
    depth, _, n = w_mod.shape
    tn = 2048
    rows = c_pad.shape[0]
    return pl.pallas_call(
        _mod_kernel,
        grid=(depth, n // tn),
        in_specs=[
            pl.BlockSpec((rows, D), lambda l, j: (0, 0)),
            pl.BlockSpec((None, D, tn), lambda l, j: (l, 0, j)),
            pl.BlockSpec((None, 1, tn), lambda l, j: (l, 0, j)),
        ],
        out_specs=pl.BlockSpec((None, rows, tn), lambda l, j: (l, 0, j)),
        out_shape=jax.ShapeDtypeStruct((depth, rows, n), F32),
        compiler_params=_cparams(("parallel", "parallel")),
        name="modulation",
    )(c_pad, w_mod, b_mod.reshape(depth, 1, n))


def _permute_rows(src_ref, slab, dst, r):
    tm, cols = src_ref.shape
    n = tm // r
    for c in range(cols // LANES):
        slab[c] = src_ref[:, c * LANES:(c + 1) * LANES]
    for res in range(r):
        for c in range(cols // LANES):
            dst[res * n:(res + 1) * n, c * LANES:(c + 1) * LANES] = slab[c, pl.ds(res, n, stride=r), :]


def _qkv_kernel(tb_ref, tp_ref, x_ref, mod_ref, g_ref, w_ref, cos_ref, sin_ref, o_ref, h_scr, *scr, r):
    j = pl.program_id(1)

    @pl.when(j == 0)
    def _():
        m = mod_ref[...]
        if r == 1:
            x = x_ref[...]
        else:
            slab, xp = scr
            _permute_rows(x_ref, slab, xp, r)
            x = xp[...]
        h_scr[...] = _norm_mod(x, g_ref[...], m[0:1], m[1:2]).astype(BF16)

    n = h_scr.shape[0] // r
    scale = jnp.where(j == 0, np.float32(HEAD_DIM ** -0.5), np.float32(1.0))
    cos = jnp.where(j == 2, np.float32(1.0), cos_ref[...] * scale)
    sin = jnp.where(j == 2, np.float32(0.0), sin_ref[...] * scale)
    lane = lax.broadcasted_iota(jnp.int32, cos.shape, 1)
    first = (lane % HEAD_DIM) < (HEAD_DIM // 2)
    nw = 2 * LANES
    for cw in range(D // nw):
        acc = _dot(h_scr[...], w_ref[:, cw * nw:(cw + 1) * nw])
        for c in range(nw // LANES):
            xc = acc[:, c * LANES:(c + 1) * LANES]
            partner = jnp.where(first, pltpu.roll(xc, LANES - HEAD_DIM // 2, 1),
                                pltpu.roll(xc, HEAD_DIM // 2, 1))
            val = xc * cos + partner * sin
            cs = slice(cw * nw + c * LANES, cw * nw + (c + 1) * LANES)
            o_ref[:, :, cs] = val.reshape(r, n, LANES).astype(BF16)


def _qkv_proj(x, mod, g, w_bf, cos_p, sin_p, groups, r):
    M = x.shape[0]
    tm = TM_PROJ
    tb, tp = _tile_tables(groups, tm)
    scratch = [pltpu.VMEM((tm, D), BF16)]
    if r > 1:
        scratch += [pltpu.VMEM((D // LANES, tm, LANES), F32), pltpu.VMEM((tm, D), F32)]
    grid_spec = pltpu.PrefetchScalarGridSpec(
        num_scalar_prefetch=2,
        grid=(M // tm, 3),
        in_specs=[
            pl.BlockSpec((tm, D), lambda i, j, tb, tp: (i, 0)),
            pl.BlockSpec((None, 6, D), lambda i, j, tb, tp: (tb[i], 0, 0)),
            pl.BlockSpec((1, D), lambda i, j, tb, tp: (0, 0)),
            pl.BlockSpec((D, D), lambda i, j, tb, tp: (0, j)),
            pl.BlockSpec((tm, LANES), lambda i, j, tb, tp: (tp[i], 0)),
            pl.BlockSpec((tm, LANES), lambda i, j, tb, tp: (tp[i], 0)),
        ],
        out_specs=pl.BlockSpec((r, tm // r, D), lambda i, j, tb, tp: (0, i, j)),
        scratch_shapes=scratch,
    )
    return pl.pallas_call(
        functools.partial(_qkv_kernel, r=r),
        grid_spec=grid_spec,
        out_shape=jax.ShapeDtypeStruct((r, M // r, 3 * D), BF16),
        compiler_params=_cparams(("parallel", "arbitrary")),
        name=f"qkv_rope_d{r}",
    )(tb, tp, x, mod, g, w_bf, cos_p, sin_p)


def _rope_tables(s_max):
    inv = ROPE_THETA ** (-jnp.arange(0, HEAD_DIM, 2, dtype=F32) / HEAD_DIM)
    ang = jnp.arange(s_max, dtype=F32)[:, None] * inv[None, :]
    cos, sin = jnp.cos(ang), jnp.sin(ang)
    cos_t = jnp.tile(cos, (1, LANES // (HEAD_DIM // 2)))
    sin_t = jnp.tile(jnp.concatenate([-sin, sin], axis=1), (1, LANES // HEAD_DIM))
    return cos_t, sin_t


def _permute_table(t, tm, r):
    s = t.shape[0]
    return t.reshape(s // tm, tm // r, r, LANES).transpose(0, 2, 1, 3).reshape(s, LANES)


def _attn_kernel(q_ref, kp_ref, kc_ref, kn_ref, vp_ref, vc_ref, vn_ref, o_ref, l_ref,
                 kw, vw, s_scr, p_scr, *, seq_blocks):
    i = pl.program_id(1)
    (n1, nb1), (n2, nb2) = seq_blocks
    first_grp = i < n1 * nb1
    pos = jnp.where(first_grp, i % nb1, (i - n1 * nb1) % nb2)
    nb = jnp.where(first_grp, nb1, nb2)
    has_prev = pos > 0
    has_next = pos < nb - 1

    W = RADIUS
    kw[0:W] = kp_ref[...]
    kw[W:W + TQ] = kc_ref[...]
    kw[W + TQ:W + TQ + W] = kn_ref[...]
    vw[0:W] = vp_ref[...]
    vw[W:W + TQ] = vc_ref[...]
    vw[W + TQ:W + TQ + W] = vn_ref[...]

    lane = lax.broadcasted_iota(jnp.int32, (UQ, LANES), 1)
    low_half = lane < HEAD_DIM
    nu = TQ // UQ
    for u in range(nu):
        for hp in range(N_HEADS // 2):
            cs = slice(hp * LANES, (hp + 1) * LANES)
            q2 = q_ref[u * UQ:(u + 1) * UQ, cs]
            kwin = kw[u * UQ:u * UQ + 2 * UQ, cs]
            zero = jnp.zeros_like(q2)
            s_scr[u, 2 * hp] = _dot_nt(jnp.where(low_half, q2, zero), kwin)
            s_scr[u, 2 * hp + 1] = _dot_nt(jnp.where(low_half, zero, q2), kwin)

    rr = lax.broadcasted_iota(jnp.int32, (UQ, 2 * UQ), 0)
    cc = lax.broadcasted_iota(jnp.int32, (UQ, 2 * UQ), 1)
    band = jnp.abs((cc - W) - rr) <= RADIUS
    l_ref[...] = jnp.zeros(l_ref.shape, F32)
    for u in range(nu):
        valid = band
        if u == 0:
            valid = valid & ((cc >= W) | has_prev)
        if u == nu - 1:
            valid = valid & ((cc < 2 * UQ - W) | has_next)
        s = jnp.where(valid[None], s_scr[u], NEG_BIG)
        m = jnp.max(s, axis=-1, keepdims=True)
        p = jnp.exp(s - m)
        den = jnp.sum(p, axis=-1, keepdims=True)
        p_scr[u] = p.astype(BF16)
        rden = 1.0 / den
        lse = m + jnp.log(den)
        rows = slice(u * UQ, (u + 1) * UQ)
        for hp in range(N_HEADS // 2):
            cs = slice(hp * LANES, (hp + 1) * LANES)
            vwin = vw[u * UQ:u * UQ + 2 * UQ, cs]
            oa = _dot(p_scr[u, 2 * hp], vwin) * rden[2 * hp]
            ob = _dot(p_scr[u, 2 * hp + 1], vwin) * rden[2 * hp + 1]
            o_ref[rows, cs] = jnp.where(low_half, oa, ob).astype(BF16)
        for h in range(N_HEADS):
            l_ref[rows, h:h + 1] = lse[h]


def _attention_group(qkv_g, groups):
    r, rows, _ = qkv_g.shape
    W = RADIUS
    for (B, S) in groups:
        assert S % (r * TQ) == 0
    seq_blocks = tuple((B, S // r // TQ) for (B, S) in groups)
    q64 = TQ // W
    last64 = rows // W - 1

    def big(col):
        return pl.BlockSpec((None, TQ, D), lambda res, i: (res, i, col))

    def prv(col):
        return pl.BlockSpec((None, W, D), lambda res, i: (res, jnp.maximum(i * q64 - 1, 0), col))

    def nxt(col):
        return pl.BlockSpec((None, W, D), lambda res, i: (res, jnp.minimum((i + 1) * q64, last64), col))

    return pl.pallas_call(
        functools.partial(_attn_kernel, seq_blocks=seq_blocks),
        grid=(r, rows // TQ),
        in_specs=[big(0), prv(1), big(1), nxt(1), prv(2), big(2), nxt(2)],
        out_specs=[pl.BlockSpec((None, TQ, D), lambda res, i: (res, i, 0)),
                   pl.BlockSpec((None, TQ, LANES), lambda res, i: (res, i, 0))],
        out_shape=[jax.ShapeDtypeStruct((r, rows, D), BF16),
                   jax.ShapeDtypeStruct((r, rows, LANES), F32)],
        scratch_shapes=[pltpu.VMEM((TQ + 2 * W, D), BF16), pltpu.VMEM((TQ + 2 * W, D), BF16),
                        pltpu.VMEM((TQ // UQ, N_HEADS, UQ, 2 * UQ), F32),
                        pltpu.VMEM((TQ // UQ, N_HEADS, UQ, 2 * UQ), BF16)],
        compiler_params=_cparams(("parallel", "parallel")),
        name=f"attn_d{r}",
    )(*([qkv_g] * 7))


def _router(x1, m, g_ffn, wr_t):
    h2 = _norm_mod(x1, g_ffn, m[3:4], m[4:5])
    logits_t = _dot3_nt(wr_t, h2)
    mx = jnp.max(logits_t, axis=0, keepdims=True)
    e = jnp.exp(logits_t - mx)
    return h2, e / jnp.sum(e, axis=0, keepdims=True)


def _oproj_kernel(tb_ref, o1_ref, o4_ref, o16_ref, l1_ref, l4_ref, l16_ref, x_ref, mod_ref, w_ref,
                  gf_ref, wr_ref, x1_ref, h2_ref, aff_ref, oslab, lslab, om_scr):
    tm = x_ref.shape[0]
    nc = D // LANES
    for g, (r, o_ref, l_ref) in enumerate(zip(DILATIONS, (o1_ref, o4_ref, o16_ref),
                                              (l1_ref, l4_ref, l16_ref))):
        n = tm // r
        for res in range(r):
            rows = slice(None) if r == 1 else pl.ds(res, n, stride=r)
            lslab[g, 0, rows, :] = l_ref[res]
            for c in range(nc):
                oslab[g, c, rows, :] = o_ref[res, :, c * LANES:(c + 1) * LANES].astype(F32)
    ls = [lslab[g, 0] for g in range(N_GROUPS)]
    mx = jnp.maximum(jnp.maximum(ls[0], ls[1]), ls[2])
    es = [jnp.exp(l - mx) for l in ls]
    tot = es[0] + es[1] + es[2]
    hrow = lax.broadcasted_iota(jnp.int32, (LANES, D), 0)
    hcol = lax.broadcasted_iota(jnp.int32, (LANES, D), 1)
    expand = jnp.where(hcol // HEAD_DIM == hrow, 1.0, 0.0).astype(BF16)
    wexp = []
    for g in range(N_GROUPS):
        wh, wl = _split2(es[g] / tot)
        wexp.append(_dot(wh, expand) + _dot(wl, expand))
    for c in range(nc):
        cs = slice(c * LANES, (c + 1) * LANES)
        om = wexp[0][:, cs] * oslab[0, c] + wexp[1][:, cs] * oslab[1, c] + wexp[2][:, cs] * oslab[2, c]
        om_scr[:, cs] = om.astype(BF16)
    m = mod_ref[...]
    y = _dot(om_scr[...], w_ref[...])
    x1 = x_ref[...] + m[2:3] * y
    x1_ref[...] = x1
    h2, aff_t = _router(x1, m, gf_ref[...], wr_ref[...])
    h2_ref[...] = h2
    aff_ref[...] = aff_t


def _out_proj(os_, ls_, x, mod, w_bf, g_ffn, wr_t, groups):
    M = x.shape[0]
    tm = TM_PROJ
    tb, _ = _tile_tables(groups, tm)
    o_specs = [pl.BlockSpec((r, tm // r, D), lambda i, tb: (0, i, 0)) for r in DILATIONS]
    l_specs = [pl.BlockSpec((r, tm // r, LANES), lambda i, tb: (0, i, 0)) for r in DILATIONS]
    grid_spec = pltpu.PrefetchScalarGridSpec(
        num_scalar_prefetch=1,
        grid=(M // tm,),
        in_specs=o_specs + l_specs + [
            pl.BlockSpec((tm, D), lambda i, tb: (i, 0)),
            pl.BlockSpec((None, 6, D), lambda i, tb: (tb[i], 0, 0)),
            pl.BlockSpec((D, D), lambda i, tb: (0, 0)),
            pl.BlockSpec((1, D), lambda i, tb: (0, 0)),
            pl.BlockSpec((N_EXPERTS, D), lambda i, tb: (0, 0)),
        ],
        out_specs=[
            pl.BlockSpec((tm, D), lambda i, tb: (i, 0)),
            pl.BlockSpec((tm, D), lambda i, tb: (i, 0)),
            pl.BlockSpec((N_EXPERTS, tm), lambda i, tb: (0, i)),
        ],
        scratch_shapes=[pltpu.VMEM((N_GROUPS, D // LANES, tm, LANES), F32),
                        pltpu.VMEM((N_GROUPS, 1, tm, LANES), F32),
                        pltpu.VMEM((tm, D), BF16)],
    )
    return pl.pallas_call(
        _oproj_kernel,
        grid_spec=grid_spec,
        out_shape=[jax.ShapeDtypeStruct((M, D), F32), jax.ShapeDtypeStruct((M, D), F32),
                   jax.ShapeDtypeStruct((N_EXPERTS, M), F32)],
        compiler_params=_cparams(("parallel",)),
        name="attn_out_router",
    )(tb, *os_, *ls_, x, mod, w_bf, g_ffn, wr_t)


def _sgu_kernel(tb_ref, x_ref, mod_ref, g_ref, win_ref, lng_ref, lnb_ref, ws_ref, bs_ref, wout_ref,
                gf_ref, wr_ref, x1_ref, h2_ref, aff_ref, h_scr, v_scr, g_scr):
    m = mod_ref[...]
    x = x_ref[...]
    h_scr[...] = _norm_mod(x, g_ref[...], m[0:1], m[1:2]).astype(BF16)
    tm = x.shape[0]
    nc = 512
    s1 = jnp.zeros((tm, 1), F32)
    s2 = jnp.zeros((tm, 1), F32)
    for c in range(SGU_HALF // nc):
        z = _gelu_tanh(_dot(h_scr[...], win_ref[:, SGU_HALF + c * nc:SGU_HALF + (c + 1) * nc]))
        v_scr[:, c * nc:(c + 1) * nc] = z
        s1 = s1 + jnp.sum(z, axis=-1, keepdims=True)
    mu = s1 * np.float32(1.0 / SGU_HALF)
    for c in range(SGU_HALF // nc):
        d = v_scr[:, c * nc:(c + 1) * nc] - mu
        s2 = s2 + jnp.sum(d * d, axis=-1, keepdims=True)
    rstd = lax.rsqrt(s2 * np.float32(1.0 / SGU_HALF) + NORM_EPS)
    for c in range(SGU_HALF // nc):
        cs = slice(c * nc, (c + 1) * nc)
        v_scr[:, cs] = (v_scr[:, cs] - mu) * rstd * lng_ref[:, cs] + lnb_ref[:, cs]
    for hd in range(SGU_HEADS):
        cs = slice(hd * SGU_HEAD_CH, (hd + 1) * SGU_HEAD_CH)
        u = _gelu_tanh(_dot(h_scr[...], win_ref[:, cs]))
        for k in range(tm // CHUNK):
            rs = slice(k * CHUNK, (k + 1) * CHUNK)
            sv = _dot(ws_ref[hd], v_scr[rs, cs].astype(BF16)) + bs_ref[hd]
            g_scr[rs, cs] = (u[rs] * sv).astype(BF16)
    y = _dot(g_scr[...], wout_ref[...])
    x1 = x + m[2:3] * y
    x1_ref[...] = x1
    h2, aff_t = _router(x1, m, gf_ref[...], wr_ref[...])
    h2_ref[...] = h2
    aff_ref[...] = aff_t


def _sgu_layer(x, mod, g_mix, w_in_bf, ln_g, ln_b, w_s_bf, b_s_col, w_out_bf, g_ffn, wr_t, groups):
    M = x.shape[0]
    tm = TM_SGU
    tb, _ = _tile_tables(groups, tm)
    const = lambda *shape: pl.BlockSpec(shape, lambda i, tb: (0,) * len(shape),
                                        pipeline_mode=pl.Buffered(1))
    grid_spec = pltpu.PrefetchScalarGridSpec(
        num_scalar_prefetch=1,
        grid=(M // tm,),
        in_specs=[
            pl.BlockSpec((tm, D), lambda i, tb: (i, 0)),
            pl.BlockSpec((None, 6, D), lambda i, tb: (tb[i], 0, 0)),
            const(1, D),
            const(D, 2 * SGU_HALF),
            const(1, SGU_HALF),
            const(1, SGU_HALF),
            const(SGU_HEADS, CHUNK, CHUNK),
            const(SGU_HEADS, CHUNK, 1),
            const(SGU_HALF, D),
            const(1, D),
            const(N_EXPERTS, D),
        ],
        out_specs=[
            pl.BlockSpec((tm, D), lambda i, tb: (i, 0)),
            pl.BlockSpec((tm, D), lambda i, tb: (i, 0)),
            pl.BlockSpec((N_EXPERTS, tm), lambda i, tb: (0, i)),
        ],
        scratch_shapes=[pltpu.VMEM((tm, D), BF16), pltpu.VMEM((tm, SGU_HALF), F32),
                        pltpu.VMEM((tm, SGU_HALF), BF16)],
    )
    return pl.pallas_call(
        _sgu_kernel,
        grid_spec=grid_spec,
        out_shape=[jax.ShapeDtypeStruct((M, D), F32), jax.ShapeDtypeStruct((M, D), F32),
                   jax.ShapeDtypeStruct((N_EXPERTS, M), F32)],
        compiler_params=_cparams(("parallel",)),
        name="sgu_router",
    )(tb, x, mod, g_mix, w_in_bf, ln_g, ln_b, w_s_bf, b_s_col, w_out_bf, g_ffn, wr_t)


def _topk_kernel(aff_ref, idx_ref, pos_ref, bst_ref, incl_scr, tot_scr, bex_scr, *, nb, cap,
                 tok_base, slot_base):
    E = N_EXPERTS
    a = aff_ref[...]
    bits = pltpu.bitcast(a, jnp.int32)

    def count(mask):
        c = jnp.sum(jnp.where(mask, 1.0, 0.0), axis=1, keepdims=True)
        return jnp.sum(c, axis=2, keepdims=True)

    def search(k, thr):
        cand = thr | jnp.left_shift(jnp.int32(1), 30 - k)
        return jnp.where(count(bits >= cand) >= cap, cand, thr)

    thr = lax.fori_loop(0, 31, search, jnp.zeros((E, 1, 1), jnp.int32))
    gt = bits > thr
    eq = bits == thr
    need = cap - count(gt)

    row = lax.broadcasted_iota(jnp.int32, (LANES, LANES), 0)
    col = lax.broadcasted_iota(jnp.int32, (LANES, LANES), 1)
    upper = jnp.where(row <= col, 1.0, 0.0).astype(BF16)
    ones = jnp.ones((LANES, LANES), BF16)
    brow = lax.broadcasted_iota(jnp.int32, (nb, nb), 0)
    bcol = lax.broadcasted_iota(jnp.int32, (nb, nb), 1)
    lower_strict = jnp.where(bcol < brow, 1.0, 0.0).astype(BF16)

    def prefix(mask):
        xb = jnp.where(mask, 1.0, 0.0).astype(BF16).reshape(E * nb, LANES)
        incl_scr[...] = _dot(xb, upper).reshape(E, nb, LANES)
        tot_scr[...] = _dot(xb, ones).reshape(E, nb, LANES)
        for e in range(E):
            bex_scr[e] = _dot(lower_strict, tot_scr[e].astype(BF16))

    prefix(eq)
    rank_eq = bex_scr[...] + incl_scr[...] - jnp.where(eq, 1.0, 0.0)
    sel = gt | (eq & (rank_eq < need))
    prefix(sel)
    self32 = jnp.where(sel, 1.0, 0.0)
    pos = bex_scr[...] + incl_scr[...] - self32
    pos_ref[...] = jnp.where(sel, pos + np.float32(slot_base), -1.0)
    bst_ref[...] = bex_scr[...] + np.float32(slot_base)

    sc = SLOT_CHUNK
    bidx = lax.broadcasted_iota(jnp.int32, (nb, sc), 0).astype(F32)
    for e in range(E):
        tot_e = tot_scr[e][:, 0:1]
        binc_e = bex_scr[e][:, 0:1] + tot_e
        incl_t = jnp.transpose(incl_scr[e]).astype(BF16)
        for c in range(cap // sc):
            s = (lax.broadcasted_iota(jnp.int32, (nb, sc), 1) + c * sc).astype(F32)
            before = binc_e <= s
            blk = jnp.sum(jnp.where(before, 1.0, 0.0), axis=0, keepdims=True)
            base = jnp.sum(jnp.where(before, tot_e, 0.0), axis=0, keepdims=True)
            local = s[0:1] - base
            onehot = jnp.where(bidx == blk, 1.0, 0.0).astype(BF16)
            rows = _dot(incl_t, onehot)
            cnt = jnp.sum(jnp.where(rows <= local, 1.0, 0.0), axis=0, keepdims=True)
            tok = blk * np.float32(LANES) + cnt + np.float32(tok_base)
            idx_ref[e:e + 1, c * sc:(c + 1) * sc] = tok.astype(jnp.int32)


def _topk_group(aff_t, tok_base, n, slot_base):
    E = N_EXPERTS
    M = aff_t.shape[1]
    nb = n // LANES
    cap = CAPACITY_FACTOR * n // E
    assert n % LANES == 0 and nb % 8 == 0 and cap % SLOT_CHUNK == 0 and tok_base % n == 0
    a3 = aff_t.reshape(E, M // LANES, LANES)
    blk = tok_base // n
    return pl.pallas_call(
        functools.partial(_topk_kernel, nb=nb, cap=cap, tok_base=tok_base, slot_base=slot_base),
        grid=(1,),
        in_specs=[pl.BlockSpec((E, nb, LANES), lambda i: (0, blk, 0))],
        out_specs=[pl.BlockSpec((E, cap), lambda i: (0, 0)),
                   pl.BlockSpec((E, nb, LANES), lambda i: (0, 0, 0)),
                   pl.BlockSpec((E, nb, LANES), lambda i: (0, 0, 0))],
        out_shape=[jax.ShapeDtypeStruct((E, cap), jnp.int32),
                   jax.ShapeDtypeStruct((E, nb, LANES), F32),
                   jax.ShapeDtypeStruct((E, nb, LANES), F32)],
        scratch_shapes=[pltpu.VMEM((E, nb, LANES), F32)] * 3,
        compiler_params=_cparams(("arbitrary",)),
        name=f"expert_choice_topk_{n}",
    )(a3)


def _ffn_kernel(idx_ref, h_hbm, wg_ref, wu_ref, wd_ref, ye_ref, xbuf, sem, *, tiles_per_expert, n_steps):
    step = pl.program_id(0) * tiles_per_expert + pl.program_id(1)
    slot = step % 2

    def row_copy(tile, k, slot_):
        t = idx_ref[tile * FFN_TILE + k]
        return pltpu.make_async_copy(h_hbm.at[pl.ds(t, 1), :], xbuf.at[slot_, pl.ds(k, 1), :],
                                     sem.at[slot_])

    @pl.when(step == 0)
    def _():
        def issue(k, carry):
            row_copy(0, k, 0).start()
            return carry
        lax.fori_loop(0, FFN_TILE, issue, 0, unroll=8)

    for k in range(FFN_TILE):
        row_copy(step, k, slot).wait()
    x = xbuf[slot].astype(BF16)
    nxt = jnp.minimum(step + 1, n_steps - 1)
    for k in range(FFN_TILE):
        row_copy(nxt, k, 1 - slot).start()

    acc = jnp.zeros((FFN_TILE, D), F32)
    for c in range(D_EXPERT // F_CHUNK):
        cs = slice(c * F_CHUNK, (c + 1) * F_CHUNK)
        hg = _dot(x, wg_ref[:, cs])
        hu = _dot(x, wu_ref[:, cs])
        act = (hg * _sigmoid(hg) * hu).astype(BF16)
        acc = acc + _dot(act, wd_ref[cs, :])
    ye_ref[...] = acc.astype(BF16)

    @pl.when(step == n_steps - 1)
    def _():
        for k in range(FFN_TILE):
            row_copy(nxt, k, 1 - slot).wait()


def _expert_ffn(idx_flat, h2, wg_bf, wu_bf, wd_bf, slots):
    E = N_EXPERTS
    tiles = slots // FFN_TILE
    grid_spec = pltpu.PrefetchScalarGridSpec(
        num_scalar_prefetch=1,
        grid=(E, tiles),
        in_specs=[
            pl.BlockSpec(memory_space=pl.ANY),
            pl.BlockSpec((None, D, D_EXPERT), lambda e, j, idx: (e, 0, 0)),
            pl.BlockSpec((None, D, D_EXPERT), lambda e, j, idx: (e, 0, 0)),
            pl.BlockSpec((None, D_EXPERT, D), lambda e, j, idx: (e, 0, 0)),
        ],
        out_specs=pl.BlockSpec((None, FFN_TILE, D), lambda e, j, idx: (e, j, 0)),
        scratch_shapes=[pltpu.VMEM((2, FFN_TILE, D), F32), pltpu.SemaphoreType.DMA((2,))],
    )
    return pl.pallas_call(
        functools.partial(_ffn_kernel, tiles_per_expert=tiles, n_steps=E * tiles),
        grid_spec=grid_spec,
        out_shape=jax.ShapeDtypeStruct((E, slots, D), BF16),
        compiler_params=_cparams(("arbitrary", "arbitrary")),
        name="expert_ffn",
    )(idx_flat, h2, wg_bf, wu_bf, wd_bf)


def _combine_kernel(tb_ref, bst_ref, x_ref, pos_ref, aff_ref, mod_ref, gfin_ref, ye_hbm, o_ref,
                    win, acc, sem, *, slots, blocks, final_norm):
    E = N_EXPERTS
    i = pl.program_id(0)
    bpt = TM_COMB // LANES
    nblocks_total = blocks

    starts, ends = [], []
    npass = jnp.int32(1)
    for e in range(E):
        st = bst_ref[e * (nblocks_total + 1) + i * bpt]
        en = bst_ref[e * (nblocks_total + 1) + (i + 1) * bpt]
        st = (st // 16) * 16
        starts.append(st)
        ends.append(en)
        npass = jnp.maximum(npass, (en - st + (WIN - 1)) // WIN)

    lane_e = lax.broadcasted_iota(jnp.int32, (1, E), 1)
    erow = lax.broadcasted_iota(jnp.int32, (E, E * WIN), 0)
    ecol = lax.broadcasted_iota(jnp.int32, (E, E * WIN), 1)
    expand = jnp.where(ecol // WIN == erow, 1.0, 0.0).astype(BF16)
    sprime = (lax.broadcasted_iota(jnp.int32, (1, E * WIN), 1) % WIN).astype(F32)

    pos = pos_ref[...]
    gh, gl = _split2(aff_ref[...])
    gate = _dot(gh, expand) + _dot(gl, expand)

    acc[...] = jnp.zeros(acc.shape, F32)

    def one_pass(p, carry):
        lo_vec = jnp.zeros((1, E), F32)
        w0_vec = jnp.zeros((1, E), F32)
        copies = []
        for e in range(E):
            lo = starts[e] + p * WIN
            w0 = jnp.minimum(lo, slots - WIN)
            w0 = pl.multiple_of(w0, 16)
            lo_vec = jnp.where(lane_e == e, lo.astype(F32), lo_vec)
            w0_vec = jnp.where(lane_e == e, w0.astype(F32), w0_vec)
            cp = pltpu.make_async_copy(ye_hbm.at[e, pl.ds(w0, WIN), :],
                                       win.at[pl.ds(e * WIN, WIN), :], sem.at[e])
            cp.start()
            copies.append(cp)
        rel = jnp.where(pos >= lo_vec, pos - w0_vec, -1.0)
        rel = jnp.clip(rel, -1.0, np.float32(WIN))
        relx = _dot(rel.astype(BF16), expand)
        pm = jnp.where(relx == sprime, gate, 0.0)
        ph, plo = _split2(pm)
        for cp in copies:
            cp.wait()
        w = win[...]
        acc[...] += _dot(ph, w) + _dot(plo, w)
        return carry

    lax.fori_loop(0, npass, one_pass, 0)

    m = mod_ref[...]
    y = x_ref[...] + m[5:6] * acc[...]
    if final_norm:
        ms = jnp.mean(y * y, axis=-1, keepdims=True)
        y = y * lax.rsqrt(ms + NORM_EPS) * gfin_ref[...]
    o_ref[...] = y


def _combine(x1, pos_t, aff_tok, mod, g_final, ye, bst_flat, groups, final_norm):
    M = x1.shape[0]
    tm = TM_COMB
    E = N_EXPERTS
    slots = ye.shape[1]
    tb, _ = _tile_tables(groups, tm)
    grid_spec = pltpu.PrefetchScalarGridSpec(
        num_scalar_prefetch=2,
        grid=(M // tm,),
        in_specs=[
            pl.BlockSpec((tm, D), lambda i, tb, bst: (i, 0)),
            pl.BlockSpec((tm, E), lambda i, tb, bst: (i, 0)),
            pl.BlockSpec((tm, E), lambda i, tb, bst: (i, 0)),
            pl.BlockSpec((None, 6, D), lambda i, tb, bst: (tb[i], 0, 0)),
            pl.BlockSpec((1, D), lambda i, tb, bst: (0, 0)),
            pl.BlockSpec(memory_space=pl.ANY),
        ],
        out_specs=pl.BlockSpec((tm, D), lambda i, tb, bst: (i, 0)),
        scratch_shapes=[pltpu.VMEM((E * WIN, D), BF16), pltpu.VMEM((tm, D), F32),
                        pltpu.SemaphoreType.DMA((E,))],
    )
    return pl.pallas_call(
        functools.partial(_combine_kernel, slots=slots, blocks=M // LANES, final_norm=final_norm),
        grid_spec=grid_spec,
        out_shape=jax.ShapeDtypeStruct((M, D), F32),
        compiler_params=_cparams(("arbitrary",)),
        name="moe_combine",
    )(tb, bst_flat, x1, pos_t, aff_tok, mod, g_final, ye)


def _moe(x1, h2, aff_t, mod, g_final, wg_bf, wu_bf, wd_bf, groups, final_norm):
    E = N_EXPERTS
    idxs, poss, bsts = [], [], []
    tok_base = 0
    slot_base = 0
    for (B, S) in groups:
        n = B * S
        idx, pos, bst = _topk_group(aff_t, tok_base, n, slot_base)
        idxs.append(idx)
        poss.append(pos.reshape(E, n))
        bsts.append(bst[:, :, 0])
        tok_base += n
        slot_base += CAPACITY_FACTOR * n // E
    slots = slot_base
    idx_flat = jnp.concatenate(idxs, axis=1).reshape(-1)
    ye = _expert_ffn(idx_flat, h2, wg_bf, wu_bf, wd_bf, slots)
    pos_t = jnp.concatenate(poss, axis=1).T
    bst = jnp.concatenate(bsts + [jnp.full((E, 1), slots, F32)], axis=1)
    bst_flat = bst.astype(jnp.int32).reshape(-1)
    return _combine(x1, pos_t, aff_t.T, mod, g_final, ye, bst_flat, groups, final_norm)


def kernel(x_prompt, x_sample, c_prompt, c_sample, w_mod, b_mod, g_mix, g_ffn, a_w_in, a_w_out, b_w_in,
           b_ln_g, b_ln_b, b_w_s, b_b_s, b_w_out, moe_w_router, moe_w_gate, moe_w_up, moe_w_down, g_final):
    groups = (x_prompt.shape[:2], x_sample.shape[:2])
    assert x_prompt.shape[2] == D and x_sample.shape[2] == D
    x = jnp.concatenate([x_prompt.reshape(-1, D), x_sample.reshape(-1, D)], axis=0)
    n_seq = groups[0][0] + groups[1][0]
    rows = -(-n_seq // 8) * 8
    c_all = jnp.concatenate([c_prompt, c_sample, jnp.zeros((rows - n_seq, D), F32)], axis=0)
    mod = _modulation(c_all, w_mod, b_mod).reshape(w_mod.shape[0], rows, 6, D)
    cos_t, sin_t = _rope_tables(max(groups[0][1], groups[1][1]))
    wr_t = jnp.swapaxes(moe_w_router, 1, 2)
    gfin = g_final.reshape(1, D)

    os_, ls_ = [], []
    for g, r in enumerate(DILATIONS):
        w_g = a_w_in[0][:, g * 3 * D:(g + 1) * 3 * D].astype(BF16)
        qkv_g = _qkv_proj(x, mod[0], g_mix[0].reshape(1, D), w_g, _permute_table(cos_t, TM_PROJ, r),
                          _permute_table(sin_t, TM_PROJ, r), groups, r)
        o_g, l_g = _attention_group(qkv_g, groups)
        os_.append(o_g)
        ls_.append(l_g)
    x1, h2, aff_t = _out_proj(os_, ls_, x, mod[0], a_w_out[0].astype(BF16), g_ffn[0].reshape(1, D),
                              wr_t[0], groups)
    x = _moe(x1, h2, aff_t, mod[0], gfin, moe_w_gate[0].astype(BF16), moe_w_up[0].astype(BF16),
             moe_w_down[0].astype(BF16), groups, final_norm=False)

    x1, h2, aff_t = _sgu_layer(x, mod[1], g_mix[1].reshape(1, D), b_w_in[0].astype(BF16),
                               b_ln_g[0].reshape(1, SGU_HALF), b_ln_b[0].reshape(1, SGU_HALF),
                               b_w_s[0].astype(BF16), b_b_s[0].reshape(SGU_HEADS, CHUNK, 1),
                               b_w_out[0].astype(BF16), g_ffn[1].reshape(1, D), wr_t[1], groups)
    y = _moe(x1, h2, aff_t, mod[1], gfin, moe_w_gate[1].astype(BF16), moe_w_up[1].astype(BF16),
             moe_w_down[1].astype(BF16), groups, final_norm=True)

    n1 = groups[0][0] * groups[0][1]
    return (y[:n1].reshape(x_prompt.shape), y[n1:].reshape(x_sample.shape))
```

```python
import functools

import numpy as np
import jax
import jax.numpy as jnp
from jax import lax
from jax.experimental import pallas as pl
from jax.experimental.pallas import tpu as pltpu

F32 = jnp.float32
BF16 = jnp.bfloat16

D = 1024
N_HEADS = 16
HEAD_DIM = 64
DILATIONS = (1, 4, 16)
WINDOWS = (128, 512, 2048)
RADIUS = 64
N_GROUPS = 3
QKV_COLS = N_GROUPS * 3 * D
ROPE_THETA = 10000.0
CHUNK = 128
SGU_HALF = 3 * D
SGU_HEADS = 16
SGU_HEAD_CH = SGU_HALF // SGU_HEADS
N_EXPERTS = 16
CAPACITY_FACTOR = 2
D_EXPERT = 2 * D
NORM_EPS = 1e-6
NEG_BIG = -1e30

LANES = 128
VMEM_LIMIT = 56 * 1024 * 1024

TM_PROJ = 512
TM_SGU = 256
TM_COMB = 256
TQ = 256
UQ = 128
FFN_TILE = 256
F_CHUNK = 512
WIN = 64
SLOT_CHUNK = 512


def _cparams(sem):
    return pltpu.CompilerParams(dimension_semantics=sem, vmem_limit_bytes=VMEM_LIMIT)


def _split2(x):
    hi = x.astype(BF16)
    lo = (x - hi.astype(F32)).astype(BF16)
    return hi, lo


def _split3(x):
    hi = x.astype(BF16)
    r1 = x - hi.astype(F32)
    mid = r1.astype(BF16)
    lo = (r1 - mid.astype(F32)).astype(BF16)
    return hi, mid, lo


def _dot(a, b):
    return jnp.dot(a, b, preferred_element_type=F32)


def _dot_nt(a, b):
    return lax.dot_general(a, b, (((1,), (1,)), ((), ())), preferred_element_type=F32)


def _dot3(a, b):
    ah, al = _split2(a)
    bh, bl = _split2(b)
    return _dot(ah, bh) + (_dot(ah, bl) + _dot(al, bh))


def _dot3_nt(a, b):
    ah, al = _split2(a)
    bh, bl = _split2(b)
    return _dot_nt(ah, bh) + (_dot_nt(ah, bl) + _dot_nt(al, bh))


def _sigmoid(x):
    return 1.0 / (1.0 + jnp.exp(-x))


def _gelu_tanh(x):
    c = np.float32(np.sqrt(2.0 / np.pi))
    return 0.5 * x * (1.0 + jnp.tanh(c * (x + np.float32(0.044715) * (x * x * x))))


def _norm_mod(x, g, shift, scale):
    ms = jnp.mean(x * x, axis=-1, keepdims=True)
    return (x * lax.rsqrt(ms + NORM_EPS) * g) * (1.0 + scale) + shift


def _tile_tables(groups, tm):
    tb, tp = [], []
    b0 = 0
    for (B, S) in groups:
        assert S % tm == 0
        for b in range(B):
            for k in range(S // tm):
                tb.append(b0 + b)
                tp.append(k)
        b0 += B
    return jnp.asarray(np.array(tb, np.int32)), jnp.asarray(np.array(tp, np.int32))


def _mod_kernel(c_ref, w_ref, b_ref, o_ref):
    c = c_ref[...]
    o_ref[...] = _dot3(c * _sigmoid(c), w_ref[...]) + b_ref[...]


def _modulation(c_pad, w_mod, b_mod):
    depth, _, n = w_mod.shape
    tn = 2048
    rows = c_pad.shape[0]
    return pl.pallas_call(
        _mod_kernel,
        grid=(depth, n // tn),
        in_specs=[
            pl.BlockSpec((rows, D), lambda l, j: (0, 0)),
            pl.BlockSpec((None, D, tn), lambda l, j: (l, 0, j)),
            pl.BlockSpec((None, 1, tn), lambda l, j: (l, 0, j)),
        ],
        out_specs=pl.BlockSpec((None, rows, tn), lambda l, j: (l, 0, j)),
        out_shape=jax.ShapeDtypeStruct((depth, rows, n), F32),
        compiler_params=_cparams(("parallel", "parallel")),
        name="modulation",
    )(c_pad, w_mod, b_mod.reshape(depth, 1, n))


def _permute_rows(src_ref, slab, dst, r):
    tm, cols = src_ref.shape
    n = tm // r
    for c in range(cols // LANES):
        slab[c] = src_ref[:, c * LANES:(c + 1) * LANES]
    for res in range(r):
        for c in range(cols // LANES):
            dst[res * n:(res + 1) * n, c * LANES:(c + 1) * LANES] = slab[c, pl.ds(res, n, stride=r), :]


def _qkv_kernel(tb_ref, tp_ref, x_ref, mod_ref, g_ref, w_ref, cos_ref, sin_ref, o_ref, h_scr, *scr, r):
    j = pl.program_id(1)

    @pl.when(j == 0)
    def _():
        m = mod_ref[...]
        if r == 1:
            x = x_ref[...]
        else:
            slab, xp = scr
            _permute_rows(x_ref, slab, xp, r)
            x = xp[...]
        h_scr[...] = _norm_mod(x, g_ref[...], m[0:1], m[1:2]).astype(BF16)

    n = h_scr.shape[0] // r
    nw = 2 * LANES

    def project(rope):
        if rope:
            scale = jnp.where(j == 0, np.float32(HEAD_DIM ** -0.5), np.float32(1.0))
            cos = cos_ref[...] * scale
            sin = sin_ref[...] * scale
        for cw in range(D // nw):
            acc = _dot(h_scr[...], w_ref[:, cw * nw:(cw + 1) * nw])
            for c in range(nw // LANES):
                xc = acc[:, c * LANES:(c + 1) * LANES]
                if rope:
                    xc = xc * cos + pltpu.roll(xc, LANES // 2, 1) * sin
                cs = slice(cw * nw + c * LANES, cw * nw + (c + 1) * LANES)
                o_ref[:, :, cs] = xc.reshape(r, n, LANES).astype(BF16)

    @pl.when(j == 2)
    def _():
        project(False)

    @pl.when(j != 2)
    def _():
        project(True)


def _qkv_proj(x, mod, g, w_bf, cos_p, sin_p, groups, r):
    M = x.shape[0]
    tm = TM_PROJ
    tb, tp = _tile_tables(groups, tm)
    scratch = [pltpu.VMEM((tm, D), BF16)]
    if r > 1:
        scratch += [pltpu.VMEM((D // LANES, tm, LANES), F32), pltpu.VMEM((tm, D), F32)]
    grid_spec = pltpu.PrefetchScalarGridSpec(
        num_scalar_prefetch=2,
        grid=(M // tm, 3),
        in_specs=[
            pl.BlockSpec((tm, D), lambda i, j, tb, tp: (i, 0)),
            pl.BlockSpec((None, 6, D), lambda i, j, tb, tp: (tb[i], 0, 0)),
            pl.BlockSpec((1, D), lambda i, j, tb, tp: (0, 0)),
            pl.BlockSpec((D, D), lambda i, j, tb, tp: (0, j)),
            pl.BlockSpec((tm, LANES), lambda i, j, tb, tp: (tp[i], 0)),
            pl.BlockSpec((tm, LANES), lambda i, j, tb, tp: (tp[i], 0)),
        ],
        out_specs=pl.BlockSpec((r, tm // r, D), lambda i, j, tb, tp: (0, i, j)),
        scratch_shapes=scratch,
    )
    return pl.pallas_call(
        functools.partial(_qkv_kernel, r=r),
        grid_spec=grid_spec,
        out_shape=jax.ShapeDtypeStruct((r, M // r, 3 * D), BF16),
        compiler_params=_cparams(("parallel", "arbitrary")),
        name=f"qkv_rope_d{r}",
    )(tb, tp, x, mod, g, w_bf, cos_p, sin_p)


def _rope_tables(s_max):
    half = HEAD_DIM // 2
    inv = ROPE_THETA ** (-jnp.arange(0, HEAD_DIM, 2, dtype=F32) / HEAD_DIM)
    ang = jnp.arange(s_max, dtype=F32)[:, None] * inv[None, :]
    cos, sin = jnp.cos(ang), jnp.sin(ang)
    cos_t = jnp.tile(cos, (1, LANES // half))
    sin_t = jnp.concatenate([-sin, -sin, sin, sin], axis=1)
    return cos_t, sin_t


def _pair_heads(w):
    half = HEAD_DIM // 2
    perm = np.concatenate([np.arange(0, half), np.arange(HEAD_DIM, HEAD_DIM + half),
                           np.arange(half, HEAD_DIM), np.arange(HEAD_DIM + half, LANES)])
    k = w.shape[0]
    qk = w[:, :2 * D].reshape(k, 2 * D // LANES, LANES)[:, :, perm].reshape(k, 2 * D)
    return jnp.concatenate([qk, w[:, 2 * D:]], axis=1)


def _permute_table(t, tm, r):
    s = t.shape[0]
    return t.reshape(s // tm, tm // r, r, LANES).transpose(0, 2, 1, 3).reshape(s, LANES)


def _attn_kernel(q_ref, kp_ref, kc_ref, kn_ref, vp_ref, vc_ref, vn_ref, o_ref, l_ref,
                 kw, vw, s_scr, p_scr, *, seq_blocks):
    i = pl.program_id(1)
    (n1, nb1), (n2, nb2) = seq_blocks
    first_grp = i < n1 * nb1
    pos = jnp.where(first_grp, i % nb1, (i - n1 * nb1) % nb2)
    nb = jnp.where(first_grp, nb1, nb2)
    has_prev = pos > 0
    has_next = pos < nb - 1

    W = RADIUS
    kw[0:W] = kp_ref[...]
    kw[W:W + TQ] = kc_ref[...]
    kw[W + TQ:W + TQ + W] = kn_ref[...]
    vw[0:W] = vp_ref[...]
    vw[W:W + TQ] = vc_ref[...]
    vw[W + TQ:W + TQ + W] = vn_ref[...]

    lane = lax.broadcasted_iota(jnp.int32, (UQ, LANES), 1)
    low_half = lane < HEAD_DIM
    head_a = (lane % HEAD_DIM) < (HEAD_DIM // 2)
    nu = TQ // UQ
    for u in range(nu):
        for hp in range(N_HEADS // 2):
            cs = slice(hp * LANES, (hp + 1) * LANES)
            q2 = q_ref[u * UQ:(u + 1) * UQ, cs]
            kwin = kw[u * UQ:u * UQ + 2 * UQ, cs]
            zero = jnp.zeros_like(q2)
            s_scr[u, 2 * hp] = _dot_nt(jnp.where(head_a, q2, zero), kwin)
            s_scr[u, 2 * hp + 1] = _dot_nt(jnp.where(head_a, zero, q2), kwin)

    rr = lax.broadcasted_iota(jnp.int32, (UQ, 2 * UQ), 0)
    cc = lax.broadcasted_iota(jnp.int32, (UQ, 2 * UQ), 1)
    band = jnp.abs((cc - W) - rr) <= RADIUS
    l_ref[...] = jnp.zeros(l_ref.shape, F32)
    for u in range(nu):
        valid = band
        if u == 0:
            valid = valid & ((cc >= W) | has_prev)
        if u == nu - 1:
            valid = valid & ((cc < 2 * UQ - W) | has_next)
        s = jnp.where(valid[None], s_scr[u], NEG_BIG)
        m = jnp.max(s, axis=-1, keepdims=True)
        p = jnp.exp(s - m)
        den = jnp.sum(p, axis=-1, keepdims=True)
        p_scr[u] = p.astype(BF16)
        rden = 1.0 / den
        lse = m + jnp.log(den)
        rows = slice(u * UQ, (u + 1) * UQ)
        for hp in range(N_HEADS // 2):
            cs = slice(hp * LANES, (hp + 1) * LANES)
            vwin = vw[u * UQ:u * UQ + 2 * UQ, cs]
            oa = _dot(p_scr[u, 2 * hp], vwin) * rden[2 * hp]
            ob = _dot(p_scr[u, 2 * hp + 1], vwin) * rden[2 * hp + 1]
            o_ref[rows, cs] = jnp.where(low_half, oa, ob).astype(BF16)
        for h in range(N_HEADS):
            l_ref[rows, h:h + 1] = lse[h]


def _attention_group(qkv_g, groups):
    r, rows, _ = qkv_g.shape
    W = RADIUS
    for (B, S) in groups:
        assert S % (r * TQ) == 0
    seq_blocks = tuple((B, S // r // TQ) for (B, S) in groups)
    q64 = TQ // W
    last64 = rows // W - 1

    def big(col):
        return pl.BlockSpec((None, TQ, D), lambda res, i: (res, i, col))

    def prv(col):
        return pl.BlockSpec((None, W, D), lambda res, i: (res, jnp.maximum(i * q64 - 1, 0), col))

    def nxt(col):
        return pl.BlockSpec((None, W, D), lambda res, i: (res, jnp.minimum((i + 1) * q64, last64), col))

    return pl.pallas_call(
        functools.partial(_attn_kernel, seq_blocks=seq_blocks),
        grid=(r, rows // TQ),
        in_specs=[big(0), prv(1), big(1), nxt(1), prv(2), big(2), nxt(2)],
        out_specs=[pl.BlockSpec((None, TQ, D), lambda res, i: (res, i, 0)),
                   pl.BlockSpec((None, TQ, LANES), lambda res, i: (res, i, 0))],
        out_shape=[jax.ShapeDtypeStruct((r, rows, D), BF16),
                   jax.ShapeDtypeStruct((r, rows, LANES), F32)],
        scratch_shapes=[pltpu.VMEM((TQ + 2 * W, D), BF16), pltpu.VMEM((TQ + 2 * W, D), BF16),
                        pltpu.VMEM((TQ // UQ, N_HEADS, UQ, 2 * UQ), F32),
                        pltpu.VMEM((TQ // UQ, N_HEADS, UQ, 2 * UQ), BF16)],
        compiler_params=_cparams(("parallel", "parallel")),
        name=f"attn_d{r}",
    )(*([qkv_g] * 7))


def _router(x1, m, g_ffn, wr_t):
    h2 = _norm_mod(x1, g_ffn, m[3:4], m[4:5])
    logits_t = _dot3_nt(wr_t, h2)
    mx = jnp.max(logits_t, axis=0, keepdims=True)
    e = jnp.exp(logits_t - mx)
    return h2, e / jnp.sum(e, axis=0, keepdims=True)


def _oproj_kernel(tb_ref, o1_ref, o4_ref, o16_ref, l1_ref, l4_ref, l16_ref, x_ref, mod_ref, w_ref,
                  gf_ref, wr_ref, x1_ref, h2_ref, aff_ref, oslab, lslab, om_scr):
    tm = x_ref.shape[0]
    nc = D // LANES
    for g, (r, o_ref, l_ref) in enumerate(zip(DILATIONS, (o1_ref, o4_ref, o16_ref),
                                              (l1_ref, l4_ref, l16_ref))):
        n = tm // r
        for res in range(r):
            rows = slice(None) if r == 1 else pl.ds(res, n, stride=r)
            lslab[g, 0, rows, :] = l_ref[res]
            for c in range(nc):
                oslab[g, c, rows, :] = o_ref[res, :, c * LANES:(c + 1) * LANES].astype(F32)
    ls = [lslab[g, 0] for g in range(N_GROUPS)]
    mx = jnp.maximum(jnp.maximum(ls[0], ls[1]), ls[2])
    es = [jnp.exp(l - mx) for l in ls]
    tot = es[0] + es[1] + es[2]
    hrow = lax.broadcasted_iota(jnp.int32, (LANES, D), 0)
    hcol = lax.broadcasted_iota(jnp.int32, (LANES, D), 1)
    expand = jnp.where(hcol // HEAD_DIM == hrow, 1.0, 0.0).astype(BF16)
    wexp = []
    for g in range(N_GROUPS):
        wh, wl = _split2(es[g] / tot)
        wexp.append(_dot(wh, expand) + _dot(wl, expand))
    for c in range(nc):
        cs = slice(c * LANES, (c + 1) * LANES)
        om = wexp[0][:, cs] * oslab[0, c] + wexp[1][:, cs] * oslab[1, c] + wexp[2][:, cs] * oslab[2, c]
        om_scr[:, cs] = om.astype(BF16)
    m = mod_ref[...]
    y = _dot(om_scr[...], w_ref[...])
    x1 = x_ref[...] + m[2:3] * y
    x1_ref[...] = x1
    h2, aff_t = _router(x1, m, gf_ref[...], wr_ref[...])
    h2_ref[...] = h2
    aff_ref[...] = aff_t


def _out_proj(os_, ls_, x, mod, w_bf, g_ffn, wr_t, groups):
    M = x.shape[0]
    tm = TM_PROJ
    tb, _ = _tile_tables(groups, tm)
    o_specs = [pl.BlockSpec((r, tm // r, D), lambda i, tb: (0, i, 0)) for r in DILATIONS]
    l_specs = [pl.BlockSpec((r, tm // r, LANES), lambda i, tb: (0, i, 0)) for r in DILATIONS]
    grid_spec = pltpu.PrefetchScalarGridSpec(
        num_scalar_prefetch=1,
        grid=(M // tm,),
        in_specs=o_specs + l_specs + [
            pl.BlockSpec((tm, D), lambda i, tb: (i, 0)),
            pl.BlockSpec((None, 6, D), lambda i, tb: (tb[i], 0, 0)),
            pl.BlockSpec((D, D), lambda i, tb: (0, 0)),
            pl.BlockSpec((1, D), lambda i, tb: (0, 0)),
            pl.BlockSpec((N_EXPERTS, D), lambda i, tb: (0, 0)),
        ],
        out_specs=[
            pl.BlockSpec((tm, D), lambda i, tb: (i, 0)),
            pl.BlockSpec((tm, D), lambda i, tb: (i, 0)),
            pl.BlockSpec((N_EXPERTS, tm), lambda i, tb: (0, i)),
        ],
        scratch_shapes=[pltpu.VMEM((N_GROUPS, D // LANES, tm, LANES), F32),
                        pltpu.VMEM((N_GROUPS, 1, tm, LANES), F32),
                        pltpu.VMEM((tm, D), BF16)],
    )
    return pl.pallas_call(
        _oproj_kernel,
        grid_spec=grid_spec,
        out_shape=[jax.ShapeDtypeStruct((M, D), F32), jax.ShapeDtypeStruct((M, D), F32),
                   jax.ShapeDtypeStruct((N_EXPERTS, M), F32)],
        compiler_params=_cparams(("parallel",)),
        name="attn_out_router",
    )(tb, *os_, *ls_, x, mod, w_bf, g_ffn, wr_t)


def _sgu_kernel(tb_ref, x_ref, mod_ref, g_ref, win_ref, lng_ref, lnb_ref, ws_ref, bs_ref, wout_ref,
                gf_ref, wr_ref, x1_ref, h2_ref, aff_ref, h_scr, v_scr, g_scr):
    m = mod_ref[...]
    x = x_ref[...]
    h_scr[...] = _norm_mod(x, g_ref[...], m[0:1], m[1:2]).astype(BF16)
    tm = x.shape[0]
    nc = 512
    s1 = jnp.zeros((tm, 1), F32)
    s2 = jnp.zeros((tm, 1), F32)
    for c in range(SGU_HALF // nc):
        z = _gelu_tanh(_dot(h_scr[...], win_ref[:, SGU_HALF + c * nc:SGU_HALF + (c + 1) * nc]))
        v_scr[:, c * nc:(c + 1) * nc] = z
        s1 = s1 + jnp.sum(z, axis=-1, keepdims=True)
    mu = s1 * np.float32(1.0 / SGU_HALF)
    for c in range(SGU_HALF // nc):
        d = v_scr[:, c * nc:(c + 1) * nc] - mu
        s2 = s2 + jnp.sum(d * d, axis=-1, keepdims=True)
    rstd = lax.rsqrt(s2 * np.float32(1.0 / SGU_HALF) + NORM_EPS)
    for c in range(SGU_HALF // nc):
        cs = slice(c * nc, (c + 1) * nc)
        v_scr[:, cs] = (v_scr[:, cs] - mu) * rstd * lng_ref[:, cs] + lnb_ref[:, cs]
    for hd in range(SGU_HEADS):
        cs = slice(hd * SGU_HEAD_CH, (hd + 1) * SGU_HEAD_CH)
        u = _gelu_tanh(_dot(h_scr[...], win_ref[:, cs]))
        for k in range(tm // CHUNK):
            rs = slice(k * CHUNK, (k + 1) * CHUNK)
            sv = _dot(ws_ref[hd], v_scr[rs, cs].astype(BF16)) + bs_ref[hd]
            g_scr[rs, cs] = (u[rs] * sv).astype(BF16)
    y = _dot(g_scr[...], wout_ref[...])
    x1 = x + m[2:3] * y
    x1_ref[...] = x1
    h2, aff_t = _router(x1, m, gf_ref[...], wr_ref[...])
    h2_ref[...] = h2
    aff_ref[...] = aff_t


def _sgu_layer(x, mod, g_mix, w_in_bf, ln_g, ln_b, w_s_bf, b_s_col, w_out_bf, g_ffn, wr_t, groups):
    M = x.shape[0]
    tm = TM_SGU
    tb, _ = _tile_tables(groups, tm)
    const = lambda *shape: pl.BlockSpec(shape, lambda i, tb: (0,) * len(shape),
                                        pipeline_mode=pl.Buffered(1))
    grid_spec = pltpu.PrefetchScalarGridSpec(
        num_scalar_prefetch=1,
        grid=(M // tm,),
        in_specs=[
            pl.BlockSpec((tm, D), lambda i, tb: (i, 0)),
            pl.BlockSpec((None, 6, D), lambda i, tb: (tb[i], 0, 0)),
            const(1, D),
            const(D, 2 * SGU_HALF),
            const(1, SGU_HALF),
            const(1, SGU_HALF),
            const(SGU_HEADS, CHUNK, CHUNK),
            const(SGU_HEADS, CHUNK, 1),
            const(SGU_HALF, D),
            const(1, D),
            const(N_EXPERTS, D),
        ],
        out_specs=[
            pl.BlockSpec((tm, D), lambda i, tb: (i, 0)),
            pl.BlockSpec((tm, D), lambda i, tb: (i, 0)),
            pl.BlockSpec((N_EXPERTS, tm), lambda i, tb: (0, i)),
        ],
        scratch_shapes=[pltpu.VMEM((tm, D), BF16), pltpu.VMEM((tm, SGU_HALF), F32),
                        pltpu.VMEM((tm, SGU_HALF), BF16)],
    )
    return pl.pallas_call(
        _sgu_kernel,
        grid_spec=grid_spec,
        out_shape=[jax.ShapeDtypeStruct((M, D), F32), jax.ShapeDtypeStruct((M, D), F32),
                   jax.ShapeDtypeStruct((N_EXPERTS, M), F32)],
        compiler_params=_cparams(("parallel",)),
        name="sgu_router",
    )(tb, x, mod, g_mix, w_in_bf, ln_g, ln_b, w_s_bf, b_s_col, w_out_bf, g_ffn, wr_t)


def _topk_kernel(aff_ref, idx_ref, gate_ref, pos_ref, bst_ref, incl_scr, tot_scr, bex_scr, *, nb, cap,
                 tok_base, slot_base):
    E = N_EXPERTS
    a = aff_ref[...]
    bits = pltpu.bitcast(a, jnp.int32)

    def count(mask):
        c = jnp.sum(jnp.where(mask, 1.0, 0.0), axis=1, keepdims=True)
        return jnp.sum(c, axis=2, keepdims=True)

    def search(k, thr):
        cand = thr | jnp.left_shift(jnp.int32(1), 30 - k)
        return jnp.where(count(bits >= cand) >= cap, cand, thr)

    thr = lax.fori_loop(0, 31, search, jnp.zeros((E, 1, 1), jnp.int32))
    gt = bits > thr
    eq = bits == thr
    need = cap - count(gt)

    row = lax.broadcasted_iota(jnp.int32, (LANES, LANES), 0)
    col = lax.broadcasted_iota(jnp.int32, (LANES, LANES), 1)
    upper = jnp.where(row <= col, 1.0, 0.0).astype(BF16)
    ones = jnp.ones((LANES, LANES), BF16)
    brow = lax.broadcasted_iota(jnp.int32, (nb, nb), 0)
    bcol = lax.broadcasted_iota(jnp.int32, (nb, nb), 1)
    lower_strict = jnp.where(bcol < brow, 1.0, 0.0).astype(BF16)

    def prefix(mask):
        xb = jnp.where(mask, 1.0, 0.0).astype(BF16).reshape(E * nb, LANES)
        incl_scr[...] = _dot(xb, upper).reshape(E, nb, LANES)
        tot_scr[...] = _dot(xb, ones).reshape(E, nb, LANES)
        for e in range(E):
            bex_scr[e] = _dot(lower_strict, tot_scr[e].astype(BF16))

    prefix(eq)
    rank_eq = bex_scr[...] + incl_scr[...] - jnp.where(eq, 1.0, 0.0)
    sel = gt | (eq & (rank_eq < need))
    prefix(sel)
    self32 = jnp.where(sel, 1.0, 0.0)
    pos = bex_scr[...] + incl_scr[...] - self32
    pos_ref[...] = jnp.where(sel, pos + np.float32(slot_base), -1.0)
    bst_ref[...] = bex_scr[...] + np.float32(slot_base)

    sc = SLOT_CHUNK
    bidx = lax.broadcasted_iota(jnp.int32, (nb, sc), 0).astype(F32)
    jidx = lax.broadcasted_iota(jnp.int32, (LANES, sc), 0).astype(F32)
    for e in range(E):
        tot_e = tot_scr[e][:, 0:1]
        binc_e = bex_scr[e][:, 0:1] + tot_e
        incl_t = jnp.transpose(incl_scr[e]).astype(BF16)
        aff_parts = _split3(jnp.transpose(a[e]))
        for c in range(cap // sc):
            s = (lax.broadcasted_iota(jnp.int32, (nb, sc), 1) + c * sc).astype(F32)
            before = binc_e <= s
            blk = jnp.sum(jnp.where(before, 1.0, 0.0), axis=0, keepdims=True)
            base = jnp.sum(jnp.where(before, tot_e, 0.0), axis=0, keepdims=True)
            local = s[0:1] - base
            onehot = jnp.where(bidx == blk, 1.0, 0.0).astype(BF16)
            rows = _dot(incl_t, onehot)
            cnt = jnp.sum(jnp.where(rows <= local, 1.0, 0.0), axis=0, keepdims=True)
            tok = blk * np.float32(LANES) + cnt + np.float32(tok_base)
            idx_ref[e:e + 1, c * sc:(c + 1) * sc] = tok.astype(jnp.int32)
            arow = (_dot(aff_parts[0], onehot) + _dot(aff_parts[1], onehot)) + _dot(aff_parts[2], onehot)
            gate_ref[e:e + 1, c * sc:(c + 1) * sc] = jnp.sum(
                jnp.where(jidx == cnt, arow, 0.0), axis=0, keepdims=True)


def _topk_group(aff_t, tok_base, n, slot_base):
    E = N_EXPERTS
    M = aff_t.shape[1]
    nb = n // LANES
    cap = CAPACITY_FACTOR * n // E
    assert n % LANES == 0 and nb % 8 == 0 and cap % SLOT_CHUNK == 0 and tok_base % n == 0
    a3 = aff_t.reshape(E, M // LANES, LANES)
    blk = tok_base // n
    return pl.pallas_call(
        functools.partial(_topk_kernel, nb=nb, cap=cap, tok_base=tok_base, slot_base=slot_base),
        grid=(1,),
        in_specs=[pl.BlockSpec((E, nb, LANES), lambda i: (0, blk, 0))],
        out_specs=[pl.BlockSpec((E, cap), lambda i: (0, 0)),
                   pl.BlockSpec((E, cap), lambda i: (0, 0)),
                   pl.BlockSpec((E, nb, LANES), lambda i: (0, 0, 0)),
                   pl.BlockSpec((E, nb, LANES), lambda i: (0, 0, 0))],
        out_shape=[jax.ShapeDtypeStruct((E, cap), jnp.int32),
                   jax.ShapeDtypeStruct((E, cap), F32),
                   jax.ShapeDtypeStruct((E, nb, LANES), F32),
                   jax.ShapeDtypeStruct((E, nb, LANES), F32)],
        scratch_shapes=[pltpu.VMEM((E, nb, LANES), F32)] * 3,
        compiler_params=_cparams(("arbitrary",)),
        name=f"expert_choice_topk_{n}",
    )(a3)


def _ffn_kernel(idx_ref, h_hbm, gate_ref, wg_ref, wu_ref, wd_ref, ye_ref, xbuf, sem, *, tiles_per_expert,
                n_steps):
    step = pl.program_id(0) * tiles_per_expert + pl.program_id(1)
    slot = step % 2

    def row_copy(tile, k, slot_):
        t = idx_ref[tile * FFN_TILE + k]
        return pltpu.make_async_copy(h_hbm.at[pl.ds(t, 1), :], xbuf.at[slot_, pl.ds(k, 1), :],
                                     sem.at[slot_])

    @pl.when(step == 0)
    def _():
        def issue(k, carry):
            row_copy(0, k, 0).start()
            return carry
        lax.fori_loop(0, FFN_TILE, issue, 0, unroll=8)

    for k in range(FFN_TILE):
        row_copy(step, k, slot).wait()
    x = xbuf[slot].astype(BF16)
    nxt = jnp.minimum(step + 1, n_steps - 1)
    for k in range(FFN_TILE):
        row_copy(nxt, k, 1 - slot).start()

    acc = jnp.zeros((FFN_TILE, D), F32)
    for c in range(D_EXPERT // F_CHUNK):
        cs = slice(c * F_CHUNK, (c + 1) * F_CHUNK)
        hg = _dot(x, wg_ref[:, cs])
        hu = _dot(x, wu_ref[:, cs])
        act = (hg * _sigmoid(hg) * hu).astype(BF16)
        acc = acc + _dot(act, wd_ref[cs, :])
    krow = lax.broadcasted_iota(jnp.int32, (FFN_TILE, FFN_TILE), 0)
    kcol = lax.broadcasted_iota(jnp.int32, (FFN_TILE, FFN_TILE), 1)
    gate_col = jnp.sum(jnp.where(krow == kcol, gate_ref[...], 0.0), axis=1, keepdims=True)
    ye_ref[...] = (acc * gate_col).astype(BF16)

    @pl.when(step == n_steps - 1)
    def _():
        for k in range(FFN_TILE):
            row_copy(nxt, k, 1 - slot).wait()


def _expert_ffn(idx_flat, h2, gate_all, wg_bf, wu_bf, wd_bf, slots):
    E = N_EXPERTS
    tiles = slots // FFN_TILE
    grid_spec = pltpu.PrefetchScalarGridSpec(
        num_scalar_prefetch=1,
        grid=(E, tiles),
        in_specs=[
            pl.BlockSpec(memory_space=pl.ANY),
            pl.BlockSpec((None, 1, FFN_TILE), lambda e, j, idx: (e * tiles + j, 0, 0)),
            pl.BlockSpec((None, D, D_EXPERT), lambda e, j, idx: (e, 0, 0)),
            pl.BlockSpec((None, D, D_EXPERT), lambda e, j, idx: (e, 0, 0)),
            pl.BlockSpec((None, D_EXPERT, D), lambda e, j, idx: (e, 0, 0)),
        ],
        out_specs=pl.BlockSpec((None, FFN_TILE, D), lambda e, j, idx: (e, j, 0)),
        scratch_shapes=[pltpu.VMEM((2, FFN_TILE, D), F32), pltpu.SemaphoreType.DMA((2,))],
    )
    return pl.pallas_call(
        functools.partial(_ffn_kernel, tiles_per_expert=tiles, n_steps=E * tiles),
        grid_spec=grid_spec,
        out_shape=jax.ShapeDtypeStruct((E, slots, D), BF16),
        compiler_params=_cparams(("arbitrary", "arbitrary")),
        name="expert_ffn",
    )(idx_flat, h2, gate_all.reshape(E * tiles, 1, FFN_TILE), wg_bf, wu_bf, wd_bf)


def _combine_kernel(tb_ref, bst_ref, x_ref, pos_ref, mod_ref, gfin_ref, ye_hbm, *rest, slots, blocks,
                    n_tiles, final_norm, split):
    if split is None:
        (o_ref, win, winx, acc, sem, semx) = rest
    else:
        (o_ref, o2_ref, win, winx, acc, sem, semx) = rest
    E = N_EXPERTS
    i = pl.program_id(0)
    bpt = TM_COMB // LANES
    stride = blocks + 1

    def tile_start(tile, e):
        return (bst_ref[e * stride + tile * bpt] // 16) * 16

    def window_copy(e, lo, buf, sem_):
        w0 = pl.multiple_of(jnp.minimum(lo, slots - WIN), 16)
        return pltpu.make_async_copy(ye_hbm.at[e, pl.ds(w0, WIN), :], buf.at[pl.ds(e * WIN, WIN), :],
                                     sem_.at[e])

    slot = i % 2

    @pl.when(i == 0)
    def _():
        for e in range(E):
            window_copy(e, tile_start(0, e), win.at[0], sem.at[0]).start()

    starts = [tile_start(i, e) for e in range(E)]
    npass = jnp.int32(1)
    for e in range(E):
        end = bst_ref[e * stride + (i + 1) * bpt]
        npass = jnp.maximum(npass, (end - starts[e] + (WIN - 1)) // WIN)

    for e in range(E):
        window_copy(e, starts[e], win.at[slot], sem.at[slot]).wait()
    nxt = jnp.minimum(i + 1, n_tiles - 1)
    for e in range(E):
        window_copy(e, tile_start(nxt, e), win.at[1 - slot], sem.at[1 - slot]).start()

    lane_e = lax.broadcasted_iota(jnp.int32, (1, E), 1)
    erow = lax.broadcasted_iota(jnp.int32, (E, E * WIN), 0)
    ecol = lax.broadcasted_iota(jnp.int32, (E, E * WIN), 1)
    expand = jnp.where(ecol // WIN == erow, 1.0, 0.0).astype(BF16)
    sprime = (lax.broadcasted_iota(jnp.int32, (1, E * WIN), 1) % WIN).astype(F32)
    pos = pos_ref[...]

    def scatter_rows(p, wbuf):
        lo_vec = jnp.zeros((1, E), F32)
        w0_vec = jnp.zeros((1, E), F32)
        for e in range(E):
            lo = starts[e] + p * WIN
            w0 = jnp.minimum(lo, slots - WIN)
            lo_vec = jnp.where(lane_e == e, lo.astype(F32), lo_vec)
            w0_vec = jnp.where(lane_e == e, w0.astype(F32), w0_vec)
        rel = jnp.where(pos >= lo_vec, pos - w0_vec, -1.0)
        rel = jnp.clip(rel, -1.0, np.float32(WIN))
        relx = _dot(rel.astype(BF16), expand)
        onehot = jnp.where(relx == sprime, 1.0, 0.0).astype(BF16)
        return _dot(onehot, wbuf[...])

    acc[...] = scatter_rows(0, win.at[slot])

    def extra_pass(p, carry):
        copies = [window_copy(e, starts[e] + p * WIN, winx, semx) for e in range(E)]
        for cp in copies:
            cp.start()
        for cp in copies:
            cp.wait()
        acc[...] += scatter_rows(p, winx)
        return carry

    lax.fori_loop(1, npass, extra_pass, 0)

    m = mod_ref[...]
    y = x_ref[...] + m[5:6] * acc[...]
    if final_norm:
        ms = jnp.mean(y * y, axis=-1, keepdims=True)
        y = y * lax.rsqrt(ms + NORM_EPS) * gfin_ref[...]
    if split is None:
        o_ref[...] = y
    else:
        @pl.when(i < split)
        def _():
            o_ref[...] = y

        @pl.when(i >= split)
        def _():
            o2_ref[...] = y

    @pl.when(i == n_tiles - 1)
    def _():
        for e in range(E):
            window_copy(e, tile_start(nxt, e), win.at[1 - slot], sem.at[1 - slot]).wait()


def _combine(x1, pos_t, mod, g_final, ye, bst_flat, groups, final_norm):
    M = x1.shape[0]
    tm = TM_COMB
    E = N_EXPERTS
    slots = ye.shape[1]
    tb, _ = _tile_tables(groups, tm)
    n_tiles = M // tm
    n1 = groups[0][0] * groups[0][1]
    if final_norm:
        split = n1 // tm
        out_specs = [pl.BlockSpec((tm, D), lambda i, tb, bst: (jnp.minimum(i, split - 1), 0)),
                     pl.BlockSpec((tm, D), lambda i, tb, bst: (jnp.maximum(i - split, 0), 0))]
        out_shape = [jax.ShapeDtypeStruct((n1, D), F32), jax.ShapeDtypeStruct((M - n1, D), F32)]
    else:
        split = None
        out_specs = pl.BlockSpec((tm, D), lambda i, tb, bst: (i, 0))
        out_shape = jax.ShapeDtypeStruct((M, D), F32)
    grid_spec = pltpu.PrefetchScalarGridSpec(
        num_scalar_prefetch=2,
        grid=(n_tiles,),
        in_specs=[
            pl.BlockSpec((tm, D), lambda i, tb, bst: (i, 0)),
            pl.BlockSpec((tm, E), lambda i, tb, bst: (i, 0)),
            pl.BlockSpec((None, 6, D), lambda i, tb, bst: (tb[i], 0, 0)),
            pl.BlockSpec((1, D), lambda i, tb, bst: (0, 0)),
            pl.BlockSpec(memory_space=pl.ANY),
        ],
        out_specs=out_specs,
        scratch_shapes=[pltpu.VMEM((2, E * WIN, D), BF16), pltpu.VMEM((E * WIN, D), BF16),
                        pltpu.VMEM((tm, D), F32), pltpu.SemaphoreType.DMA((2, E)),
                        pltpu.SemaphoreType.DMA((E,))],
    )
    return pl.pallas_call(
        functools.partial(_combine_kernel, slots=slots, blocks=M // LANES, n_tiles=n_tiles,
                          final_norm=final_norm, split=split),
        grid_spec=grid_spec,
        out_shape=out_shape,
        compiler_params=_cparams(("arbitrary",)),
        name="moe_combine",
    )(tb, bst_flat, x1, pos_t, mod, g_final, ye)


def _moe(x1, h2, aff_t, mod, g_final, wg_bf, wu_bf, wd_bf, groups, final_norm):
    E = N_EXPERTS
    idxs, gates, poss, bsts = [], [], [], []
    tok_base = 0
    slot_base = 0
    for (B, S) in groups:
        n = B * S
        idx, gate, pos, bst = _topk_group(aff_t, tok_base, n, slot_base)
        idxs.append(idx)
        gates.append(gate)
        poss.append(pos.reshape(E, n))
        bsts.append(bst[:, :, 0])
        tok_base += n
        slot_base += CAPACITY_FACTOR * n // E
    slots = slot_base
    idx_flat = jnp.concatenate(idxs, axis=1).reshape(-1)
    ye = _expert_ffn(idx_flat, h2, jnp.concatenate(gates, axis=1), wg_bf, wu_bf, wd_bf, slots)
    pos_t = jnp.concatenate(poss, axis=1).T
    bst = jnp.concatenate(bsts + [jnp.full((E, 1), slots, F32)], axis=1)
    bst_flat = bst.astype(jnp.int32).reshape(-1)
    return _combine(x1, pos_t, mod, g_final, ye, bst_flat, groups, final_norm)


def kernel(x_prompt, x_sample, c_prompt, c_sample, w_mod, b_mod, g_mix, g_ffn, a_w_in, a_w_out, b_w_in,
           b_ln_g, b_ln_b, b_w_s, b_b_s, b_w_out, moe_w_router, moe_w_gate, moe_w_up, moe_w_down, g_final):
    groups = (x_prompt.shape[:2], x_sample.shape[:2])
    assert x_prompt.shape[2] == D and x_sample.shape[2] == D
    x = jnp.concatenate([x_prompt.reshape(-1, D), x_sample.reshape(-1, D)], axis=0)
    n_seq = groups[0][0] + groups[1][0]
    rows = -(-n_seq // 8) * 8
    c_all = jnp.concatenate([c_prompt, c_sample, jnp.zeros((rows - n_seq, D), F32)], axis=0)
    mod = _modulation(c_all, w_mod, b_mod).reshape(w_mod.shape[0], rows, 6, D)
    cos_t, sin_t = _rope_tables(max(groups[0][1], groups[1][1]))
    wr_t = jnp.swapaxes(moe_w_router, 1, 2)
    gfin = g_final.reshape(1, D)

    os_, ls_ = [], []
    for g, r in enumerate(DILATIONS):
        w_g = _pair_heads(a_w_in[0][:, g * 3 * D:(g + 1) * 3 * D]).astype(BF16)
        qkv_g = _qkv_proj(x, mod[0], g_mix[0].reshape(1, D), w_g, _permute_table(cos_t, TM_PROJ, r),
                          _permute_table(sin_t, TM_PROJ, r), groups, r)
        o_g, l_g = _attention_group(qkv_g, groups)
        os_.append(o_g)
        ls_.append(l_g)
    x1, h2, aff_t = _out_proj(os_, ls_, x, mod[0], a_w_out[0].astype(BF16), g_ffn[0].reshape(1, D),
                              wr_t[0], groups)
    x = _moe(x1, h2, aff_t, mod[0], gfin, moe_w_gate[0].astype(BF16), moe_w_up[0].astype(BF16),
             moe_w_down[0].astype(BF16), groups, final_norm=False)

    x1, h2, aff_t = _sgu_layer(x, mod[1], g_mix[1].reshape(1, D), b_w_in[0].astype(BF16),
                               b_ln_g[0].reshape(1, SGU_HALF), b_ln_b[0].reshape(1, SGU_HALF),
                               b_w_s[0].astype(BF16), b_b_s[0].reshape(SGU_HEADS, CHUNK, 1),
                               b_w_out[0].astype(BF16), g_ffn[1].reshape(1, D), wr_t[1], groups)
    y1, y2 = _moe(x1, h2, aff_t, mod[1], gfin, moe_w_gate[1].astype(BF16), moe_w_up[1].astype(BF16),
                  moe_w_down[1].astype(BF16), groups, final_norm=True)

    return (y1.reshape(x_prompt.shape), y2.reshape(x_sample.shape))
```

```python
import functools

import numpy as np
import jax
import jax.numpy as jnp
from jax import lax
from jax.experimental import pallas as pl
from jax.experimental.pallas import tpu as pltpu

F32 = jnp.float32
BF16 = jnp.bfloat16

D = 1024
N_HEADS = 16
HEAD_DIM = 64
DILATIONS = (1, 4, 16)
WINDOWS = (128, 512, 2048)
RADIUS = 64
N_GROUPS = 3
QKV_COLS = N_GROUPS * 3 * D
ROPE_THETA = 10000.0
CHUNK = 128
SGU_HALF = 3 * D
SGU_HEADS = 16
SGU_HEAD_CH = SGU_HALF // SGU_HEADS
N_EXPERTS = 16
CAPACITY_FACTOR = 2
D_EXPERT = 2 * D
NORM_EPS = 1e-6
NEG_BIG = -1e30

LANES = 128
VMEM_LIMIT = 56 * 1024 * 1024

TM_PROJ = 512
TM_SGU = 256
TM_COMB = 256
TQ = 256
UQ = 128
FFN_TILE = 256
F_CHUNK = 512
WIN = 64
SLOT_CHUNK = 512


def _cparams(sem):
    return pltpu.CompilerParams(dimension_semantics=sem, vmem_limit_bytes=VMEM_LIMIT)


def _split2(x):
    hi = x.astype(BF16)
    lo = (x - hi.astype(F32)).astype(BF16)
    return hi, lo


def _split3(x):
    hi = x.astype(BF16)
    r1 = x - hi.astype(F32)
    mid = r1.astype(BF16)
    lo = (r1 - mid.astype(F32)).astype(BF16)
    return hi, mid, lo


def _dot(a, b):
    return jnp.dot(a, b, preferred_element_type=F32)


def _dot_nt(a, b):
    return lax.dot_general(a, b, (((1,), (1,)), ((), ())), preferred_element_type=F32)


def _dot3(a, b):
    ah, al = _split2(a)
    bh, bl = _split2(b)
    return _dot(ah, bh) + (_dot(ah, bl) + _dot(al, bh))


def _dot3_nt(a, b):
    ah, al = _split2(a)
    bh, bl = _split2(b)
    return _dot_nt(ah, bh) + (_dot_nt(ah, bl) + _dot_nt(al, bh))


def _sigmoid(x):
    return 1.0 / (1.0 + jnp.exp(-x))


def _gelu_tanh(x):
    c = np.float32(np.sqrt(2.0 / np.pi))
    return 0.5 * x * (1.0 + jnp.tanh(c * (x + np.float32(0.044715) * (x * x * x))))


def _norm_mod(x, g, shift, scale):
    ms = jnp.mean(x * x, axis=-1, keepdims=True)
    return (x * lax.rsqrt(ms + NORM_EPS) * g) * (1.0 + scale) + shift


def _tile_tables(groups, tm):
    tb, tp = [], []
    b0 = 0
    for (B, S) in groups:
        assert S % tm == 0
        for b in range(B):
            for k in range(S // tm):
                tb.append(b0 + b)
                tp.append(k)
        b0 += B
    return jnp.asarray(np.array(tb, np.int32)), jnp.asarray(np.array(tp, np.int32))


def _mod_kernel(c_ref, w_ref, b_ref, o_ref):
    c = c_ref[...]
    o_ref[...] = _dot3(c * _sigmoid(c), w_ref[...]) + b_ref[...]


def _modulation(c_pad, w_mod, b_mod):
    depth, _, n = w_mod.shape
    tn = 2048
    rows = c_pad.shape[0]
    return pl.pallas_call(
        _mod_kernel,
        grid=(depth, n // tn),
        in_specs=[
            pl.BlockSpec((rows, D), lambda l, j: (0, 0)),
            pl.BlockSpec((None, D, tn), lambda l, j: (l, 0, j)),
            pl.BlockSpec((None, 1, tn), lambda l, j: (l, 0, j)),
        ],
        out_specs=pl.BlockSpec((None, rows, tn), lambda l, j: (l, 0, j)),
        out_shape=jax.ShapeDtypeStruct((depth, rows, n), F32),
        compiler_params=_cparams(("parallel", "parallel")),
        name="modulation",
    )(c_pad, w_mod, b_mod.reshape(depth, 1, n))


def _permute_rows(src_ref, slab, dst, r):
    tm, cols = src_ref.shape
    n = tm // r
    for c in range(cols // LANES):
        slab[c] = src_ref[:, c * LANES:(c + 1) * LANES]
    for res in range(r):
        for c in range(cols // LANES):
            dst[res * n:(res + 1) * n, c * LANES:(c + 1) * LANES] = slab[c, pl.ds(res, n, stride=r), :]


def _qkv_kernel(tb_ref, tp_ref, x_ref, mod_ref, g_ref, w_ref, cos_ref, sin_ref, o_ref, h_scr, *scr, r):
    m = mod_ref[...]
    if r == 1:
        x = x_ref[...]
    else:
        slab, xp = scr
        _permute_rows(x_ref, slab, xp, r)
        x = xp[...]
    h_scr[...] = _norm_mod(x, g_ref[...], m[0:1], m[1:2]).astype(BF16)

    n = h_scr.shape[0] // r
    nw = 2 * LANES
    cos = cos_ref[...]
    sin = sin_ref[...]
    qscale = np.float32(HEAD_DIM ** -0.5)
    tables = ((cos * qscale, sin * qscale), (cos, sin), None)
    for sec, table in enumerate(tables):
        for cw in range(D // nw):
            col0 = sec * D + cw * nw
            acc = _dot(h_scr[...], w_ref[:, col0:col0 + nw])
            for c in range(nw // LANES):
                xc = acc[:, c * LANES:(c + 1) * LANES]
                if table is not None:
                    xc = xc * table[0] + pltpu.roll(xc, LANES // 2, 1) * table[1]
                cs = slice(col0 + c * LANES, col0 + (c + 1) * LANES)
                o_ref[:, :, cs] = xc.reshape(r, n, LANES).astype(BF16)


def _qkv_proj(x, mod, g, w_bf, cos_p, sin_p, groups, r):
    M = x.shape[0]
    tm = TM_PROJ
    tb, tp = _tile_tables(groups, tm)
    scratch = [pltpu.VMEM((tm, D), BF16)]
    if r > 1:
        scratch += [pltpu.VMEM((D // LANES, tm, LANES), F32), pltpu.VMEM((tm, D), F32)]
    grid_spec = pltpu.PrefetchScalarGridSpec(
        num_scalar_prefetch=2,
        grid=(M // tm,),
        in_specs=[
            pl.BlockSpec((tm, D), lambda i, tb, tp: (i, 0)),
            pl.BlockSpec((None, 6, D), lambda i, tb, tp: (tb[i], 0, 0)),
            pl.BlockSpec((1, D), lambda i, tb, tp: (0, 0)),
            pl.BlockSpec((D, 3 * D), lambda i, tb, tp: (0, 0), pipeline_mode=pl.Buffered(1)),
            pl.BlockSpec((tm, LANES), lambda i, tb, tp: (tp[i], 0)),
            pl.BlockSpec((tm, LANES), lambda i, tb, tp: (tp[i], 0)),
        ],
        out_specs=pl.BlockSpec((r, tm // r, 3 * D), lambda i, tb, tp: (0, i, 0)),
        scratch_shapes=scratch,
    )
    return pl.pallas_call(
        functools.partial(_qkv_kernel, r=r),
        grid_spec=grid_spec,
        out_shape=jax.ShapeDtypeStruct((r, M // r, 3 * D), BF16),
        compiler_params=_cparams(("parallel",)),
        name=f"qkv_rope_d{r}",
    )(tb, tp, x, mod, g, w_bf, cos_p, sin_p)


def _rope_tables(s_max):
    half = HEAD_DIM // 2
    inv = ROPE_THETA ** (-jnp.arange(0, HEAD_DIM, 2, dtype=F32) / HEAD_DIM)
    ang = jnp.arange(s_max, dtype=F32)[:, None] * inv[None, :]
    cos, sin = jnp.cos(ang), jnp.sin(ang)
    cos_t = jnp.tile(cos, (1, LANES // half))
    sin_t = jnp.concatenate([-sin, -sin, sin, sin], axis=1)
    return cos_t, sin_t


def _pair_heads(w):
    half = HEAD_DIM // 2
    perm = np.concatenate([np.arange(0, half), np.arange(HEAD_DIM, HEAD_DIM + half),
                           np.arange(half, HEAD_DIM), np.arange(HEAD_DIM + half, LANES)])
    k = w.shape[0]
    qk = w[:, :2 * D].reshape(k, 2 * D // LANES, LANES)[:, :, perm].reshape(k, 2 * D)
    return jnp.concatenate([qk, w[:, 2 * D:]], axis=1)


def _permute_table(t, tm, r):
    s = t.shape[0]
    return t.reshape(s // tm, tm // r, r, LANES).transpose(0, 2, 1, 3).reshape(s, LANES)


def _attn_kernel(q_ref, kp_ref, kc_ref, kn_ref, vp_ref, vc_ref, vn_ref, o_ref, l_ref,
                 kw, vw, s_scr, p_scr, *, seq_blocks):
    i = pl.program_id(1)
    (n1, nb1), (n2, nb2) = seq_blocks
    first_grp = i < n1 * nb1
    pos = jnp.where(first_grp, i % nb1, (i - n1 * nb1) % nb2)
    nb = jnp.where(first_grp, nb1, nb2)
    has_prev = pos > 0
    has_next = pos < nb - 1

    W = RADIUS
    kw[0:W] = kp_ref[...]
    kw[W:W + TQ] = kc_ref[...]
    kw[W + TQ:W + TQ + W] = kn_ref[...]
    vw[0:W] = vp_ref[...]
    vw[W:W + TQ] = vc_ref[...]
    vw[W + TQ:W + TQ + W] = vn_ref[...]

    lane = lax.broadcasted_iota(jnp.int32, (UQ, LANES), 1)
    low_half = lane < HEAD_DIM
    head_a = (lane % HEAD_DIM) < (HEAD_DIM // 2)
    nu = TQ // UQ
    for u in range(nu):
        for hp in range(N_HEADS // 2):
            cs = slice(hp * LANES, (hp + 1) * LANES)
            q2 = q_ref[u * UQ:(u + 1) * UQ, cs]
            kwin = kw[u * UQ:u * UQ + 2 * UQ, cs]
            zero = jnp.zeros_like(q2)
            s_scr[u, 2 * hp] = _dot_nt(jnp.where(head_a, q2, zero), kwin)
            s_scr[u, 2 * hp + 1] = _dot_nt(jnp.where(head_a, zero, q2), kwin)

    rr = lax.broadcasted_iota(jnp.int32, (UQ, 2 * UQ), 0)
    cc = lax.broadcasted_iota(jnp.int32, (UQ, 2 * UQ), 1)
    band = jnp.abs((cc - W) - rr) <= RADIUS
    l_ref[...] = jnp.zeros(l_ref.shape, F32)
    for u in range(nu):
        valid = band
        if u == 0:
            valid = valid & ((cc >= W) | has_prev)
        if u == nu - 1:
            valid = valid & ((cc < 2 * UQ - W) | has_next)
        s = jnp.where(valid[None], s_scr[u], NEG_BIG)
        m = jnp.max(s, axis=-1, keepdims=True)
        p = jnp.exp(s - m)
        den = jnp.sum(p, axis=-1, keepdims=True)
        p_scr[u] = p.astype(BF16)
        rden = 1.0 / den
        lse = m + jnp.log(den)
        rows = slice(u * UQ, (u + 1) * UQ)
        for hp in range(N_HEADS // 2):
            cs = slice(hp * LANES, (hp + 1) * LANES)
            vwin = vw[u * UQ:u * UQ + 2 * UQ, cs]
            oa = _dot(p_scr[u, 2 * hp], vwin) * rden[2 * hp]
            ob = _dot(p_scr[u, 2 * hp + 1], vwin) * rden[2 * hp + 1]
            o_ref[rows, cs] = jnp.where(low_half, oa, ob).astype(BF16)
        for h in range(N_HEADS):
            l_ref[rows, h:h + 1] = lse[h]


def _attention_group(qkv_g, groups):
    r, rows, _ = qkv_g.shape
    W = RADIUS
    for (B, S) in groups:
        assert S % (r * TQ) == 0
    seq_blocks = tuple((B, S // r // TQ) for (B, S) in groups)
    q64 = TQ // W
    last64 = rows // W - 1

    def big(col):
        return pl.BlockSpec((None, TQ, D), lambda res, i: (res, i, col))

    def prv(col):
        return pl.BlockSpec((None, W, D), lambda res, i: (res, jnp.maximum(i * q64 - 1, 0), col))

    def nxt(col):
        return pl.BlockSpec((None, W, D), lambda res, i: (res, jnp.minimum((i + 1) * q64, last64), col))

    return pl.pallas_call(
        functools.partial(_attn_kernel, seq_blocks=seq_blocks),
        grid=(r, rows // TQ),
        in_specs=[big(0), prv(1), big(1), nxt(1), prv(2), big(2), nxt(2)],
        out_specs=[pl.BlockSpec((None, TQ, D), lambda res, i: (res, i, 0)),
                   pl.BlockSpec((None, TQ, LANES), lambda res, i: (res, i, 0))],
        out_shape=[jax.ShapeDtypeStruct((r, rows, D), BF16),
                   jax.ShapeDtypeStruct((r, rows, LANES), F32)],
        scratch_shapes=[pltpu.VMEM((TQ + 2 * W, D), BF16), pltpu.VMEM((TQ + 2 * W, D), BF16),
                        pltpu.VMEM((TQ // UQ, N_HEADS, UQ, 2 * UQ), F32),
                        pltpu.VMEM((TQ // UQ, N_HEADS, UQ, 2 * UQ), BF16)],
        compiler_params=_cparams(("parallel", "parallel")),
        name=f"attn_d{r}",
    )(*([qkv_g] * 7))


def _router(x1, m, g_ffn, wr_t):
    h2 = _norm_mod(x1, g_ffn, m[3:4], m[4:5])
    logits_t = _dot3_nt(wr_t, h2)
    mx = jnp.max(logits_t, axis=0, keepdims=True)
    e = jnp.exp(logits_t - mx)
    return h2, e / jnp.sum(e, axis=0, keepdims=True)


def _oproj_kernel(tb_ref, o1_ref, o4_ref, o16_ref, l1_ref, l4_ref, l16_ref, x_ref, mod_ref, w_ref,
                  gf_ref, wr_ref, x1_ref, h2_ref, aff_ref, oslab, lslab, om_scr):
    tm = x_ref.shape[0]
    nc = D // LANES
    for g, (r, o_ref, l_ref) in enumerate(zip(DILATIONS, (o1_ref, o4_ref, o16_ref),
                                              (l1_ref, l4_ref, l16_ref))):
        n = tm // r
        for res in range(r):
            rows = slice(None) if r == 1 else pl.ds(res, n, stride=r)
            lslab[g, 0, rows, :] = l_ref[res]
            for c in range(nc):
                oslab[g, c, rows, :] = o_ref[res, :, c * LANES:(c + 1) * LANES].astype(F32)
    ls = [lslab[g, 0] for g in range(N_GROUPS)]
    mx = jnp.maximum(jnp.maximum(ls[0], ls[1]), ls[2])
    es = [jnp.exp(l - mx) for l in ls]
    tot = es[0] + es[1] + es[2]
    hrow = lax.broadcasted_iota(jnp.int32, (LANES, D), 0)
    hcol = lax.broadcasted_iota(jnp.int32, (LANES, D), 1)
    expand = jnp.where(hcol // HEAD_DIM == hrow, 1.0, 0.0).astype(BF16)
    wexp = []
    for g in range(N_GROUPS):
        wh, wl = _split2(es[g] / tot)
        wexp.append(_dot(wh, expand) + _dot(wl, expand))
    for c in range(nc):
        cs = slice(c * LANES, (c + 1) * LANES)
        om = wexp[0][:, cs] * oslab[0, c] + wexp[1][:, cs] * oslab[1, c] + wexp[2][:, cs] * oslab[2, c]
        om_scr[:, cs] = om.astype(BF16)
    m = mod_ref[...]
    y = _dot(om_scr[...], w_ref[...])
    x1 = x_ref[...] + m[2:3] * y
    x1_ref[...] = x1
    h2, aff_t = _router(x1, m, gf_ref[...], wr_ref[...])
    h2_ref[...] = h2
    aff_ref[...] = aff_t


def _out_proj(os_, ls_, x, mod, w_bf, g_ffn, wr_t, groups):
    M = x.shape[0]
    tm = TM_PROJ
    tb, _ = _tile_tables(groups, tm)
    o_specs = [pl.BlockSpec((r, tm // r, D), lambda i, tb: (0, i, 0)) for r in DILATIONS]
    l_specs = [pl.BlockSpec((r, tm // r, LANES), lambda i, tb: (0, i, 0)) for r in DILATIONS]
    grid_spec = pltpu.PrefetchScalarGridSpec(
        num_scalar_prefetch=1,
        grid=(M // tm,),
        in_specs=o_specs + l_specs + [
            pl.BlockSpec((tm, D), lambda i, tb: (i, 0)),
            pl.BlockSpec((None, 6, D), lambda i, tb: (tb[i], 0, 0)),
            pl.BlockSpec((D, D), lambda i, tb: (0, 0)),
            pl.BlockSpec((1, D), lambda i, tb: (0, 0)),
            pl.BlockSpec((N_EXPERTS, D), lambda i, tb: (0, 0)),
        ],
        out_specs=[
            pl.BlockSpec((tm, D), lambda i, tb: (i, 0)),
            pl.BlockSpec((tm, D), lambda i, tb: (i, 0)),
            pl.BlockSpec((N_EXPERTS, tm), lambda i, tb: (0, i)),
        ],
        scratch_shapes=[pltpu.VMEM((N_GROUPS, D // LANES, tm, LANES), F32),
                        pltpu.VMEM((N_GROUPS, 1, tm, LANES), F32),
                        pltpu.VMEM((tm, D), BF16)],
    )
    return pl.pallas_call(
        _oproj_kernel,
        grid_spec=grid_spec,
        out_shape=[jax.ShapeDtypeStruct((M, D), F32), jax.ShapeDtypeStruct((M, D), F32),
                   jax.ShapeDtypeStruct((N_EXPERTS, M), F32)],
        compiler_params=_cparams(("parallel",)),
        name="attn_out_router",
    )(tb, *os_, *ls_, x, mod, w_bf, g_ffn, wr_t)


def _sgu_kernel(tb_ref, x_ref, mod_ref, g_ref, win_ref, lng_ref, lnb_ref, ws_ref, bs_ref, wout_ref,
                gf_ref, wr_ref, x1_ref, h2_ref, aff_ref, h_scr, v_scr, g_scr):
    m = mod_ref[...]
    x = x_ref[...]
    h_scr[...] = _norm_mod(x, g_ref[...], m[0:1], m[1:2]).astype(BF16)
    tm = x.shape[0]
    nc = 512
    s1 = jnp.zeros((tm, 1), F32)
    s2 = jnp.zeros((tm, 1), F32)
    for c in range(SGU_HALF // nc):
        z = _gelu_tanh(_dot(h_scr[...], win_ref[:, SGU_HALF + c * nc:SGU_HALF + (c + 1) * nc]))
        v_scr[:, c * nc:(c + 1) * nc] = z
        s1 = s1 + jnp.sum(z, axis=-1, keepdims=True)
    mu = s1 * np.float32(1.0 / SGU_HALF)
    for c in range(SGU_HALF // nc):
        d = v_scr[:, c * nc:(c + 1) * nc] - mu
        s2 = s2 + jnp.sum(d * d, axis=-1, keepdims=True)
    rstd = lax.rsqrt(s2 * np.float32(1.0 / SGU_HALF) + NORM_EPS)
    for c in range(SGU_HALF // nc):
        cs = slice(c * nc, (c + 1) * nc)
        v_scr[:, cs] = (v_scr[:, cs] - mu) * rstd * lng_ref[:, cs] + lnb_ref[:, cs]
    for hd in range(SGU_HEADS):
        cs = slice(hd * SGU_HEAD_CH, (hd + 1) * SGU_HEAD_CH)
        u = _gelu_tanh(_dot(h_scr[...], win_ref[:, cs]))
        for k in range(tm // CHUNK):
            rs = slice(k * CHUNK, (k + 1) * CHUNK)
            sv = _dot(ws_ref[hd], v_scr[rs, cs].astype(BF16)) + bs_ref[hd]
            g_scr[rs, cs] = (u[rs] * sv).astype(BF16)
    y = _dot(g_scr[...], wout_ref[...])
    x1 = x + m[2:3] * y
    x1_ref[...] = x1
    h2, aff_t = _router(x1, m, gf_ref[...], wr_ref[...])
    h2_ref[...] = h2
    aff_ref[...] = aff_t


def _sgu_layer(x, mod, g_mix, w_in_bf, ln_g, ln_b, w_s_bf, b_s_col, w_out_bf, g_ffn, wr_t, groups):
    M = x.shape[0]
    tm = TM_SGU
    tb, _ = _tile_tables(groups, tm)
    const = lambda *shape: pl.BlockSpec(shape, lambda i, tb: (0,) * len(shape),
                                        pipeline_mode=pl.Buffered(1))
    grid_spec = pltpu.PrefetchScalarGridSpec(
        num_scalar_prefetch=1,
        grid=(M // tm,),
        in_specs=[
            pl.BlockSpec((tm, D), lambda i, tb: (i, 0)),
            pl.BlockSpec((None, 6, D), lambda i, tb: (tb[i], 0, 0)),
            const(1, D),
            const(D, 2 * SGU_HALF),
            const(1, SGU_HALF),
            const(1, SGU_HALF),
            const(SGU_HEADS, CHUNK, CHUNK),
            const(SGU_HEADS, CHUNK, 1),
            const(SGU_HALF, D),
            const(1, D),
            const(N_EXPERTS, D),
        ],
        out_specs=[
            pl.BlockSpec((tm, D), lambda i, tb: (i, 0)),
            pl.BlockSpec((tm, D), lambda i, tb: (i, 0)),
            pl.BlockSpec((N_EXPERTS, tm), lambda i, tb: (0, i)),
        ],
        scratch_shapes=[pltpu.VMEM((tm, D), BF16), pltpu.VMEM((tm, SGU_HALF), F32),
                        pltpu.VMEM((tm, SGU_HALF), BF16)],
    )
    return pl.pallas_call(
        _sgu_kernel,
        grid_spec=grid_spec,
        out_shape=[jax.ShapeDtypeStruct((M, D), F32), jax.ShapeDtypeStruct((M, D), F32),
                   jax.ShapeDtypeStruct((N_EXPERTS, M), F32)],
        compiler_params=_cparams(("parallel",)),
        name="sgu_router",
    )(tb, x, mod, g_mix, w_in_bf, ln_g, ln_b, w_s_bf, b_s_col, w_out_bf, g_ffn, wr_t)


def _topk_kernel(aff_ref, idx_ref, gate_ref, pos_ref, bst_ref, incl_scr, tot_scr, bex_scr, *, nb, cap,
                 tok_base, slot_base):
    E = N_EXPERTS
    a = aff_ref[...]
    bits = pltpu.bitcast(a, jnp.int32)

    def count(mask):
        c = jnp.sum(jnp.where(mask, 1.0, 0.0), axis=1, keepdims=True)
        return jnp.sum(c, axis=2, keepdims=True)

    def search(k, thr):
        cand = thr | jnp.left_shift(jnp.int32(1), 30 - k)
        return jnp.where(count(bits >= cand) >= cap, cand, thr)

    thr = lax.fori_loop(0, 31, search, jnp.zeros((E, 1, 1), jnp.int32))
    gt = bits > thr
    eq = bits == thr
    need = cap - count(gt)

    row = lax.broadcasted_iota(jnp.int32, (LANES, LANES), 0)
    col = lax.broadcasted_iota(jnp.int32, (LANES, LANES), 1)
    upper = jnp.where(row <= col, 1.0, 0.0).astype(BF16)
    ones = jnp.ones((LANES, LANES), BF16)
    brow = lax.broadcasted_iota(jnp.int32, (nb, nb), 0)
    bcol = lax.broadcasted_iota(jnp.int32, (nb, nb), 1)
    lower_strict = jnp.where(bcol < brow, 1.0, 0.0).astype(BF16)

    def prefix(mask):
        xb = jnp.where(mask, 1.0, 0.0).astype(BF16).reshape(E * nb, LANES)
        incl_scr[...] = _dot(xb, upper).reshape(E, nb, LANES)
        tot_scr[...] = _dot(xb, ones).reshape(E, nb, LANES)
        for e in range(E):
            bex_scr[e] = _dot(lower_strict, tot_scr[e].astype(BF16))

    prefix(eq)
    rank_eq = bex_scr[...] + incl_scr[...] - jnp.where(eq, 1.0, 0.0)
    sel = gt | (eq & (rank_eq < need))
    prefix(sel)
    self32 = jnp.where(sel, 1.0, 0.0)
    pos = bex_scr[...] + incl_scr[...] - self32
    pos_ref[...] = jnp.where(sel, pos + np.float32(slot_base), -1.0)
    bst_ref[...] = bex_scr[...] + np.float32(slot_base)

    sc = SLOT_CHUNK
    bidx = lax.broadcasted_iota(jnp.int32, (nb, sc), 0).astype(F32)
    jidx = lax.broadcasted_iota(jnp.int32, (LANES, sc), 0).astype(F32)
    for e in range(E):
        tot_e = tot_scr[e][:, 0:1]
        binc_e = bex_scr[e][:, 0:1] + tot_e
        incl_t = jnp.transpose(incl_scr[e]).astype(BF16)
        aff_parts = _split3(jnp.transpose(a[e]))
        for c in range(cap // sc):
            s = (lax.broadcasted_iota(jnp.int32, (nb, sc), 1) + c * sc).astype(F32)
            before = binc_e <= s
            blk = jnp.sum(jnp.where(before, 1.0, 0.0), axis=0, keepdims=True)
            base = jnp.sum(jnp.where(before, tot_e, 0.0), axis=0, keepdims=True)
            local = s[0:1] - base
            onehot = jnp.where(bidx == blk, 1.0, 0.0).astype(BF16)
            rows = _dot(incl_t, onehot)
            cnt = jnp.sum(jnp.where(rows <= local, 1.0, 0.0), axis=0, keepdims=True)
            tok = blk * np.float32(LANES) + cnt + np.float32(tok_base)
            idx_ref[e:e + 1, c * sc:(c + 1) * sc] = tok.astype(jnp.int32)
            arow = (_dot(aff_parts[0], onehot) + _dot(aff_parts[1], onehot)) + _dot(aff_parts[2], onehot)
            gate = jnp.sum(jnp.where(jidx == cnt, arow, 0.0), axis=0, keepdims=True)
            gate_col = jnp.transpose(jnp.broadcast_to(gate, (8, sc)))[:, 0:1]
            gate_ref[e, c * sc:(c + 1) * sc, :] = jnp.broadcast_to(gate_col, (sc, LANES))


def _topk_group(aff_t, tok_base, n, slot_base):
    E = N_EXPERTS
    M = aff_t.shape[1]
    nb = n // LANES
    cap = CAPACITY_FACTOR * n // E
    assert n % LANES == 0 and nb % 8 == 0 and cap % SLOT_CHUNK == 0 and tok_base % n == 0
    a3 = aff_t.reshape(E, M // LANES, LANES)
    blk = tok_base // n
    return pl.pallas_call(
        functools.partial(_topk_kernel, nb=nb, cap=cap, tok_base=tok_base, slot_base=slot_base),
        grid=(1,),
        in_specs=[pl.BlockSpec((E, nb, LANES), lambda i: (0, blk, 0))],
        out_specs=[pl.BlockSpec((E, cap), lambda i: (0, 0)),
                   pl.BlockSpec((E, cap, LANES), lambda i: (0, 0, 0)),
                   pl.BlockSpec((E, nb, LANES), lambda i: (0, 0, 0)),
                   pl.BlockSpec((E, nb, LANES), lambda i: (0, 0, 0))],
        out_shape=[jax.ShapeDtypeStruct((E, cap), jnp.int32),
                   jax.ShapeDtypeStruct((E, cap, LANES), F32),
                   jax.ShapeDtypeStruct((E, nb, LANES), F32),
                   jax.ShapeDtypeStruct((E, nb, LANES), F32)],
        scratch_shapes=[pltpu.VMEM((E, nb, LANES), F32)] * 3,
        compiler_params=_cparams(("arbitrary",)),
        name=f"expert_choice_topk_{n}",
    )(a3)


def _ffn_kernel(idx_ref, h_hbm, gate_ref, wg_hbm, wu_hbm, wd_hbm, ye_ref, xbuf, x_scr, wg_buf, wu_buf, wd_buf,
                sem, wsem, *, layer, tiles_per_expert, n_steps):
    e = pl.program_id(0)
    j = pl.program_id(1)
    step = e * tiles_per_expert + j
    slot = step % 2
    wslot = e % 2

    def weight_copies(expert, ws):
        return (pltpu.make_async_copy(wg_hbm.at[layer, expert], wg_buf.at[ws], wsem.at[ws, 0]),
                pltpu.make_async_copy(wu_hbm.at[layer, expert], wu_buf.at[ws], wsem.at[ws, 1]),
                pltpu.make_async_copy(wd_hbm.at[layer, expert], wd_buf.at[ws], wsem.at[ws, 2]))

    def row_copy(tile, k, slot_):
        t = idx_ref[tile * FFN_TILE + k]
        return pltpu.make_async_copy(h_hbm.at[pl.ds(t, 1), :], xbuf.at[slot_, pl.ds(k, 1), :],
                                     sem.at[slot_])

    @pl.when(step == 0)
    def _():
        for cp in weight_copies(0, 0):
            cp.start()

        def issue(k, carry):
            row_copy(0, k, 0).start()
            return carry
        lax.fori_loop(0, FFN_TILE, issue, 0, unroll=8)

    @pl.when(j == 0)
    def _():
        for cp in weight_copies(e, wslot):
            cp.wait()

        @pl.when(e + 1 < pl.num_programs(0))
        def _():
            for cp in weight_copies(e + 1, 1 - wslot):
                cp.start()

    for k in range(FFN_TILE):
        row_copy(step, k, slot).wait()
    nxt = jnp.minimum(step + 1, n_steps - 1)

    def compute(ws):
        x_scr[...] = xbuf[slot].astype(BF16)
        for k in range(FFN_TILE):
            row_copy(nxt, k, 1 - slot).start()
        acc = jnp.zeros((FFN_TILE, D), F32)
        for c in range(D_EXPERT // F_CHUNK):
            cs = slice(c * F_CHUNK, (c + 1) * F_CHUNK)
            hg = _dot(x_scr[...], wg_buf[ws, :, cs])
            hu = _dot(x_scr[...], wu_buf[ws, :, cs])
            act = (hg * _sigmoid(hg) * hu).astype(BF16)
            acc = acc + _dot(act, wd_buf[ws, cs, :])
        gate = gate_ref[...]
        for c in range(D // LANES):
            cs = slice(c * LANES, (c + 1) * LANES)
            ye_ref[:, cs] = (acc[:, cs] * gate).astype(BF16)

    for ws in range(2):
        @pl.when(wslot == ws)
        def _():
            compute(ws)

    @pl.when(step == n_steps - 1)
    def _():
        for k in range(FFN_TILE):
            row_copy(nxt, k, 1 - slot).wait()


def _expert_ffn(idx_flat, h2, gate_all, wg_bf, wu_bf, wd_bf, layer, slots):
    E = N_EXPERTS
    tiles = slots // FFN_TILE
    grid_spec = pltpu.PrefetchScalarGridSpec(
        num_scalar_prefetch=1,
        grid=(E, tiles),
        in_specs=[
            pl.BlockSpec(memory_space=pl.ANY),
            pl.BlockSpec((FFN_TILE, LANES), lambda e, j, idx: (e * tiles + j, 0)),
            pl.BlockSpec(memory_space=pl.ANY),
            pl.BlockSpec(memory_space=pl.ANY),
            pl.BlockSpec(memory_space=pl.ANY),
        ],
        out_specs=pl.BlockSpec((None, FFN_TILE, D), lambda e, j, idx: (e, j, 0)),
        scratch_shapes=[pltpu.VMEM((2, FFN_TILE, D), F32), pltpu.VMEM((FFN_TILE, D), BF16),
                        pltpu.VMEM((2, D, D_EXPERT), BF16), pltpu.VMEM((2, D, D_EXPERT), BF16),
                        pltpu.VMEM((2, D_EXPERT, D), BF16),
                        pltpu.SemaphoreType.DMA((2,)), pltpu.SemaphoreType.DMA((2, 3))],
    )
    return pl.pallas_call(
        functools.partial(_ffn_kernel, layer=layer, tiles_per_expert=tiles, n_steps=E * tiles),
        grid_spec=grid_spec,
        out_shape=jax.ShapeDtypeStruct((E, slots, D), BF16),
        compiler_params=_cparams(("arbitrary", "arbitrary")),
        name="expert_ffn",
    )(idx_flat, h2, gate_all.reshape(E * slots, LANES), wg_bf, wu_bf, wd_bf)


def _combine_kernel(tb_ref, bst_ref, x_ref, pos_ref, mod_ref, gfin_ref, ye_hbm, *rest, slots, blocks,
                    n_tiles, final_norm, split):
    if split is None:
        (o_ref, win, winx, acc, sem, semx) = rest
    else:
        (o_ref, o2_ref, win, winx, acc, sem, semx) = rest
    E = N_EXPERTS
    i = pl.program_id(0)
    bpt = TM_COMB // LANES
    stride = blocks + 1

    def tile_start(tile, e):
        return (bst_ref[e * stride + tile * bpt] // 16) * 16

    def window_copy(e, lo, buf, sem_):
        w0 = pl.multiple_of(jnp.minimum(lo, slots - WIN), 16)
        return pltpu.make_async_copy(ye_hbm.at[e, pl.ds(w0, WIN), :], buf.at[pl.ds(e * WIN, WIN), :],
                                     sem_.at[e])

    slot = i % 2

    @pl.when(i == 0)
    def _():
        for e in range(E):
            window_copy(e, tile_start(0, e), win.at[0], sem.at[0]).start()

    starts = [tile_start(i, e) for e in range(E)]
    npass = jnp.int32(1)
    for e in range(E):
        end = bst_ref[e * stride + (i + 1) * bpt]
        npass = jnp.maximum(npass, (end - starts[e] + (WIN - 1)) // WIN)

    for e in range(E):
        window_copy(e, starts[e], win.at[slot], sem.at[slot]).wait()
    nxt = jnp.minimum(i + 1, n_tiles - 1)
    for e in range(E):
        window_copy(e, tile_start(nxt, e), win.at[1 - slot], sem.at[1 - slot]).start()

    lane_e = lax.broadcasted_iota(jnp.int32, (1, E), 1)
    erow = lax.broadcasted_iota(jnp.int32, (E, E * WIN), 0)
    ecol = lax.broadcasted_iota(jnp.int32, (E, E * WIN), 1)
    expand = jnp.where(ecol // WIN == erow, 1.0, 0.0).astype(BF16)
    sprime = (lax.broadcasted_iota(jnp.int32, (1, E * WIN), 1) % WIN).astype(F32)
    pos = pos_ref[...]

    def scatter_rows(p, wbuf):
        lo_vec = jnp.zeros((1, E), F32)
        w0_vec = jnp.zeros((1, E), F32)
        for e in range(E):
            lo = starts[e] + p * WIN
            w0 = jnp.minimum(lo, slots - WIN)
            lo_vec = jnp.where(lane_e == e, lo.astype(F32), lo_vec)
            w0_vec = jnp.where(lane_e == e, w0.astype(F32), w0_vec)
        rel = jnp.where(pos >= lo_vec, pos - w0_vec, -1.0)
        rel = jnp.clip(rel, -1.0, np.float32(WIN))
        relx = _dot(rel.astype(BF16), expand)
        onehot = jnp.where(relx == sprime, 1.0, 0.0).astype(BF16)
        return _dot(onehot, wbuf[...])

    acc[...] = scatter_rows(0, win.at[slot])

    def extra_pass(p, carry):
        copies = [window_copy(e, starts[e] + p * WIN, winx, semx) for e in range(E)]
        for cp in copies:
            cp.start()
        for cp in copies:
            cp.wait()
        acc[...] += scatter_rows(p, winx)
        return carry

    lax.fori_loop(1, npass, extra_pass, 0)

    m = mod_ref[...]
    y = x_ref[...] + m[5:6] * acc[...]
    if final_norm:
        ms = jnp.mean(y * y, axis=-1, keepdims=True)
        y = y * lax.rsqrt(ms + NORM_EPS) * gfin_ref[...]
    if split is None:
        o_ref[...] = y
    else:
        @pl.when(i < split)
        def _():
            o_ref[...] = y

        @pl.when(i >= split)
        def _():
            o2_ref[...] = y

    @pl.when(i == n_tiles - 1)
    def _():
        for e in range(E):
            window_copy(e, tile_start(nxt, e), win.at[1 - slot], sem.at[1 - slot]).wait()


def _combine(x1, pos_t, mod, g_final, ye, bst_flat, groups, final_norm):
    M = x1.shape[0]
    tm = TM_COMB
    E = N_EXPERTS
    slots = ye.shape[1]
    tb, _ = _tile_tables(groups, tm)
    n_tiles = M // tm
    n1 = groups[0][0] * groups[0][1]
    if final_norm:
        split = n1 // tm
        out_specs = [pl.BlockSpec((tm, D), lambda i, tb, bst: (jnp.minimum(i, split - 1), 0)),
                     pl.BlockSpec((tm, D), lambda i, tb, bst: (jnp.maximum(i - split, 0), 0))]
        out_shape = [jax.ShapeDtypeStruct((n1, D), F32), jax.ShapeDtypeStruct((M - n1, D), F32)]
    else:
        split = None
        out_specs = pl.BlockSpec((tm, D), lambda i, tb, bst: (i, 0))
        out_shape = jax.ShapeDtypeStruct((M, D), F32)
    grid_spec = pltpu.PrefetchScalarGridSpec(
        num_scalar_prefetch=2,
        grid=(n_tiles,),
        in_specs=[
            pl.BlockSpec((tm, D), lambda i, tb, bst: (i, 0)),
            pl.BlockSpec((tm, E), lambda i, tb, bst: (i, 0)),
            pl.BlockSpec((None, 6, D), lambda i, tb, bst: (tb[i], 0, 0)),
            pl.BlockSpec((1, D), lambda i, tb, bst: (0, 0)),
            pl.BlockSpec(memory_space=pl.ANY),
        ],
        out_specs=out_specs,
        scratch_shapes=[pltpu.VMEM((2, E * WIN, D), BF16), pltpu.VMEM((E * WIN, D), BF16),
                        pltpu.VMEM((tm, D), F32), pltpu.SemaphoreType.DMA((2, E)),
                        pltpu.SemaphoreType.DMA((E,))],
    )
    return pl.pallas_call(
        functools.partial(_combine_kernel, slots=slots, blocks=M // LANES, n_tiles=n_tiles,
                          final_norm=final_norm, split=split),
        grid_spec=grid_spec,
        out_shape=out_shape,
        compiler_params=_cparams(("arbitrary",)),
        name="moe_combine",
    )(tb, bst_flat, x1, pos_t, mod, g_final, ye)


def _moe(x1, h2, aff_t, mod, g_final, wg_bf, wu_bf, wd_bf, layer, groups, final_norm):
    E = N_EXPERTS
    idxs, gates, poss, bsts = [], [], [], []
    tok_base = 0
    slot_base = 0
    for (B, S) in groups:
        n = B * S
        idx, gate, pos, bst = _topk_group(aff_t, tok_base, n, slot_base)
        idxs.append(idx)
        gates.append(gate)
        poss.append(pos.reshape(E, n))
        bsts.append(bst[:, :, 0])
        tok_base += n
        slot_base += CAPACITY_FACTOR * n // E
    slots = slot_base
    idx_flat = jnp.concatenate(idxs, axis=1).reshape(-1)
    ye = _expert_ffn(idx_flat, h2, jnp.concatenate(gates, axis=1), wg_bf, wu_bf, wd_bf, layer, slots)
    pos_t = jnp.concatenate(poss, axis=1).T
    bst = jnp.concatenate(bsts + [jnp.full((E, 1), slots, F32)], axis=1)
    bst_flat = bst.astype(jnp.int32).reshape(-1)
    return _combine(x1, pos_t, mod, g_final, ye, bst_flat, groups, final_norm)


def kernel(x_prompt, x_sample, c_prompt, c_sample, w_mod, b_mod, g_mix, g_ffn, a_w_in, a_w_out, b_w_in,
           b_ln_g, b_ln_b, b_w_s, b_b_s, b_w_out, moe_w_router, moe_w_gate, moe_w_up, moe_w_down, g_final):
    groups = (x_prompt.shape[:2], x_sample.shape[:2])
    assert x_prompt.shape[2] == D and x_sample.shape[2] == D
    x = jnp.concatenate([x_prompt.reshape(-1, D), x_sample.reshape(-1, D)], axis=0)
    n_seq = groups[0][0] + groups[1][0]
    rows = -(-n_seq // 8) * 8
    c_all = jnp.concatenate([c_prompt, c_sample, jnp.zeros((rows - n_seq, D), F32)], axis=0)
    mod = _modulation(c_all, w_mod, b_mod).reshape(w_mod.shape[0], rows, 6, D)
    cos_t, sin_t = _rope_tables(max(groups[0][1], groups[1][1]))
    wr_t = jnp.swapaxes(moe_w_router, 1, 2)
    gfin = g_final.reshape(1, D)

    os_, ls_ = [], []
    for g, r in enumerate(DILATIONS):
        w_g = _pair_heads(a_w_in[0][:, g * 3 * D:(g + 1) * 3 * D]).astype(BF16)
        qkv_g = _qkv_proj(x, mod[0], g_mix[0].reshape(1, D), w_g, _permute_table(cos_t, TM_PROJ, r),
                          _permute_table(sin_t, TM_PROJ, r), groups, r)
        o_g, l_g = _attention_group(qkv_g, groups)
        os_.append(o_g)
        ls_.append(l_g)
    x1, h2, aff_t = _out_proj(os_, ls_, x, mod[0], a_w_out[0].astype(BF16), g_ffn[0].reshape(1, D),
                              wr_t[0], groups)
    wg_bf, wu_bf, wd_bf = moe_w_gate.astype(BF16), moe_w_up.astype(BF16), moe_w_down.astype(BF16)
    x = _moe(x1, h2, aff_t, mod[0], gfin, wg_bf, wu_bf, wd_bf, 0, groups, final_norm=False)

    x1, h2, aff_t = _sgu_layer(x, mod[1], g_mix[1].reshape(1, D), b_w_in[0].astype(BF16),
                               b_ln_g[0].reshape(1, SGU_HALF), b_ln_b[0].reshape(1, SGU_HALF),
                               b_w_s[0].astype(BF16), b_b_s[0].reshape(SGU_HEADS, CHUNK, 1),
                               b_w_out[0].astype(BF16), g_ffn[1].reshape(1, D), wr_t[1], groups)
    y1, y2 = _moe(x1, h2, aff_t, mod[1], gfin, wg_bf, wu_bf, wd_bf, 1, groups, final_norm=True)

    return (y1.reshape(x_prompt.shape), y2.reshape(x_sample.shape))
```

```python
import functools

import numpy as np
import jax
import jax.numpy as jnp
from jax import lax
from jax.experimental import pallas as pl
from jax.experimental.pallas import tpu as pltpu

F32 = jnp.float32
BF16 = jnp.bfloat16

D = 1024
N_HEADS = 16
HEAD_DIM = 64
DILATIONS = (1, 4, 16)
WINDOWS = (128, 512, 2048)
RADIUS = 64
N_GROUPS = 3
QKV_COLS = N_GROUPS * 3 * D
ROPE_THETA = 10000.0
CHUNK = 128
SGU_HALF = 3 * D
SGU_HEADS = 16
SGU_HEAD_CH = SGU_HALF // SGU_HEADS
N_EXPERTS = 16
CAPACITY_FACTOR = 2
D_EXPERT = 2 * D
NORM_EPS = 1e-6
NEG_BIG = -1e30

LANES = 128
VMEM_LIMIT = 56 * 1024 * 1024

TM_PROJ = 512
TM_SGU = 256
TM_COMB = 256
TQ = 256
UQ = 128
FFN_TILE = 256
F_CHUNK = 512
WIN = 64
SLOT_CHUNK = 512
NSUB = D // LANES


def _cparams(sem):
    return pltpu.CompilerParams(dimension_semantics=sem, vmem_limit_bytes=VMEM_LIMIT)


def _split2(x):
    hi = x.astype(BF16)
    lo = (x - hi.astype(F32)).astype(BF16)
    return hi, lo


def _split3(x):
    hi = x.astype(BF16)
    r1 = x - hi.astype(F32)
    mid = r1.astype(BF16)
    lo = (r1 - mid.astype(F32)).astype(BF16)
    return hi, mid, lo


def _dot(a, b):
    return jnp.dot(a, b, preferred_element_type=F32)


def _dot_nt(a, b):
    return lax.dot_general(a, b, (((1,), (1,)), ((), ())), preferred_element_type=F32)


def _dot3(a, b):
    ah, al = _split2(a)
    bh, bl = _split2(b)
    return _dot(ah, bh) + (_dot(ah, bl) + _dot(al, bh))


def _dot3_nt(a, b):
    ah, al = _split2(a)
    bh, bl = _split2(b)
    return _dot_nt(ah, bh) + (_dot_nt(ah, bl) + _dot_nt(al, bh))


def _sigmoid(x):
    return 1.0 / (1.0 + jnp.exp(-x))


def _gelu_tanh(x):
    c = np.float32(np.sqrt(2.0 / np.pi))
    return 0.5 * x * (1.0 + jnp.tanh(c * (x + np.float32(0.044715) * (x * x * x))))


def _norm_mod(x, g, shift, scale):
    ms = jnp.mean(x * x, axis=-1, keepdims=True)
    return (x * lax.rsqrt(ms + NORM_EPS) * g) * (1.0 + scale) + shift


def _tile_tables(groups, tm):
    tb, tp = [], []
    b0 = 0
    for (B, S) in groups:
        assert S % tm == 0
        for b in range(B):
            for k in range(S // tm):
                tb.append(b0 + b)
                tp.append(k)
        b0 += B
    return jnp.asarray(np.array(tb, np.int32)), jnp.asarray(np.array(tp, np.int32))


def _mod_kernel(c_ref, w_ref, b_ref, o_ref):
    c = c_ref[...]
    o_ref[...] = _dot3(c * _sigmoid(c), w_ref[...]) + b_ref[...]


def _modulation(c_pad, w_mod, b_mod):
    depth, _, n = w_mod.shape
    tn = 2048
    rows = c_pad.shape[0]
    return pl.pallas_call(
        _mod_kernel,
        grid=(depth, n // tn),
        in_specs=[
            pl.BlockSpec((rows, D), lambda l, j: (0, 0)),
            pl.BlockSpec((None, D, tn), lambda l, j: (l, 0, j)),
            pl.BlockSpec((None, 1, tn), lambda l, j: (l, 0, j)),
        ],
        out_specs=pl.BlockSpec((None, rows, tn), lambda l, j: (l, 0, j)),
        out_shape=jax.ShapeDtypeStruct((depth, rows, n), F32),
        compiler_params=_cparams(("parallel", "parallel")),
        name="modulation",
    )(c_pad, w_mod, b_mod.reshape(depth, 1, n))


def _permute_rows(src, slab, dst, r):
    tm, cols = src.shape
    n = tm // r
    for c in range(cols // LANES):
        slab[c] = src[:, c * LANES:(c + 1) * LANES]
    for res in range(r):
        for c in range(cols // LANES):
            dst[res * n:(res + 1) * n, c * LANES:(c + 1) * LANES] = slab[c, pl.ds(res, n, stride=r), :]


def _two_group_specs(tm, n_first_tiles):
    first = pl.BlockSpec((tm, D), lambda i, *_: (jnp.minimum(i, n_first_tiles - 1), 0))
    second = pl.BlockSpec((tm, D), lambda i, *_: (jnp.maximum(i - n_first_tiles, 0), 0))
    return [first, second]


def _qkv_kernel(tb_ref, tp_ref, xa_ref, xb_ref, mod_ref, g_ref, w_ref, cos_ref, sin_ref, o_ref, h_scr, *scr,
                r, n_first_tiles):
    m = mod_ref[...]
    x = jnp.where(pl.program_id(0) < n_first_tiles, xa_ref[...], xb_ref[...])
    if r > 1:
        slab, xp = scr
        _permute_rows(x, slab, xp, r)
        x = xp[...]
    h_scr[...] = _norm_mod(x, g_ref[...], m[0:1], m[1:2]).astype(BF16)

    n = h_scr.shape[0] // r
    nw = 2 * LANES
    cos = cos_ref[...]
    sin = sin_ref[...]
    qscale = np.float32(HEAD_DIM ** -0.5)
    tables = ((cos * qscale, sin * qscale), (cos, sin), None)
    for sec, table in enumerate(tables):
        for cw in range(D // nw):
            col0 = sec * D + cw * nw
            acc = _dot(h_scr[...], w_ref[:, col0:col0 + nw])
            for c in range(nw // LANES):
                xc = acc[:, c * LANES:(c + 1) * LANES]
                if table is not None:
                    xc = xc * table[0] + pltpu.roll(xc, LANES // 2, 1) * table[1]
                cs = slice(col0 + c * LANES, col0 + (c + 1) * LANES)
                o_ref[:, :, cs] = xc.reshape(r, n, LANES).astype(BF16)


def _qkv_proj(xa, xb, mod, g, w_bf, cos_p, sin_p, groups, r):
    M = xa.shape[0] + xb.shape[0]
    tm = TM_PROJ
    n_first_tiles = xa.shape[0] // tm
    tb, tp = _tile_tables(groups, tm)
    scratch = [pltpu.VMEM((tm, D), BF16)]
    if r > 1:
        scratch += [pltpu.VMEM((D // LANES, tm, LANES), F32), pltpu.VMEM((tm, D), F32)]
    grid_spec = pltpu.PrefetchScalarGridSpec(
        num_scalar_prefetch=2,
        grid=(M // tm,),
        in_specs=_two_group_specs(tm, n_first_tiles) + [
            pl.BlockSpec((None, 6, D), lambda i, tb, tp: (tb[i], 0, 0)),
            pl.BlockSpec((1, D), lambda i, tb, tp: (0, 0)),
            pl.BlockSpec((D, 3 * D), lambda i, tb, tp: (0, 0), pipeline_mode=pl.Buffered(1)),
            pl.BlockSpec((tm, LANES), lambda i, tb, tp: (tp[i], 0)),
            pl.BlockSpec((tm, LANES), lambda i, tb, tp: (tp[i], 0)),
        ],
        out_specs=pl.BlockSpec((r, tm // r, 3 * D), lambda i, tb, tp: (0, i, 0)),
        scratch_shapes=scratch,
    )
    return pl.pallas_call(
        functools.partial(_qkv_kernel, r=r, n_first_tiles=n_first_tiles),
        grid_spec=grid_spec,
        out_shape=jax.ShapeDtypeStruct((r, M // r, 3 * D), BF16),
        compiler_params=_cparams(("parallel",)),
        name=f"qkv_rope_d{r}",
    )(tb, tp, xa, xb, mod, g, w_bf, cos_p, sin_p)


def _rope_tables(s_max):
    half = HEAD_DIM // 2
    inv = ROPE_THETA ** (-jnp.arange(0, HEAD_DIM, 2, dtype=F32) / HEAD_DIM)
    ang = jnp.arange(s_max, dtype=F32)[:, None] * inv[None, :]
    cos, sin = jnp.cos(ang), jnp.sin(ang)
    cos_t = jnp.tile(cos, (1, LANES // half))
    sin_t = jnp.concatenate([-sin, -sin, sin, sin], axis=1)
    return cos_t, sin_t


def _pair_heads(w):
    half = HEAD_DIM // 2
    perm = np.concatenate([np.arange(0, half), np.arange(HEAD_DIM, HEAD_DIM + half),
                           np.arange(half, HEAD_DIM), np.arange(HEAD_DIM + half, LANES)])
    k = w.shape[0]
    qk = w[:, :2 * D].reshape(k, 2 * D // LANES, LANES)[:, :, perm].reshape(k, 2 * D)
    return jnp.concatenate([qk, w[:, 2 * D:]], axis=1)


def _permute_table(t, tm, r):
    s = t.shape[0]
    return t.reshape(s // tm, tm // r, r, LANES).transpose(0, 2, 1, 3).reshape(s, LANES)


def _attn_kernel(q_ref, kp_ref, kc_ref, kn_ref, vp_ref, vc_ref, vn_ref, o_ref, l_ref,
                 kw, vw, s_scr, p_scr, *, seq_blocks):
    i = pl.program_id(1)
    (n1, nb1), (n2, nb2) = seq_blocks
    first_grp = i < n1 * nb1
    pos = jnp.where(first_grp, i % nb1, (i - n1 * nb1) % nb2)
    nb = jnp.where(first_grp, nb1, nb2)
    has_prev = pos > 0
    has_next = pos < nb - 1

    W = RADIUS
    kw[0:W] = kp_ref[...]
    kw[W:W + TQ] = kc_ref[...]
    kw[W + TQ:W + TQ + W] = kn_ref[...]
    vw[0:W] = vp_ref[...]
    vw[W:W + TQ] = vc_ref[...]
    vw[W + TQ:W + TQ + W] = vn_ref[...]

    lane = lax.broadcasted_iota(jnp.int32, (UQ, LANES), 1)
    low_half = lane < HEAD_DIM
    head_a = (lane % HEAD_DIM) < (HEAD_DIM // 2)
    nu = TQ // UQ
    for u in range(nu):
        for hp in range(N_HEADS // 2):
            cs = slice(hp * LANES, (hp + 1) * LANES)
            q2 = q_ref[u * UQ:(u + 1) * UQ, cs]
            kwin = kw[u * UQ:u * UQ + 2 * UQ, cs]
            zero = jnp.zeros_like(q2)
            s_scr[u, 2 * hp] = _dot_nt(jnp.where(head_a, q2, zero), kwin)
            s_scr[u, 2 * hp + 1] = _dot_nt(jnp.where(head_a, zero, q2), kwin)

    rr = lax.broadcasted_iota(jnp.int32, (UQ, 2 * UQ), 0)
    cc = lax.broadcasted_iota(jnp.int32, (UQ, 2 * UQ), 1)
    band = jnp.abs((cc - W) - rr) <= RADIUS
    l_ref[...] = jnp.zeros(l_ref.shape, F32)
    for u in range(nu):
        valid = band
        if u == 0:
            valid = valid & ((cc >= W) | has_prev)
        if u == nu - 1:
            valid = valid & ((cc < 2 * UQ - W) | has_next)
        s = jnp.where(valid[None], s_scr[u], NEG_BIG)
        m = jnp.max(s, axis=-1, keepdims=True)
        p = jnp.exp(s - m)
        den = jnp.sum(p, axis=-1, keepdims=True)
        p_scr[u] = p.astype(BF16)
        rden = 1.0 / den
        lse = m + jnp.log(den)
        rows = slice(u * UQ, (u + 1) * UQ)
        for hp in range(N_HEADS // 2):
            cs = slice(hp * LANES, (hp + 1) * LANES)
            vwin = vw[u * UQ:u * UQ + 2 * UQ, cs]
            oa = _dot(p_scr[u, 2 * hp], vwin) * rden[2 * hp]
            ob = _dot(p_scr[u, 2 * hp + 1], vwin) * rden[2 * hp + 1]
            o_ref[rows, cs] = jnp.where(low_half, oa, ob).astype(BF16)
        for h in range(N_HEADS):
            l_ref[rows, h:h + 1] = lse[h]


def _attention_group(qkv_g, groups):
    r, rows, _ = qkv_g.shape
    W = RADIUS
    for (B, S) in groups:
        assert S % (r * TQ) == 0
    seq_blocks = tuple((B, S // r // TQ) for (B, S) in groups)
    q64 = TQ // W
    last64 = rows // W - 1

    def big(col):
        return pl.BlockSpec((None, TQ, D), lambda res, i: (res, i, col))

    def prv(col):
        return pl.BlockSpec((None, W, D), lambda res, i: (res, jnp.maximum(i * q64 - 1, 0), col))

    def nxt(col):
        return pl.BlockSpec((None, W, D), lambda res, i: (res, jnp.minimum((i + 1) * q64, last64), col))

    return pl.pallas_call(
        functools.partial(_attn_kernel, seq_blocks=seq_blocks),
        grid=(r, rows // TQ),
        in_specs=[big(0), prv(1), big(1), nxt(1), prv(2), big(2), nxt(2)],
        out_specs=[pl.BlockSpec((None, TQ, D), lambda res, i: (res, i, 0)),
                   pl.BlockSpec((None, TQ, LANES), lambda res, i: (res, i, 0))],
        out_shape=[jax.ShapeDtypeStruct((r, rows, D), BF16),
                   jax.ShapeDtypeStruct((r, rows, LANES), F32)],
        scratch_shapes=[pltpu.VMEM((TQ + 2 * W, D), BF16), pltpu.VMEM((TQ + 2 * W, D), BF16),
                        pltpu.VMEM((TQ // UQ, N_HEADS, UQ, 2 * UQ), F32),
                        pltpu.VMEM((TQ // UQ, N_HEADS, UQ, 2 * UQ), BF16)],
        compiler_params=_cparams(("parallel", "parallel")),
        name=f"attn_d{r}",
    )(*([qkv_g] * 7))


def _router(x1, m, g_ffn, wr_t):
    h2 = _norm_mod(x1, g_ffn, m[3:4], m[4:5])
    logits_t = _dot3_nt(wr_t, h2)
    mx = jnp.max(logits_t, axis=0, keepdims=True)
    e = jnp.exp(logits_t - mx)
    return h2, e / jnp.sum(e, axis=0, keepdims=True)


def _store_token_tiles(ref, val):
    rows, cols = val.shape
    nsub = cols // LANES
    for sub in range(nsub):
        ref[pl.ds(sub, rows, stride=nsub), :] = val[:, sub * LANES:(sub + 1) * LANES]


def _oproj_kernel(tb_ref, o1_ref, o4_ref, o16_ref, l1_ref, l4_ref, l16_ref, xa_ref, xb_ref, mod_ref, w_ref,
                  gf_ref, wr_ref, x1_ref, h2_ref, aff_ref, oslab, lslab, om_scr, *, n_first_tiles):
    tm = xa_ref.shape[0]
    nc = D // LANES
    for g, (r, o_ref, l_ref) in enumerate(zip(DILATIONS, (o1_ref, o4_ref, o16_ref),
                                              (l1_ref, l4_ref, l16_ref))):
        n = tm // r
        for res in range(r):
            rows = slice(None) if r == 1 else pl.ds(res, n, stride=r)
            lslab[g, 0, rows, :] = l_ref[res]
            for c in range(nc):
                oslab[g, c, rows, :] = o_ref[res, :, c * LANES:(c + 1) * LANES].astype(F32)
    ls = [lslab[g, 0] for g in range(N_GROUPS)]
    mx = jnp.maximum(jnp.maximum(ls[0], ls[1]), ls[2])
    es = [jnp.exp(l - mx) for l in ls]
    tot = es[0] + es[1] + es[2]
    hrow = lax.broadcasted_iota(jnp.int32, (LANES, D), 0)
    hcol = lax.broadcasted_iota(jnp.int32, (LANES, D), 1)
    expand = jnp.where(hcol // HEAD_DIM == hrow, 1.0, 0.0).astype(BF16)
    wexp = []
    for g in range(N_GROUPS):
        wh, wl = _split2(es[g] / tot)
        wexp.append(_dot(wh, expand) + _dot(wl, expand))
    for c in range(nc):
        cs = slice(c * LANES, (c + 1) * LANES)
        om = wexp[0][:, cs] * oslab[0, c] + wexp[1][:, cs] * oslab[1, c] + wexp[2][:, cs] * oslab[2, c]
        om_scr[:, cs] = om.astype(BF16)
    m = mod_ref[...]
    y = _dot(om_scr[...], w_ref[...])
    x = jnp.where(pl.program_id(0) < n_first_tiles, xa_ref[...], xb_ref[...])
    x1 = x + m[2:3] * y
    x1_ref[...] = x1
    h2, aff_t = _router(x1, m, gf_ref[...], wr_ref[...])
    _store_token_tiles(h2_ref, h2)
    aff_ref[...] = aff_t


def _out_proj(os_, ls_, xa, xb, mod, w_bf, g_ffn, wr_t, groups):
    M = xa.shape[0] + xb.shape[0]
    tm = TM_PROJ
    n_first_tiles = xa.shape[0] // tm
    tb, _ = _tile_tables(groups, tm)
    o_specs = [pl.BlockSpec((r, tm // r, D), lambda i, tb: (0, i, 0)) for r in DILATIONS]
    l_specs = [pl.BlockSpec((r, tm // r, LANES), lambda i, tb: (0, i, 0)) for r in DILATIONS]
    grid_spec = pltpu.PrefetchScalarGridSpec(
        num_scalar_prefetch=1,
        grid=(M // tm,),
        in_specs=o_specs + l_specs + _two_group_specs(tm, n_first_tiles) + [
            pl.BlockSpec((None, 6, D), lambda i, tb: (tb[i], 0, 0)),
            pl.BlockSpec((D, D), lambda i, tb: (0, 0)),
            pl.BlockSpec((1, D), lambda i, tb: (0, 0)),
            pl.BlockSpec((N_EXPERTS, D), lambda i, tb: (0, 0)),
        ],
        out_specs=[
            pl.BlockSpec((tm, D), lambda i, tb: (i, 0)),
            pl.BlockSpec((tm * (D // LANES), LANES), lambda i, tb: (i, 0)),
            pl.BlockSpec((N_EXPERTS, tm), lambda i, tb: (0, i)),
        ],
        scratch_shapes=[pltpu.VMEM((N_GROUPS, D // LANES, tm, LANES), F32),
                        pltpu.VMEM((N_GROUPS, 1, tm, LANES), F32),
                        pltpu.VMEM((tm, D), BF16)],
    )
    return pl.pallas_call(
        functools.partial(_oproj_kernel, n_first_tiles=n_first_tiles),
        grid_spec=grid_spec,
        out_shape=[jax.ShapeDtypeStruct((M, D), F32), jax.ShapeDtypeStruct((M * (D // LANES), LANES), F32),
                   jax.ShapeDtypeStruct((N_EXPERTS, M), F32)],
        compiler_params=_cparams(("parallel",)),
        name="attn_out_router",
    )(tb, *os_, *ls_, xa, xb, mod, w_bf, g_ffn, wr_t)


def _sgu_kernel(tb_ref, x_ref, mod_ref, g_ref, win_ref, lng_ref, lnb_ref, ws_ref, bs_ref, wout_ref,
                gf_ref, wr_ref, x1_ref, h2_ref, aff_ref, h_scr, v_scr, g_scr):
    m = mod_ref[...]
    x = x_ref[...]
    h_scr[...] = _norm_mod(x, g_ref[...], m[0:1], m[1:2]).astype(BF16)
    tm = x.shape[0]
    nc = 512
    s1 = jnp.zeros((tm, 1), F32)
    s2 = jnp.zeros((tm, 1), F32)
    for c in range(SGU_HALF // nc):
        z = _gelu_tanh(_dot(h_scr[...], win_ref[:, SGU_HALF + c * nc:SGU_HALF + (c + 1) * nc]))
        v_scr[:, c * nc:(c + 1) * nc] = z
        s1 = s1 + jnp.sum(z, axis=-1, keepdims=True)
    mu = s1 * np.float32(1.0 / SGU_HALF)
    for c in range(SGU_HALF // nc):
        d = v_scr[:, c * nc:(c + 1) * nc] - mu
        s2 = s2 + jnp.sum(d * d, axis=-1, keepdims=True)
    rstd = lax.rsqrt(s2 * np.float32(1.0 / SGU_HALF) + NORM_EPS)
    for c in range(SGU_HALF // nc):
        cs = slice(c * nc, (c + 1) * nc)
        v_scr[:, cs] = (v_scr[:, cs] - mu) * rstd * lng_ref[:, cs] + lnb_ref[:, cs]
    for hd in range(SGU_HEADS):
        cs = slice(hd * SGU_HEAD_CH, (hd + 1) * SGU_HEAD_CH)
        u = _gelu_tanh(_dot(h_scr[...], win_ref[:, cs]))
        for k in range(tm // CHUNK):
            rs = slice(k * CHUNK, (k + 1) * CHUNK)
            sv = _dot(ws_ref[hd], v_scr[rs, cs].astype(BF16)) + bs_ref[hd]
            g_scr[rs, cs] = (u[rs] * sv).astype(BF16)
    y = _dot(g_scr[...], wout_ref[...])
    x1 = x + m[2:3] * y
    x1_ref[...] = x1
    h2, aff_t = _router(x1, m, gf_ref[...], wr_ref[...])
    _store_token_tiles(h2_ref, h2)
    aff_ref[...] = aff_t


def _sgu_layer(x, mod, g_mix, w_in_bf, ln_g, ln_b, w_s_bf, b_s_col, w_out_bf, g_ffn, wr_t, groups):
    M = x.shape[0]
    tm = TM_SGU
    tb, _ = _tile_tables(groups, tm)
    const = lambda *shape: pl.BlockSpec(shape, lambda i, tb: (0,) * len(shape),
                                        pipeline_mode=pl.Buffered(1))
    grid_spec = pltpu.PrefetchScalarGridSpec(
        num_scalar_prefetch=1,
        grid=(M // tm,),
        in_specs=[
            pl.BlockSpec((tm, D), lambda i, tb: (i, 0)),
            pl.BlockSpec((None, 6, D), lambda i, tb: (tb[i], 0, 0)),
            const(1, D),
            const(D, 2 * SGU_HALF),
            const(1, SGU_HALF),
            const(1, SGU_HALF),
            const(SGU_HEADS, CHUNK, CHUNK),
            const(SGU_HEADS, CHUNK, 1),
            const(SGU_HALF, D),
            const(1, D),
            const(N_EXPERTS, D),
        ],
        out_specs=[
            pl.BlockSpec((tm, D), lambda i, tb: (i, 0)),
            pl.BlockSpec((tm * (D // LANES), LANES), lambda i, tb: (i, 0)),
            pl.BlockSpec((N_EXPERTS, tm), lambda i, tb: (0, i)),
        ],
        scratch_shapes=[pltpu.VMEM((tm, D), BF16), pltpu.VMEM((tm, SGU_HALF), F32),
                        pltpu.VMEM((tm, SGU_HALF), BF16)],
    )
    return pl.pallas_call(
        _sgu_kernel,
        grid_spec=grid_spec,
        out_shape=[jax.ShapeDtypeStruct((M, D), F32), jax.ShapeDtypeStruct((M * (D // LANES), LANES), F32),
                   jax.ShapeDtypeStruct((N_EXPERTS, M), F32)],
        compiler_params=_cparams(("parallel",)),
        name="sgu_router",
    )(tb, x, mod, g_mix, w_in_bf, ln_g, ln_b, w_s_bf, b_s_col, w_out_bf, g_ffn, wr_t)


def _topk_kernel(aff_ref, idx_ref, gate_ref, pos_ref, bst_ref, incl_scr, tot_scr, bex_scr, *, nb, cap,
                 tok_base, slot_base):
    E = N_EXPERTS
    a = aff_ref[...]
    bits = pltpu.bitcast(a, jnp.int32)

    def count(mask):
        c = jnp.sum(jnp.where(mask, 1.0, 0.0), axis=1, keepdims=True)
        return jnp.sum(c, axis=2, keepdims=True)

    def search(k, thr):
        cand = thr | jnp.left_shift(jnp.int32(1), 30 - k)
        return jnp.where(count(bits >= cand) >= cap, cand, thr)

    thr = lax.fori_loop(0, 31, search, jnp.zeros((E, 1, 1), jnp.int32))
    gt = bits > thr
    eq = bits == thr
    need = cap - count(gt)

    row = lax.broadcasted_iota(jnp.int32, (LANES, LANES), 0)
    col = lax.broadcasted_iota(jnp.int32, (LANES, LANES), 1)
    upper = jnp.where(row <= col, 1.0, 0.0).astype(BF16)
    ones = jnp.ones((LANES, LANES), BF16)
    brow = lax.broadcasted_iota(jnp.int32, (nb, nb), 0)
    bcol = lax.broadcasted_iota(jnp.int32, (nb, nb), 1)
    lower_strict = jnp.where(bcol < brow, 1.0, 0.0).astype(BF16)

    def prefix(mask):
        xb = jnp.where(mask, 1.0, 0.0).astype(BF16).reshape(E * nb, LANES)
        incl_scr[...] = _dot(xb, upper).reshape(E, nb, LANES)
        tot_scr[...] = _dot(xb, ones).reshape(E, nb, LANES)
        for e in range(E):
            bex_scr[e] = _dot(lower_strict, tot_scr[e].astype(BF16))

    prefix(eq)
    rank_eq = bex_scr[...] + incl_scr[...] - jnp.where(eq, 1.0, 0.0)
    sel = gt | (eq & (rank_eq < need))
    prefix(sel)
    self32 = jnp.where(sel, 1.0, 0.0)
    pos = bex_scr[...] + incl_scr[...] - self32
    pos_ref[...] = jnp.where(sel, pos + np.float32(slot_base), -1.0)
    bst_ref[...] = bex_scr[...] + np.float32(slot_base)

    sc = SLOT_CHUNK
    bidx = lax.broadcasted_iota(jnp.int32, (nb, sc), 0).astype(F32)
    jidx = lax.broadcasted_iota(jnp.int32, (LANES, sc), 0).astype(F32)
    for e in range(E):
        tot_e = tot_scr[e][:, 0:1]
        binc_e = bex_scr[e][:, 0:1] + tot_e
        incl_t = jnp.transpose(incl_scr[e]).astype(BF16)
        aff_parts = _split3(jnp.transpose(a[e]))
        for c in range(cap // sc):
            s = (lax.broadcasted_iota(jnp.int32, (nb, sc), 1) + c * sc).astype(F32)
            before = binc_e <= s
            blk = jnp.sum(jnp.where(before, 1.0, 0.0), axis=0, keepdims=True)
            base = jnp.sum(jnp.where(before, tot_e, 0.0), axis=0, keepdims=True)
            local = s[0:1] - base
            onehot = jnp.where(bidx == blk, 1.0, 0.0).astype(BF16)
            rows = _dot(incl_t, onehot)
            cnt = jnp.sum(jnp.where(rows <= local, 1.0, 0.0), axis=0, keepdims=True)
            tok = blk * np.float32(LANES) + cnt + np.float32(tok_base)
            idx_ref[e:e + 1, c * sc:(c + 1) * sc] = tok.astype(jnp.int32)
            arow = (_dot(aff_parts[0], onehot) + _dot(aff_parts[1], onehot)) + _dot(aff_parts[2], onehot)
            gate = jnp.sum(jnp.where(jidx == cnt, arow, 0.0), axis=0, keepdims=True)
            gate_col = jnp.transpose(jnp.broadcast_to(gate, (8, sc)))[:, 0:1]
            gate_ref[e, c * sc:(c + 1) * sc, :] = jnp.broadcast_to(gate_col, (sc, LANES))


def _topk_group(aff_t, tok_base, n, slot_base):
    E = N_EXPERTS
    M = aff_t.shape[1]
    nb = n // LANES
    cap = CAPACITY_FACTOR * n // E
    assert n % LANES == 0 and nb % 8 == 0 and cap % SLOT_CHUNK == 0 and tok_base % n == 0
    a3 = aff_t.reshape(E, M // LANES, LANES)
    blk = tok_base // n
    return pl.pallas_call(
        functools.partial(_topk_kernel, nb=nb, cap=cap, tok_base=tok_base, slot_base=slot_base),
        grid=(1,),
        in_specs=[pl.BlockSpec((E, nb, LANES), lambda i: (0, blk, 0))],
        out_specs=[pl.BlockSpec((E, cap), lambda i: (0, 0)),
                   pl.BlockSpec((E, cap, LANES), lambda i: (0, 0, 0)),
                   pl.BlockSpec((E, nb, LANES), lambda i: (0, 0, 0)),
                   pl.BlockSpec((E, nb, LANES), lambda i: (0, 0, 0))],
        out_shape=[jax.ShapeDtypeStruct((E, cap), jnp.int32),
                   jax.ShapeDtypeStruct((E, cap, LANES), F32),
                   jax.ShapeDtypeStruct((E, nb, LANES), F32),
                   jax.ShapeDtypeStruct((E, nb, LANES), F32)],
        scratch_shapes=[pltpu.VMEM((E, nb, LANES), F32)] * 3,
        compiler_params=_cparams(("arbitrary",)),
        name=f"expert_choice_topk_{n}",
    )(a3)


def _ffn_kernel(idx_ref, h_hbm, gate_ref, wg_hbm, wu_hbm, wd_hbm, ye_ref, xbuf, x_scr, wg_buf, wu_buf, wd_buf,
                sem, wsem, *, layer, tiles_per_expert, n_steps):
    e = pl.program_id(0)
    j = pl.program_id(1)
    step = e * tiles_per_expert + j
    slot = step % 2
    wslot = e % 2

    def weight_copies(expert, ws):
        return (pltpu.make_async_copy(wg_hbm.at[layer, expert], wg_buf.at[ws], wsem.at[ws, 0]),
                pltpu.make_async_copy(wu_hbm.at[layer, expert], wu_buf.at[ws], wsem.at[ws, 1]),
                pltpu.make_async_copy(wd_hbm.at[layer, expert], wd_buf.at[ws], wsem.at[ws, 2]))

    def row_copy(tile, k, slot_):
        t = pl.multiple_of(idx_ref[tile * FFN_TILE + k] * NSUB, NSUB)
        return pltpu.make_async_copy(h_hbm.at[pl.ds(t, NSUB), :], xbuf.at[slot_, pl.ds(k * NSUB, NSUB), :],
                                     sem.at[slot_])

    @pl.when(step == 0)
    def _():
        for cp in weight_copies(0, 0):
            cp.start()

        def issue(k, carry):
            row_copy(0, k, 0).start()
            return carry
        lax.fori_loop(0, FFN_TILE, issue, 0, unroll=8)

    @pl.when(j == 0)
    def _():
        for cp in weight_copies(e, wslot):
            cp.wait()

        @pl.when(e + 1 < pl.num_programs(0))
        def _():
            for cp in weight_copies(e + 1, 1 - wslot):
                cp.start()

    for k in range(FFN_TILE):
        row_copy(step, k, slot).wait()
    nxt = jnp.minimum(step + 1, n_steps - 1)

    def compute(ws):
        for sub in range(NSUB):
            x_scr[:, sub * LANES:(sub + 1) * LANES] = xbuf[slot, pl.ds(sub, FFN_TILE, stride=NSUB), :].astype(BF16)
        for k in range(FFN_TILE):
            row_copy(nxt, k, 1 - slot).start()
        acc = jnp.zeros((FFN_TILE, D), F32)
        for c in range(D_EXPERT // F_CHUNK):
            cs = slice(c * F_CHUNK, (c + 1) * F_CHUNK)
            hg = _dot(x_scr[...], wg_buf[ws, :, cs])
            hu = _dot(x_scr[...], wu_buf[ws, :, cs])
            act = (hg * _sigmoid(hg) * hu).astype(BF16)
            acc = acc + _dot(act, wd_buf[ws, cs, :])
        gate = gate_ref[...]
        for c in range(D // LANES):
            cs = slice(c * LANES, (c + 1) * LANES)
            ye_ref[:, cs] = (acc[:, cs] * gate).astype(BF16)

    for ws in range(2):
        @pl.when(wslot == ws)
        def _():
            compute(ws)

    @pl.when(step == n_steps - 1)
    def _():
        for k in range(FFN_TILE):
            row_copy(nxt, k, 1 - slot).wait()


def _expert_ffn(idx_flat, h2, gate_all, wg_bf, wu_bf, wd_bf, layer, slots):
    E = N_EXPERTS
    tiles = slots // FFN_TILE
    grid_spec = pltpu.PrefetchScalarGridSpec(
        num_scalar_prefetch=1,
        grid=(E, tiles),
        in_specs=[
            pl.BlockSpec(memory_space=pl.ANY),
            pl.BlockSpec((FFN_TILE, LANES), lambda e, j, idx: (e * tiles + j, 0)),
            pl.BlockSpec(memory_space=pl.ANY),
            pl.BlockSpec(memory_space=pl.ANY),
            pl.BlockSpec(memory_space=pl.ANY),
        ],
        out_specs=pl.BlockSpec((None, FFN_TILE, D), lambda e, j, idx: (e, j, 0)),
        scratch_shapes=[pltpu.VMEM((2, FFN_TILE * NSUB, LANES), F32), pltpu.VMEM((FFN_TILE, D), BF16),
                        pltpu.VMEM((2, D, D_EXPERT), BF16), pltpu.VMEM((2, D, D_EXPERT), BF16),
                        pltpu.VMEM((2, D_EXPERT, D), BF16),
                        pltpu.SemaphoreType.DMA((2,)), pltpu.SemaphoreType.DMA((2, 3))],
    )
    return pl.pallas_call(
        functools.partial(_ffn_kernel, layer=layer, tiles_per_expert=tiles, n_steps=E * tiles),
        grid_spec=grid_spec,
        out_shape=jax.ShapeDtypeStruct((E, slots, D), BF16),
        compiler_params=_cparams(("arbitrary", "arbitrary")),
        name="expert_ffn",
    )(idx_flat, h2, gate_all.reshape(E * slots, LANES), wg_bf, wu_bf, wd_bf)


def _combine_kernel(tb_ref, bst_ref, x_ref, pos_ref, mod_ref, gfin_ref, ye_hbm, *rest, slots, blocks,
                    n_tiles, final_norm, split):
    if split is None:
        (o_ref, win, winx, acc, sem, semx) = rest
    else:
        (o_ref, o2_ref, win, winx, acc, sem, semx) = rest
    E = N_EXPERTS
    i = pl.program_id(0)
    bpt = TM_COMB // LANES
    stride = blocks + 1

    def tile_start(tile, e):
        return (bst_ref[e * stride + tile * bpt] // 16) * 16

    def window_copy(e, lo, buf, sem_):
        w0 = pl.multiple_of(jnp.minimum(lo, slots - WIN), 16)
        return pltpu.make_async_copy(ye_hbm.at[e, pl.ds(w0, WIN), :], buf.at[pl.ds(e * WIN, WIN), :],
                                     sem_.at[e])

    slot = i % 2

    @pl.when(i == 0)
    def _():
        for e in range(E):
            window_copy(e, tile_start(0, e), win.at[0], sem.at[0]).start()

    starts = [tile_start(i, e) for e in range(E)]
    npass = jnp.int32(1)
    for e in range(E):
        end = bst_ref[e * stride + (i + 1) * bpt]
        npass = jnp.maximum(npass, (end - starts[e] + (WIN - 1)) // WIN)

    for e in range(E):
        window_copy(e, starts[e], win.at[slot], sem.at[slot]).wait()
    nxt = jnp.minimum(i + 1, n_tiles - 1)
    for e in range(E):
        window_copy(e, tile_start(nxt, e), win.at[1 - slot], sem.at[1 - slot]).start()

    lane_e = lax.broadcasted_iota(jnp.int32, (1, E), 1)
    erow = lax.broadcasted_iota(jnp.int32, (E, E * WIN), 0)
    ecol = lax.broadcasted_iota(jnp.int32, (E, E * WIN), 1)
    expand = jnp.where(ecol // WIN == erow, 1.0, 0.0).astype(BF16)
    sprime = (lax.broadcasted_iota(jnp.int32, (1, E * WIN), 1) % WIN).astype(F32)
    pos = pos_ref[...]

    def scatter_rows(p, wbuf):
        lo_vec = jnp.zeros((1, E), F32)
        w0_vec = jnp.zeros((1, E), F32)
        for e in range(E):
            lo = starts[e] + p * WIN
            w0 = jnp.minimum(lo, slots - WIN)
            lo_vec = jnp.where(lane_e == e, lo.astype(F32), lo_vec)
            w0_vec = jnp.where(lane_e == e, w0.astype(F32), w0_vec)
        rel = jnp.where(pos >= lo_vec, pos - w0_vec, -1.0)
        rel = jnp.clip(rel, -1.0, np.float32(WIN))
        relx = _dot(rel.astype(BF16), expand)
        onehot = jnp.where(relx == sprime, 1.0, 0.0).astype(BF16)
        return _dot(onehot, wbuf[...])

    acc[...] = scatter_rows(0, win.at[slot])

    def extra_pass(p, carry):
        copies = [window_copy(e, starts[e] + p * WIN, winx, semx) for e in range(E)]
        for cp in copies:
            cp.start()
        for cp in copies:
            cp.wait()
        acc[...] += scatter_rows(p, winx)
        return carry

    lax.fori_loop(1, npass, extra_pass, 0)

    m = mod_ref[...]
    y = x_ref[...] + m[5:6] * acc[...]
    if final_norm:
        ms = jnp.mean(y * y, axis=-1, keepdims=True)
        y = y * lax.rsqrt(ms + NORM_EPS) * gfin_ref[...]
    if split is None:
        o_ref[...] = y
    else:
        @pl.when(i < split)
        def _():
            o_ref[...] = y

        @pl.when(i >= split)
        def _():
            o2_ref[...] = y

    @pl.when(i == n_tiles - 1)
    def _():
        for e in range(E):
            window_copy(e, tile_start(nxt, e), win.at[1 - slot], sem.at[1 - slot]).wait()


def _combine(x1, pos_t, mod, g_final, ye, bst_flat, groups, final_norm):
    M = x1.shape[0]
    tm = TM_COMB
    E = N_EXPERTS
    slots = ye.shape[1]
    tb, _ = _tile_tables(groups, tm)
    n_tiles = M // tm
    n1 = groups[0][0] * groups[0][1]
    if final_norm:
        split = n1 // tm
        out_specs = [pl.BlockSpec((tm, D), lambda i, tb, bst: (jnp.minimum(i, split - 1), 0)),
                     pl.BlockSpec((tm, D), lambda i, tb, bst: (jnp.maximum(i - split, 0), 0))]
        out_shape = [jax.ShapeDtypeStruct((n1, D), F32), jax.ShapeDtypeStruct((M - n1, D), F32)]
    else:
        split = None
        out_specs = pl.BlockSpec((tm, D), lambda i, tb, bst: (i, 0))
        out_shape = jax.ShapeDtypeStruct((M, D), F32)
    grid_spec = pltpu.PrefetchScalarGridSpec(
        num_scalar_prefetch=2,
        grid=(n_tiles,),
        in_specs=[
            pl.BlockSpec((tm, D), lambda i, tb, bst: (i, 0)),
            pl.BlockSpec((tm, E), lambda i, tb, bst: (i, 0)),
            pl.BlockSpec((None, 6, D), lambda i, tb, bst: (tb[i], 0, 0)),
            pl.BlockSpec((1, D), lambda i, tb, bst: (0, 0)),
            pl.BlockSpec(memory_space=pl.ANY),
        ],
        out_specs=out_specs,
        scratch_shapes=[pltpu.VMEM((2, E * WIN, D), BF16), pltpu.VMEM((E * WIN, D), BF16),
                        pltpu.VMEM((tm, D), F32), pltpu.SemaphoreType.DMA((2, E)),
                        pltpu.SemaphoreType.DMA((E,))],
    )
    return pl.pallas_call(
        functools.partial(_combine_kernel, slots=slots, blocks=M // LANES, n_tiles=n_tiles,
                          final_norm=final_norm, split=split),
        grid_spec=grid_spec,
        out_shape=out_shape,
        compiler_params=_cparams(("arbitrary",)),
        name="moe_combine",
    )(tb, bst_flat, x1, pos_t, mod, g_final, ye)


def _moe(x1, h2, aff_t, mod, g_final, wg_bf, wu_bf, wd_bf, layer, groups, final_norm):
    E = N_EXPERTS
    idxs, gates, poss, bsts = [], [], [], []
    tok_base = 0
    slot_base = 0
    for (B, S) in groups:
        n = B * S
        idx, gate, pos, bst = _topk_group(aff_t, tok_base, n, slot_base)
        idxs.append(idx)
        gates.append(gate)
        poss.append(pos.reshape(E, n))
        bsts.append(bst[:, :, 0])
        tok_base += n
        slot_base += CAPACITY_FACTOR * n // E
    slots = slot_base
    idx_flat = jnp.concatenate(idxs, axis=1).reshape(-1)
    ye = _expert_ffn(idx_flat, h2, jnp.concatenate(gates, axis=1), wg_bf, wu_bf, wd_bf, layer, slots)
    pos_t = jnp.concatenate(poss, axis=1).T
    bst = jnp.concatenate(bsts + [jnp.full((E, 1), slots, F32)], axis=1)
    bst_flat = bst.astype(jnp.int32).reshape(-1)
    return _combine(x1, pos_t, mod, g_final, ye, bst_flat, groups, final_norm)


def kernel(x_prompt, x_sample, c_prompt, c_sample, w_mod, b_mod, g_mix, g_ffn, a_w_in, a_w_out, b_w_in,
           b_ln_g, b_ln_b, b_w_s, b_b_s, b_w_out, moe_w_router, moe_w_gate, moe_w_up, moe_w_down, g_final):
    groups = (x_prompt.shape[:2], x_sample.shape[:2])
    assert x_prompt.shape[2] == D and x_sample.shape[2] == D
    xa, xb = x_prompt.reshape(-1, D), x_sample.reshape(-1, D)
    n_seq = groups[0][0] + groups[1][0]
    rows = -(-n_seq // 8) * 8
    c_all = jnp.concatenate([c_prompt, c_sample, jnp.zeros((rows - n_seq, D), F32)], axis=0)
    mod = _modulation(c_all, w_mod, b_mod).reshape(w_mod.shape[0], rows, 6, D)
    cos_t, sin_t = _rope_tables(max(groups[0][1], groups[1][1]))
    wr_t = jnp.swapaxes(moe_w_router, 1, 2)
    gfin = g_final.reshape(1, D)

    os_, ls_ = [], []
    for g, r in enumerate(DILATIONS):
        w_g = _pair_heads(a_w_in[0][:, g * 3 * D:(g + 1) * 3 * D]).astype(BF16)
        qkv_g = _qkv_proj(xa, xb, mod[0], g_mix[0].reshape(1, D), w_g, _permute_table(cos_t, TM_PROJ, r),
                          _permute_table(sin_t, TM_PROJ, r), groups, r)
        o_g, l_g = _attention_group(qkv_g, groups)
        os_.append(o_g)
        ls_.append(l_g)
    x1, h2, aff_t = _out_proj(os_, ls_, xa, xb, mod[0], a_w_out[0].astype(BF16), g_ffn[0].reshape(1, D),
                              wr_t[0], groups)
    wg_bf, wu_bf, wd_bf = moe_w_gate.astype(BF16), moe_w_up.astype(BF16), moe_w_down.astype(BF16)
    x = _moe(x1, h2, aff_t, mod[0], gfin, wg_bf, wu_bf, wd_bf, 0, groups, final_norm=False)

    x1, h2, aff_t = _sgu_layer(x, mod[1], g_mix[1].reshape(1, D), b_w_in[0].astype(BF16),
                               b_ln_g[0].reshape(1, SGU_HALF), b_ln_b[0].reshape(1, SGU_HALF),
                               b_w_s[0].astype(BF16), b_b_s[0].reshape(SGU_HEADS, CHUNK, 1),
                               b_w_out[0].astype(BF16), g_ffn[1].reshape(1, D), wr_t[1], groups)
    y1, y2 = _moe(x1, h2, aff_t, mod[1], gfin, wg_bf, wu_bf, wd_bf, 1, groups, final_norm=True)

    return (y1.reshape(x_prompt.shape), y2.reshape(x_sample.shape))
```

```python
import functools

import numpy as np
import jax
import jax.numpy as jnp
from jax import lax
from jax.experimental import pallas as pl
from jax.experimental.pallas import tpu as pltpu

F32 = jnp.float32
BF16 = jnp.bfloat16

D = 1024
N_HEADS = 16
HEAD_DIM = 64
DILATIONS = (1, 4, 16)
WINDOWS = (128, 512, 2048)
RADIUS = 64
N_GROUPS = 3
QKV_COLS = N_GROUPS * 3 * D
ROPE_THETA = 10000.0
CHUNK = 128
SGU_HALF = 3 * D
SGU_HEADS = 16
SGU_HEAD_CH = SGU_HALF // SGU_HEADS
N_EXPERTS = 16
CAPACITY_FACTOR = 2
D_EXPERT = 2 * D
NORM_EPS = 1e-6
NEG_BIG = -1e30

LANES = 128
VMEM_LIMIT = 56 * 1024 * 1024

TM_PROJ = 512
TM_SGU = 256
TM_COMB = 256
TQ = 256
UQ = 128
FFN_TILE = 256
F_CHUNK = 512
WIN = 64
SLOT_CHUNK = 512
NSUB = D // LANES


def _cparams(sem):
    return pltpu.CompilerParams(dimension_semantics=sem, vmem_limit_bytes=VMEM_LIMIT)


def _split2(x):
    hi = x.astype(BF16)
    lo = (x - hi.astype(F32)).astype(BF16)
    return hi, lo


def _split3(x):
    hi = x.astype(BF16)
    r1 = x - hi.astype(F32)
    mid = r1.astype(BF16)
    lo = (r1 - mid.astype(F32)).astype(BF16)
    return hi, mid, lo


def _dot(a, b):
    return jnp.dot(a, b, preferred_element_type=F32)


def _dot_nt(a, b):
    return lax.dot_general(a, b, (((1,), (1,)), ((), ())), preferred_element_type=F32)


def _dot3(a, b):
    ah, al = _split2(a)
    bh, bl = _split2(b)
    return _dot(ah, bh) + (_dot(ah, bl) + _dot(al, bh))


def _dot3_nt(a, b):
    ah, al = _split2(a)
    bh, bl = _split2(b)
    return _dot_nt(ah, bh) + (_dot_nt(ah, bl) + _dot_nt(al, bh))


def _sigmoid(x):
    return 1.0 / (1.0 + jnp.exp(-x))


def _gelu_tanh(x):
    c = np.float32(np.sqrt(2.0 / np.pi))
    return 0.5 * x * (1.0 + jnp.tanh(c * (x + np.float32(0.044715) * (x * x * x))))


def _norm_mod(x, g, shift, scale):
    ms = jnp.mean(x * x, axis=-1, keepdims=True)
    return (x * lax.rsqrt(ms + NORM_EPS) * g) * (1.0 + scale) + shift


def _tile_tables(groups, tm):
    tb, tp = [], []
    b0 = 0
    for (B, S) in groups:
        assert S % tm == 0
        for b in range(B):
            for k in range(S // tm):
                tb.append(b0 + b)
                tp.append(k)
        b0 += B
    return jnp.asarray(np.array(tb, np.int32)), jnp.asarray(np.array(tp, np.int32))


def _mod_kernel(c_ref, w_ref, b_ref, o_ref):
    c = c_ref[...]
    o_ref[...] = _dot3(c * _sigmoid(c), w_ref[...]) + b_ref[...]


def _modulation(c_pad, w_mod, b_mod):
    depth, _, n = w_mod.shape
    tn = 2048
    rows = c_pad.shape[0]
    return pl.pallas_call(
        _mod_kernel,
        grid=(depth, n // tn),
        in_specs=[
            pl.BlockSpec((rows, D), lambda l, j: (0, 0)),
            pl.BlockSpec((None, D, tn), lambda l, j: (l, 0, j)),
            pl.BlockSpec((None, 1, tn), lambda l, j: (l, 0, j)),
        ],
        out_specs=pl.BlockSpec((None, rows, tn), lambda l, j: (l, 0, j)),
        out_shape=jax.ShapeDtypeStruct((depth, rows, n), F32),
        compiler_params=_cparams(("parallel", "parallel")),
        name="modulation",
    )(c_pad, w_mod, b_mod.reshape(depth, 1, n))


def _permute_rows(src, slab, dst, r):
    tm, cols = src.shape
    n = tm // r
    for c in range(cols // LANES):
        slab[c] = src[:, c * LANES:(c + 1) * LANES]
    for res in range(r):
        for c in range(cols // LANES):
            dst[res * n:(res + 1) * n, c * LANES:(c + 1) * LANES] = slab[c, pl.ds(res, n, stride=r), :]


def _two_group_specs(tm, n_first_tiles):
    first = pl.BlockSpec((tm, D), lambda i, *_: (jnp.minimum(i, n_first_tiles - 1), 0))
    second = pl.BlockSpec((tm, D), lambda i, *_: (jnp.maximum(i - n_first_tiles, 0), 0))
    return [first, second]


def _qkv_kernel(tb_ref, tp_ref, xa_ref, xb_ref, mod_ref, g_ref, w_ref, cos_ref, sin_ref, o_ref, h_scr, *scr,
                r, n_first_tiles):
    m = mod_ref[...]
    x = jnp.where(pl.program_id(0) < n_first_tiles, xa_ref[...], xb_ref[...])
    if r > 1:
        slab, xp = scr
        _permute_rows(x, slab, xp, r)
        x = xp[...]
    h_scr[...] = _norm_mod(x, g_ref[...], m[0:1], m[1:2]).astype(BF16)

    n = h_scr.shape[0] // r
    nw = 2 * LANES
    cos = cos_ref[...]
    sin = sin_ref[...]
    qscale = np.float32(HEAD_DIM ** -0.5)
    tables = ((cos * qscale, sin * qscale), (cos, sin), None)
    for sec, table in enumerate(tables):
        for cw in range(D // nw):
            col0 = sec * D + cw * nw
            acc = _dot(h_scr[...], w_ref[:, col0:col0 + nw])
            for c in range(nw // LANES):
                xc = acc[:, c * LANES:(c + 1) * LANES]
                if table is not None:
                    xc = xc * table[0] + pltpu.roll(xc, LANES // 2, 1) * table[1]
                cs = slice(col0 + c * LANES, col0 + (c + 1) * LANES)
                o_ref[:, :, cs] = xc.reshape(r, n, LANES).astype(BF16)


def _qkv_proj(xa, xb, mod, g, w_bf, cos_p, sin_p, groups, r):
    M = xa.shape[0] + xb.shape[0]
    tm = TM_PROJ
    n_first_tiles = xa.shape[0] // tm
    tb, tp = _tile_tables(groups, tm)
    scratch = [pltpu.VMEM((tm, D), BF16)]
    if r > 1:
        scratch += [pltpu.VMEM((D // LANES, tm, LANES), F32), pltpu.VMEM((tm, D), F32)]
    grid_spec = pltpu.PrefetchScalarGridSpec(
        num_scalar_prefetch=2,
        grid=(M // tm,),
        in_specs=_two_group_specs(tm, n_first_tiles) + [
            pl.BlockSpec((None, 6, D), lambda i, tb, tp: (tb[i], 0, 0)),
            pl.BlockSpec((1, D), lambda i, tb, tp: (0, 0)),
            pl.BlockSpec((D, 3 * D), lambda i, tb, tp: (0, 0), pipeline_mode=pl.Buffered(1)),
            pl.BlockSpec((tm, LANES), lambda i, tb, tp: (tp[i], 0)),
            pl.BlockSpec((tm, LANES), lambda i, tb, tp: (tp[i], 0)),
        ],
        out_specs=pl.BlockSpec((r, tm // r, 3 * D), lambda i, tb, tp: (0, i, 0)),
        scratch_shapes=scratch,
    )
    return pl.pallas_call(
        functools.partial(_qkv_kernel, r=r, n_first_tiles=n_first_tiles),
        grid_spec=grid_spec,
        out_shape=jax.ShapeDtypeStruct((r, M // r, 3 * D), BF16),
        compiler_params=_cparams(("parallel",)),
        name=f"qkv_rope_d{r}",
    )(tb, tp, xa, xb, mod, g, w_bf, cos_p, sin_p)


def _rope_tables(s_max):
    half = HEAD_DIM // 2
    inv = ROPE_THETA ** (-jnp.arange(0, HEAD_DIM, 2, dtype=F32) / HEAD_DIM)
    ang = jnp.arange(s_max, dtype=F32)[:, None] * inv[None, :]
    cos, sin = jnp.cos(ang), jnp.sin(ang)
    cos_t = jnp.tile(cos, (1, LANES // half))
    sin_t = jnp.concatenate([-sin, -sin, sin, sin], axis=1)
    return cos_t, sin_t


def _pair_heads(w):
    half = HEAD_DIM // 2
    perm = np.concatenate([np.arange(0, half), np.arange(HEAD_DIM, HEAD_DIM + half),
                           np.arange(half, HEAD_DIM), np.arange(HEAD_DIM + half, LANES)])
    k = w.shape[0]
    qk = w[:, :2 * D].reshape(k, 2 * D // LANES, LANES)[:, :, perm].reshape(k, 2 * D)
    return jnp.concatenate([qk, w[:, 2 * D:]], axis=1)


def _permute_table(t, tm, r):
    s = t.shape[0]
    return t.reshape(s // tm, tm // r, r, LANES).transpose(0, 2, 1, 3).reshape(s, LANES)


def _attn_kernel(q_ref, kp_ref, kc_ref, kn_ref, vp_ref, vc_ref, vn_ref, o_ref, l_ref,
                 kw, vw, s_scr, p_scr, *, seq_blocks):
    i = pl.program_id(1)
    (n1, nb1), (n2, nb2) = seq_blocks
    first_grp = i < n1 * nb1
    pos = jnp.where(first_grp, i % nb1, (i - n1 * nb1) % nb2)
    nb = jnp.where(first_grp, nb1, nb2)
    has_prev = pos > 0
    has_next = pos < nb - 1

    W = RADIUS
    kw[0:W] = kp_ref[...]
    kw[W:W + TQ] = kc_ref[...]
    kw[W + TQ:W + TQ + W] = kn_ref[...]
    vw[0:W] = vp_ref[...]
    vw[W:W + TQ] = vc_ref[...]
    vw[W + TQ:W + TQ + W] = vn_ref[...]

    lane = lax.broadcasted_iota(jnp.int32, (UQ, LANES), 1)
    low_half = lane < HEAD_DIM
    head_a = (lane % HEAD_DIM) < (HEAD_DIM // 2)
    nu = TQ // UQ
    for u in range(nu):
        for hp in range(N_HEADS // 2):
            cs = slice(hp * LANES, (hp + 1) * LANES)
            q2 = q_ref[u * UQ:(u + 1) * UQ, cs]
            kwin = kw[u * UQ:u * UQ + 2 * UQ, cs]
            zero = jnp.zeros_like(q2)
            s_scr[u, 2 * hp] = _dot_nt(jnp.where(head_a, q2, zero), kwin)
            s_scr[u, 2 * hp + 1] = _dot_nt(jnp.where(head_a, zero, q2), kwin)

    rr = lax.broadcasted_iota(jnp.int32, (UQ, 2 * UQ), 0)
    cc = lax.broadcasted_iota(jnp.int32, (UQ, 2 * UQ), 1)
    band = jnp.abs((cc - W) - rr) <= RADIUS
    l_ref[...] = jnp.zeros(l_ref.shape, F32)
    for u in range(nu):
        valid = band
        if u == 0:
            valid = valid & ((cc >= W) | has_prev)
        if u == nu - 1:
            valid = valid & ((cc < 2 * UQ - W) | has_next)
        s = jnp.where(valid[None], s_scr[u], NEG_BIG)
        m = jnp.max(s, axis=-1, keepdims=True)
        p = jnp.exp(s - m)
        den = jnp.sum(p, axis=-1, keepdims=True)
        p_scr[u] = p.astype(BF16)
        rden = 1.0 / den
        lse = m + jnp.log(den)
        rows = slice(u * UQ, (u + 1) * UQ)
        for hp in range(N_HEADS // 2):
            cs = slice(hp * LANES, (hp + 1) * LANES)
            vwin = vw[u * UQ:u * UQ + 2 * UQ, cs]
            oa = _dot(p_scr[u, 2 * hp], vwin) * rden[2 * hp]
            ob = _dot(p_scr[u, 2 * hp + 1], vwin) * rden[2 * hp + 1]
            o_ref[rows, cs] = jnp.where(low_half, oa, ob).astype(BF16)
        for h in range(N_HEADS):
            l_ref[rows, h:h + 1] = lse[h]


def _attention_group(qkv_g, groups):
    r, rows, _ = qkv_g.shape
    W = RADIUS
    for (B, S) in groups:
        assert S % (r * TQ) == 0
    seq_blocks = tuple((B, S // r // TQ) for (B, S) in groups)
    q64 = TQ // W
    last64 = rows // W - 1

    def big(col):
        return pl.BlockSpec((None, TQ, D), lambda res, i: (res, i, col))

    def prv(col):
        return pl.BlockSpec((None, W, D), lambda res, i: (res, jnp.maximum(i * q64 - 1, 0), col))

    def nxt(col):
        return pl.BlockSpec((None, W, D), lambda res, i: (res, jnp.minimum((i + 1) * q64, last64), col))

    return pl.pallas_call(
        functools.partial(_attn_kernel, seq_blocks=seq_blocks),
        grid=(r, rows // TQ),
        in_specs=[big(0), prv(1), big(1), nxt(1), prv(2), big(2), nxt(2)],
        out_specs=[pl.BlockSpec((None, TQ, D), lambda res, i: (res, i, 0)),
                   pl.BlockSpec((None, TQ, LANES), lambda res, i: (res, i, 0))],
        out_shape=[jax.ShapeDtypeStruct((r, rows, D), BF16),
                   jax.ShapeDtypeStruct((r, rows, LANES), F32)],
        scratch_shapes=[pltpu.VMEM((TQ + 2 * W, D), BF16), pltpu.VMEM((TQ + 2 * W, D), BF16),
                        pltpu.VMEM((TQ // UQ, N_HEADS, UQ, 2 * UQ), F32),
                        pltpu.VMEM((TQ // UQ, N_HEADS, UQ, 2 * UQ), BF16)],
        compiler_params=_cparams(("parallel", "parallel")),
        name=f"attn_d{r}",
    )(*([qkv_g] * 7))


def _router(x1, m, g_ffn, wr_t):
    h2 = _norm_mod(x1, g_ffn, m[3:4], m[4:5])
    logits_t = _dot3_nt(wr_t, h2)
    mx = jnp.max(logits_t, axis=0, keepdims=True)
    e = jnp.exp(logits_t - mx)
    return h2, e / jnp.sum(e, axis=0, keepdims=True)


def _store_token_tiles(ref, val):
    rows, cols = val.shape
    nsub = cols // LANES
    for sub in range(nsub):
        ref[pl.ds(sub, rows, stride=nsub), :] = val[:, sub * LANES:(sub + 1) * LANES]


def _oproj_kernel(tb_ref, o1_ref, o4_ref, o16_ref, l1_ref, l4_ref, l16_ref, xa_ref, xb_ref, mod_ref, w_ref,
                  gf_ref, wr_ref, x1_ref, h2_ref, aff_ref, oslab, lslab, om_scr, *, n_first_tiles):
    tm = xa_ref.shape[0]
    nc = D // LANES
    for g, (r, o_ref, l_ref) in enumerate(zip(DILATIONS, (o1_ref, o4_ref, o16_ref),
                                              (l1_ref, l4_ref, l16_ref))):
        n = tm // r
        for res in range(r):
            rows = slice(None) if r == 1 else pl.ds(res, n, stride=r)
            lslab[g, 0, rows, :] = l_ref[res]
            for c in range(nc):
                oslab[g, c, rows, :] = o_ref[res, :, c * LANES:(c + 1) * LANES].astype(F32)
    ls = [lslab[g, 0] for g in range(N_GROUPS)]
    mx = jnp.maximum(jnp.maximum(ls[0], ls[1]), ls[2])
    es = [jnp.exp(l - mx) for l in ls]
    tot = es[0] + es[1] + es[2]
    hrow = lax.broadcasted_iota(jnp.int32, (LANES, D), 0)
    hcol = lax.broadcasted_iota(jnp.int32, (LANES, D), 1)
    expand = jnp.where((hrow < 2 * N_HEADS) & (hcol // HEAD_DIM == hrow % N_HEADS), 1.0, 0.0).astype(BF16)
    lane = lax.broadcasted_iota(jnp.int32, (tm, LANES), 1)
    wexp = []
    for g in range(N_GROUPS):
        w = es[g] / tot
        hi = w.astype(BF16).astype(F32)
        packed = jnp.where(lane < N_HEADS, hi, pltpu.roll(w - hi, N_HEADS, 1))
        wexp.append(_dot(packed.astype(BF16), expand))
    for c in range(nc):
        cs = slice(c * LANES, (c + 1) * LANES)
        om = wexp[0][:, cs] * oslab[0, c] + wexp[1][:, cs] * oslab[1, c] + wexp[2][:, cs] * oslab[2, c]
        om_scr[:, cs] = om.astype(BF16)
    m = mod_ref[...]
    y = _dot(om_scr[...], w_ref[...])
    x = jnp.where(pl.program_id(0) < n_first_tiles, xa_ref[...], xb_ref[...])
    x1 = x + m[2:3] * y
    x1_ref[...] = x1
    h2, aff_t = _router(x1, m, gf_ref[...], wr_ref[...])
    _store_token_tiles(h2_ref, h2)
    aff_ref[...] = aff_t


def _out_proj(os_, ls_, xa, xb, mod, w_bf, g_ffn, wr_t, groups):
    M = xa.shape[0] + xb.shape[0]
    tm = TM_PROJ
    n_first_tiles = xa.shape[0] // tm
    tb, _ = _tile_tables(groups, tm)
    o_specs = [pl.BlockSpec((r, tm // r, D), lambda i, tb: (0, i, 0)) for r in DILATIONS]
    l_specs = [pl.BlockSpec((r, tm // r, LANES), lambda i, tb: (0, i, 0)) for r in DILATIONS]
    grid_spec = pltpu.PrefetchScalarGridSpec(
        num_scalar_prefetch=1,
        grid=(M // tm,),
        in_specs=o_specs + l_specs + _two_group_specs(tm, n_first_tiles) + [
            pl.BlockSpec((None, 6, D), lambda i, tb: (tb[i], 0, 0)),
            pl.BlockSpec((D, D), lambda i, tb: (0, 0)),
            pl.BlockSpec((1, D), lambda i, tb: (0, 0)),
            pl.BlockSpec((N_EXPERTS, D), lambda i, tb: (0, 0)),
        ],
        out_specs=[
            pl.BlockSpec((tm, D), lambda i, tb: (i, 0)),
            pl.BlockSpec((tm * (D // LANES), LANES), lambda i, tb: (i, 0)),
            pl.BlockSpec((N_EXPERTS, tm), lambda i, tb: (0, i)),
        ],
        scratch_shapes=[pltpu.VMEM((N_GROUPS, D // LANES, tm, LANES), F32),
                        pltpu.VMEM((N_GROUPS, 1, tm, LANES), F32),
                        pltpu.VMEM((tm, D), BF16)],
    )
    return pl.pallas_call(
        functools.partial(_oproj_kernel, n_first_tiles=n_first_tiles),
        grid_spec=grid_spec,
        out_shape=[jax.ShapeDtypeStruct((M, D), F32), jax.ShapeDtypeStruct((M * (D // LANES), LANES), F32),
                   jax.ShapeDtypeStruct((N_EXPERTS, M), F32)],
        compiler_params=_cparams(("parallel",)),
        name="attn_out_router",
    )(tb, *os_, *ls_, xa, xb, mod, w_bf, g_ffn, wr_t)


def _sgu_kernel(tb_ref, x_ref, mod_ref, g_ref, win_ref, lng_ref, lnb_ref, ws_ref, bs_ref, wout_ref,
                gf_ref, wr_ref, x1_ref, h2_ref, aff_ref, h_scr, v_scr, vn_scr, u_scr, g_scr):
    m = mod_ref[...]
    h_scr[...] = _norm_mod(x_ref[...], g_ref[...], m[0:1], m[1:2]).astype(BF16)
    tm = x_ref.shape[0]
    nc = 2 * LANES
    nchunks = SGU_HALF // nc
    s1 = jnp.zeros((tm, 1), F32)
    s2 = jnp.zeros((tm, 1), F32)
    for c in range(nchunks):
        z = _gelu_tanh(_dot(h_scr[...], win_ref[:, SGU_HALF + c * nc:SGU_HALF + (c + 1) * nc]))
        v_scr[:, c * nc:(c + 1) * nc] = z
        s1 = s1 + jnp.sum(z, axis=-1, keepdims=True)
        s2 = s2 + jnp.sum(z * z, axis=-1, keepdims=True)
    mu = s1 * np.float32(1.0 / SGU_HALF)
    var = s2 * np.float32(1.0 / SGU_HALF) - mu * mu
    rstd = lax.rsqrt(var + NORM_EPS)
    for c in range(nchunks):
        cs = slice(c * nc, (c + 1) * nc)
        vn_scr[:, cs] = ((v_scr[:, cs] - mu) * rstd * lng_ref[:, cs] + lnb_ref[:, cs]).astype(BF16)
        u_scr[:, cs] = _gelu_tanh(_dot(h_scr[...], win_ref[:, cs])).astype(BF16)
    for hd in range(SGU_HEADS):
        cs = slice(hd * SGU_HEAD_CH, (hd + 1) * SGU_HEAD_CH)
        for k in range(tm // CHUNK):
            rs = slice(k * CHUNK, (k + 1) * CHUNK)
            v_scr[rs, cs] = _dot(ws_ref[hd], vn_scr[rs, cs]) + bs_ref[hd]
    for c in range(nchunks):
        cs = slice(c * nc, (c + 1) * nc)
        g_scr[:, cs] = (u_scr[:, cs].astype(F32) * v_scr[:, cs]).astype(BF16)
    y = _dot(g_scr[...], wout_ref[...])
    x1 = x_ref[...] + m[2:3] * y
    x1_ref[...] = x1
    h2, aff_t = _router(x1, m, gf_ref[...], wr_ref[...])
    _store_token_tiles(h2_ref, h2)
    aff_ref[...] = aff_t


def _sgu_layer(x, mod, g_mix, w_in_bf, ln_g, ln_b, w_s_bf, b_s_col, w_out_bf, g_ffn, wr_t, groups):
    M = x.shape[0]
    tm = TM_SGU
    tb, _ = _tile_tables(groups, tm)
    const = lambda *shape: pl.BlockSpec(shape, lambda i, tb: (0,) * len(shape),
                                        pipeline_mode=pl.Buffered(1))
    grid_spec = pltpu.PrefetchScalarGridSpec(
        num_scalar_prefetch=1,
        grid=(M // tm,),
        in_specs=[
            pl.BlockSpec((tm, D), lambda i, tb: (i, 0)),
            pl.BlockSpec((None, 6, D), lambda i, tb: (tb[i], 0, 0)),
            const(1, D),
            const(D, 2 * SGU_HALF),
            const(1, SGU_HALF),
            const(1, SGU_HALF),
            const(SGU_HEADS, CHUNK, CHUNK),
            const(SGU_HEADS, CHUNK, 1),
            const(SGU_HALF, D),
            const(1, D),
            const(N_EXPERTS, D),
        ],
        out_specs=[
            pl.BlockSpec((tm, D), lambda i, tb: (i, 0)),
            pl.BlockSpec((tm * (D // LANES), LANES), lambda i, tb: (i, 0)),
            pl.BlockSpec((N_EXPERTS, tm), lambda i, tb: (0, i)),
        ],
        scratch_shapes=[pltpu.VMEM((tm, D), BF16), pltpu.VMEM((tm, SGU_HALF), F32),
                        pltpu.VMEM((tm, SGU_HALF), BF16), pltpu.VMEM((tm, SGU_HALF), BF16),
                        pltpu.VMEM((tm, SGU_HALF), BF16)],
    )
    return pl.pallas_call(
        _sgu_kernel,
        grid_spec=grid_spec,
        out_shape=[jax.ShapeDtypeStruct((M, D), F32), jax.ShapeDtypeStruct((M * (D // LANES), LANES), F32),
                   jax.ShapeDtypeStruct((N_EXPERTS, M), F32)],
        compiler_params=_cparams(("parallel",)),
        name="sgu_router",
    )(tb, x, mod, g_mix, w_in_bf, ln_g, ln_b, w_s_bf, b_s_col, w_out_bf, g_ffn, wr_t)


def _topk_kernel(aff_ref, idx_ref, gate_ref, pos_ref, bst_ref, incl_scr, tot_scr, bex_scr, *, nb, cap,
                 tok_base, slot_base):
    E = N_EXPERTS
    a = aff_ref[...]
    bits = pltpu.bitcast(a, jnp.int32)

    def count(mask):
        c = jnp.sum(jnp.where(mask, 1.0, 0.0), axis=1, keepdims=True)
        return jnp.sum(c, axis=2, keepdims=True)

    def search(k, thr):
        cand = thr | jnp.left_shift(jnp.int32(1), 30 - k)
        return jnp.where(count(bits >= cand) >= cap, cand, thr)

    thr = lax.fori_loop(0, 31, search, jnp.zeros((E, 1, 1), jnp.int32))
    gt = bits > thr
    eq = bits == thr
    need = cap - count(gt)

    row = lax.broadcasted_iota(jnp.int32, (LANES, LANES), 0)
    col = lax.broadcasted_iota(jnp.int32, (LANES, LANES), 1)
    upper = jnp.where(row <= col, 1.0, 0.0).astype(BF16)
    ones = jnp.ones((LANES, LANES), BF16)
    brow = lax.broadcasted_iota(jnp.int32, (nb, nb), 0)
    bcol = lax.broadcasted_iota(jnp.int32, (nb, nb), 1)
    lower_strict = jnp.where(bcol < brow, 1.0, 0.0).astype(BF16)

    def prefix(mask):
        xb = jnp.where(mask, 1.0, 0.0).astype(BF16).reshape(E * nb, LANES)
        incl_scr[...] = _dot(xb, upper).reshape(E, nb, LANES)
        tot_scr[...] = _dot(xb, ones).reshape(E, nb, LANES)
        for e in range(E):
            bex_scr[e] = _dot(lower_strict, tot_scr[e].astype(BF16))

    prefix(eq)
    rank_eq = bex_scr[...] + incl_scr[...] - jnp.where(eq, 1.0, 0.0)
    sel = gt | (eq & (rank_eq < need))
    prefix(sel)
    self32 = jnp.where(sel, 1.0, 0.0)
    pos = bex_scr[...] + incl_scr[...] - self32
    pos_ref[...] = jnp.where(sel, pos + np.float32(slot_base), -1.0)
    bst_ref[...] = bex_scr[...] + np.float32(slot_base)

    sc = SLOT_CHUNK
    bidx = lax.broadcasted_iota(jnp.int32, (nb, sc), 0).astype(F32)
    jidx = lax.broadcasted_iota(jnp.int32, (LANES, sc), 0).astype(F32)
    for e in range(E):
        tot_e = tot_scr[e][:, 0:1]
        binc_e = bex_scr[e][:, 0:1] + tot_e
        incl_t = jnp.transpose(incl_scr[e]).astype(BF16)
        aff_parts = _split3(jnp.transpose(a[e]))
        for c in range(cap // sc):
            s = (lax.broadcasted_iota(jnp.int32, (nb, sc), 1) + c * sc).astype(F32)
            before = binc_e <= s
            blk = jnp.sum(jnp.where(before, 1.0, 0.0), axis=0, keepdims=True)
            base = jnp.sum(jnp.where(before, tot_e, 0.0), axis=0, keepdims=True)
            local = s[0:1] - base
            onehot = jnp.where(bidx == blk, 1.0, 0.0).astype(BF16)
            rows = _dot(incl_t, onehot)
            cnt = jnp.sum(jnp.where(rows <= local, 1.0, 0.0), axis=0, keepdims=True)
            tok = blk * np.float32(LANES) + cnt + np.float32(tok_base)
            idx_ref[e:e + 1, c * sc:(c + 1) * sc] = tok.astype(jnp.int32)
            arow = (_dot(aff_parts[0], onehot) + _dot(aff_parts[1], onehot)) + _dot(aff_parts[2], onehot)
            gate = jnp.sum(jnp.where(jidx == cnt, arow, 0.0), axis=0, keepdims=True)
            gate_col = jnp.transpose(jnp.broadcast_to(gate, (8, sc)))[:, 0:1]
            gate_ref[e, c * sc:(c + 1) * sc, :] = jnp.broadcast_to(gate_col, (sc, LANES))


def _topk_group(aff_t, tok_base, n, slot_base):
    E = N_EXPERTS
    M = aff_t.shape[1]
    nb = n // LANES
    cap = CAPACITY_FACTOR * n // E
    assert n % LANES == 0 and nb % 8 == 0 and cap % SLOT_CHUNK == 0 and tok_base % n == 0
    a3 = aff_t.reshape(E, M // LANES, LANES)
    blk = tok_base // n
    return pl.pallas_call(
        functools.partial(_topk_kernel, nb=nb, cap=cap, tok_base=tok_base, slot_base=slot_base),
        grid=(1,),
        in_specs=[pl.BlockSpec((E, nb, LANES), lambda i: (0, blk, 0))],
        out_specs=[pl.BlockSpec((E, cap), lambda i: (0, 0)),
                   pl.BlockSpec((E, cap, LANES), lambda i: (0, 0, 0)),
                   pl.BlockSpec((E, nb, LANES), lambda i: (0, 0, 0)),
                   pl.BlockSpec((E, nb, LANES), lambda i: (0, 0, 0))],
        out_shape=[jax.ShapeDtypeStruct((E, cap), jnp.int32),
                   jax.ShapeDtypeStruct((E, cap, LANES), F32),
                   jax.ShapeDtypeStruct((E, nb, LANES), F32),
                   jax.ShapeDtypeStruct((E, nb, LANES), F32)],
        scratch_shapes=[pltpu.VMEM((E, nb, LANES), F32)] * 3,
        compiler_params=_cparams(("arbitrary",)),
        name=f"expert_choice_topk_{n}",
    )(a3)


def _ffn_kernel(idx_ref, h_hbm, gate_ref, wg_hbm, wu_hbm, wd_hbm, ye_ref, xbuf, x_scr, wg_buf, wu_buf, wd_buf,
                sem, wsem, *, layer, tiles_per_expert, n_steps):
    e = pl.program_id(0)
    j = pl.program_id(1)
    step = e * tiles_per_expert + j
    slot = step % 2
    wslot = e % 2

    def weight_copies(expert, ws):
        return (pltpu.make_async_copy(wg_hbm.at[layer, expert], wg_buf.at[ws], wsem.at[ws, 0]),
                pltpu.make_async_copy(wu_hbm.at[layer, expert], wu_buf.at[ws], wsem.at[ws, 1]),
                pltpu.make_async_copy(wd_hbm.at[layer, expert], wd_buf.at[ws], wsem.at[ws, 2]))

    def row_copy(tile, k, slot_):
        t = pl.multiple_of(idx_ref[tile * FFN_TILE + k] * NSUB, NSUB)
        return pltpu.make_async_copy(h_hbm.at[pl.ds(t, NSUB), :], xbuf.at[slot_, pl.ds(k * NSUB, NSUB), :],
                                     sem.at[slot_])

    @pl.when(step == 0)
    def _():
        for cp in weight_copies(0, 0):
            cp.start()

        def issue(k, carry):
            row_copy(0, k, 0).start()
            return carry
        lax.fori_loop(0, FFN_TILE, issue, 0, unroll=8)

    @pl.when(j == 0)
    def _():
        for cp in weight_copies(e, wslot):
            cp.wait()

        @pl.when(e + 1 < pl.num_programs(0))
        def _():
            for cp in weight_copies(e + 1, 1 - wslot):
                cp.start()

    for k in range(FFN_TILE):
        row_copy(step, k, slot).wait()
    nxt = jnp.minimum(step + 1, n_steps - 1)

    def compute(ws):
        for sub in range(NSUB):
            x_scr[:, sub * LANES:(sub + 1) * LANES] = xbuf[slot, pl.ds(sub, FFN_TILE, stride=NSUB), :].astype(BF16)
        for k in range(FFN_TILE):
            row_copy(nxt, k, 1 - slot).start()
        acc = jnp.zeros((FFN_TILE, D), F32)
        for c in range(D_EXPERT // F_CHUNK):
            cs = slice(c * F_CHUNK, (c + 1) * F_CHUNK)
            hg = _dot(x_scr[...], wg_buf[ws, :, cs])
            hu = _dot(x_scr[...], wu_buf[ws, :, cs])
            act = (hg * _sigmoid(hg) * hu).astype(BF16)
            acc = acc + _dot(act, wd_buf[ws, cs, :])
        gate = gate_ref[...]
        for c in range(D // LANES):
            cs = slice(c * LANES, (c + 1) * LANES)
            ye_ref[:, cs] = (acc[:, cs] * gate).astype(BF16)

    for ws in range(2):
        @pl.when(wslot == ws)
        def _():
            compute(ws)

    @pl.when(step == n_steps - 1)
    def _():
        for k in range(FFN_TILE):
            row_copy(nxt, k, 1 - slot).wait()


def _expert_ffn(idx_flat, h2, gate_all, wg_bf, wu_bf, wd_bf, layer, slots):
    E = N_EXPERTS
    tiles = slots // FFN_TILE
    grid_spec = pltpu.PrefetchScalarGridSpec(
        num_scalar_prefetch=1,
        grid=(E, tiles),
        in_specs=[
            pl.BlockSpec(memory_space=pl.ANY),
            pl.BlockSpec((FFN_TILE, LANES), lambda e, j, idx: (e * tiles + j, 0)),
            pl.BlockSpec(memory_space=pl.ANY),
            pl.BlockSpec(memory_space=pl.ANY),
            pl.BlockSpec(memory_space=pl.ANY),
        ],
        out_specs=pl.BlockSpec((None, FFN_TILE, D), lambda e, j, idx: (e, j, 0)),
        scratch_shapes=[pltpu.VMEM((2, FFN_TILE * NSUB, LANES), F32), pltpu.VMEM((FFN_TILE, D), BF16),
                        pltpu.VMEM((2, D, D_EXPERT), BF16), pltpu.VMEM((2, D, D_EXPERT), BF16),
                        pltpu.VMEM((2, D_EXPERT, D), BF16),
                        pltpu.SemaphoreType.DMA((2,)), pltpu.SemaphoreType.DMA((2, 3))],
    )
    return pl.pallas_call(
        functools.partial(_ffn_kernel, layer=layer, tiles_per_expert=tiles, n_steps=E * tiles),
        grid_spec=grid_spec,
        out_shape=jax.ShapeDtypeStruct((E, slots, D), BF16),
        compiler_params=_cparams(("arbitrary", "arbitrary")),
        name="expert_ffn",
    )(idx_flat, h2, gate_all.reshape(E * slots, LANES), wg_bf, wu_bf, wd_bf)


def _combine_kernel(tb_ref, bst_ref, x_ref, pos_ref, mod_ref, gfin_ref, ye_hbm, *rest, slots, blocks,
                    n_tiles, final_norm, split):
    if split is None:
        (o_ref, win, winx, acc, sem, semx) = rest
    else:
        (o_ref, o2_ref, win, winx, acc, sem, semx) = rest
    E = N_EXPERTS
    i = pl.program_id(0)
    bpt = TM_COMB // LANES
    stride = blocks + 1

    def tile_start(tile, e):
        return (bst_ref[e * stride + tile * bpt] // 16) * 16

    def window_copy(e, lo, buf, sem_):
        w0 = pl.multiple_of(jnp.minimum(lo, slots - WIN), 16)
        return pltpu.make_async_copy(ye_hbm.at[e, pl.ds(w0, WIN), :], buf.at[pl.ds(e * WIN, WIN), :],
                                     sem_.at[e])

    slot = i % 2

    @pl.when(i == 0)
    def _():
        for e in range(E):
            window_copy(e, tile_start(0, e), win.at[0], sem.at[0]).start()

    starts = [tile_start(i, e) for e in range(E)]
    npass = jnp.int32(1)
    for e in range(E):
        end = bst_ref[e * stride + (i + 1) * bpt]
        npass = jnp.maximum(npass, (end - starts[e] + (WIN - 1)) // WIN)

    for e in range(E):
        window_copy(e, starts[e], win.at[slot], sem.at[slot]).wait()
    nxt = jnp.minimum(i + 1, n_tiles - 1)
    for e in range(E):
        window_copy(e, tile_start(nxt, e), win.at[1 - slot], sem.at[1 - slot]).start()

    lane_e = lax.broadcasted_iota(jnp.int32, (1, E), 1)
    erow = lax.broadcasted_iota(jnp.int32, (E, E * WIN), 0)
    ecol = lax.broadcasted_iota(jnp.int32, (E, E * WIN), 1)
    expand = jnp.where(ecol // WIN == erow, 1.0, 0.0).astype(BF16)
    sprime = (lax.broadcasted_iota(jnp.int32, (1, E * WIN), 1) % WIN).astype(F32)
    pos = pos_ref[...]

    def scatter_rows(p, wbuf):
        lo_vec = jnp.zeros((1, E), F32)
        w0_vec = jnp.zeros((1, E), F32)
        for e in range(E):
            lo = starts[e] + p * WIN
            w0 = jnp.minimum(lo, slots - WIN)
            lo_vec = jnp.where(lane_e == e, lo.astype(F32), lo_vec)
            w0_vec = jnp.where(lane_e == e, w0.astype(F32), w0_vec)
        rel = jnp.where(pos >= lo_vec, pos - w0_vec, -1.0)
        rel = jnp.clip(rel, -1.0, np.float32(WIN))
        relx = _dot(rel.astype(BF16), expand)
        onehot = jnp.where(relx == sprime, 1.0, 0.0).astype(BF16)
        return _dot(onehot, wbuf[...])

    acc[...] = scatter_rows(0, win.at[slot])

    def extra_pass(p, carry):
        copies = [window_copy(e, starts[e] + p * WIN, winx, semx) for e in range(E)]
        for cp in copies:
            cp.start()
        for cp in copies:
            cp.wait()
        acc[...] += scatter_rows(p, winx)
        return carry

    lax.fori_loop(1, npass, extra_pass, 0)

    m = mod_ref[...]
    y = x_ref[...] + m[5:6] * acc[...]
    if final_norm:
        ms = jnp.mean(y * y, axis=-1, keepdims=True)
        y = y * lax.rsqrt(ms + NORM_EPS) * gfin_ref[...]
    if split is None:
        o_ref[...] = y
    else:
        @pl.when(i < split)
        def _():
            o_ref[...] = y

        @pl.when(i >= split)
        def _():
            o2_ref[...] = y

    @pl.when(i == n_tiles - 1)
    def _():
        for e in range(E):
            window_copy(e, tile_start(nxt, e), win.at[1 - slot], sem.at[1 - slot]).wait()


def _combine(x1, pos_t, mod, g_final, ye, bst_flat, groups, final_norm):
    M = x1.shape[0]
    tm = TM_COMB
    E = N_EXPERTS
    slots = ye.shape[1]
    tb, _ = _tile_tables(groups, tm)
    n_tiles = M // tm
    n1 = groups[0][0] * groups[0][1]
    if final_norm:
        split = n1 // tm
        out_specs = [pl.BlockSpec((tm, D), lambda i, tb, bst: (jnp.minimum(i, split - 1), 0)),
                     pl.BlockSpec((tm, D), lambda i, tb, bst: (jnp.maximum(i - split, 0), 0))]
        out_shape = [jax.ShapeDtypeStruct((n1, D), F32), jax.ShapeDtypeStruct((M - n1, D), F32)]
    else:
        split = None
        out_specs = pl.BlockSpec((tm, D), lambda i, tb, bst: (i, 0))
        out_shape = jax.ShapeDtypeStruct((M, D), F32)
    grid_spec = pltpu.PrefetchScalarGridSpec(
        num_scalar_prefetch=2,
        grid=(n_tiles,),
        in_specs=[
            pl.BlockSpec((tm, D), lambda i, tb, bst: (i, 0)),
            pl.BlockSpec((tm, E), lambda i, tb, bst: (i, 0)),
            pl.BlockSpec((None, 6, D), lambda i, tb, bst: (tb[i], 0, 0)),
            pl.BlockSpec((1, D), lambda i, tb, bst: (0, 0)),
            pl.BlockSpec(memory_space=pl.ANY),
        ],
        out_specs=out_specs,
        scratch_shapes=[pltpu.VMEM((2, E * WIN, D), BF16), pltpu.VMEM((E * WIN, D), BF16),
                        pltpu.VMEM((tm, D), F32), pltpu.SemaphoreType.DMA((2, E)),
                        pltpu.SemaphoreType.DMA((E,))],
    )
    return pl.pallas_call(
        functools.partial(_combine_kernel, slots=slots, blocks=M // LANES, n_tiles=n_tiles,
                          final_norm=final_norm, split=split),
        grid_spec=grid_spec,
        out_shape=out_shape,
        compiler_params=_cparams(("arbitrary",)),
        name="moe_combine",
    )(tb, bst_flat, x1, pos_t, mod, g_final, ye)


def _moe(x1, h2, aff_t, mod, g_final, wg_bf, wu_bf, wd_bf, layer, groups, final_norm):
    E = N_EXPERTS
    idxs, gates, poss, bsts = [], [], [], []
    tok_base = 0
    slot_base = 0
    for (B, S) in groups:
        n = B * S
        idx, gate, pos, bst = _topk_group(aff_t, tok_base, n, slot_base)
        idxs.append(idx)
        gates.append(gate)
        poss.append(pos.reshape(E, n))
        bsts.append(bst[:, :, 0])
        tok_base += n
        slot_base += CAPACITY_FACTOR * n // E
    slots = slot_base
    idx_flat = jnp.concatenate(idxs, axis=1).reshape(-1)
    ye = _expert_ffn(idx_flat, h2, jnp.concatenate(gates, axis=1), wg_bf, wu_bf, wd_bf, layer, slots)
    pos_t = jnp.concatenate(poss, axis=1).T
    bst = jnp.concatenate(bsts + [jnp.full((E, 1), slots, F32)], axis=1)
    bst_flat = bst.astype(jnp.int32).reshape(-1)
    return _combine(x1, pos_t, mod, g_final, ye, bst_flat, groups, final_norm)


def kernel(x_prompt, x_sample, c_prompt, c_sample, w_mod, b_mod, g_mix, g_ffn, a_w_in, a_w_out, b_w_in,
           b_ln_g, b_ln_b, b_w_s, b_b_s, b_w_out, moe_w_router, moe_w_gate, moe_w_up, moe_w_down, g_final):
    groups = (x_prompt.shape[:2], x_sample.shape[:2])
    assert x_prompt.shape[2] == D and x_sample.shape[2] == D
    xa, xb = x_prompt.reshape(-1, D), x_sample.reshape(-1, D)
    n_seq = groups[0][0] + groups[1][0]
    rows = -(-n_seq // 8) * 8
    c_all = jnp.concatenate([c_prompt, c_sample, jnp.zeros((rows - n_seq, D), F32)], axis=0)
    mod = _modulation(c_all, w_mod, b_mod).reshape(w_mod.shape[0], rows, 6, D)
    cos_t, sin_t = _rope_tables(max(groups[0][1], groups[1][1]))
    wr_t = jnp.swapaxes(moe_w_router, 1, 2)
    gfin = g_final.reshape(1, D)

    os_, ls_ = [], []
    for g, r in enumerate(DILATIONS):
        w_g = _pair_heads(a_w_in[0][:, g * 3 * D:(g + 1) * 3 * D]).astype(BF16)
        qkv_g = _qkv_proj(xa, xb, mod[0], g_mix[0].reshape(1, D), w_g, _permute_table(cos_t, TM_PROJ, r),
                          _permute_table(sin_t, TM_PROJ, r), groups, r)
        o_g, l_g = _attention_group(qkv_g, groups)
        os_.append(o_g)
        ls_.append(l_g)
    x1, h2, aff_t = _out_proj(os_, ls_, xa, xb, mod[0], a_w_out[0].astype(BF16), g_ffn[0].reshape(1, D),
                              wr_t[0], groups)
    wg_bf, wu_bf, wd_bf = moe_w_gate.astype(BF16), moe_w_up.astype(BF16), moe_w_down.astype(BF16)
    x = _moe(x1, h2, aff_t, mod[0], gfin, wg_bf, wu_bf, wd_bf, 0, groups, final_norm=False)

    x1, h2, aff_t = _sgu_layer(x, mod[1], g_mix[1].reshape(1, D), b_w_in[0].astype(BF16),
                               b_ln_g[0].reshape(1, SGU_HALF), b_ln_b[0].reshape(1, SGU_HALF),
                               b_w_s[0].astype(BF16), b_b_s[0].reshape(SGU_HEADS, CHUNK, 1),
                               b_w_out[0].astype(BF16), g_ffn[1].reshape(1, D), wr_t[1], groups)
    y1, y2 = _moe(x1, h2, aff_t, mod[1], gfin, wg_bf, wu_bf, wd_bf, 1, groups, final_norm=True)

    return (y1.reshape(x_prompt.shape), y2.reshape(x_sample.shape))
```

```python
import functools

import numpy as np
import jax
import jax.numpy as jnp
from jax import lax
from jax.experimental import pallas as pl
from jax.experimental.pallas import tpu as pltpu

F32 = jnp.float32
BF16 = jnp.bfloat16

D = 1024
N_HEADS = 16
HEAD_DIM = 64
DILATIONS = (1, 4, 16)
WINDOWS = (128, 512, 2048)
RADIUS = 64
N_GROUPS = 3
QKV_COLS = N_GROUPS * 3 * D
ROPE_THETA = 10000.0
CHUNK = 128
SGU_HALF = 3 * D
SGU_HEADS = 16
SGU_HEAD_CH = SGU_HALF // SGU_HEADS
N_EXPERTS = 16
CAPACITY_FACTOR = 2
D_EXPERT = 2 * D
NORM_EPS = 1e-6
NEG_BIG = -1e30

LANES = 128
VMEM_LIMIT = 56 * 1024 * 1024

TM_PROJ = 512
TM_SGU = 256
TM_COMB = 256
TQ = 256
UQ = 128
FFN_TILE = 256
F_CHUNK = 512
WIN = 64
SLOT_CHUNK = 512


def _cparams(sem):
    return pltpu.CompilerParams(dimension_semantics=sem, vmem_limit_bytes=VMEM_LIMIT)


def _split2(x):
    hi = x.astype(BF16)
    lo = (x - hi.astype(F32)).astype(BF16)
    return hi, lo


def _split3(x):
    hi = x.astype(BF16)
    r1 = x - hi.astype(F32)
    mid = r1.astype(BF16)
    lo = (r1 - mid.astype(F32)).astype(BF16)
    return hi, mid, lo


def _dot(a, b):
    return jnp.dot(a, b, preferred_element_type=F32)


def _dot_nt(a, b):
    return lax.dot_general(a, b, (((1,), (1,)), ((), ())), preferred_element_type=F32)


def _dot3(a, b):
    ah, al = _split2(a)
    bh, bl = _split2(b)
    return _dot(ah, bh) + (_dot(ah, bl) + _dot(al, bh))


def _dot3_nt(a, b):
    ah, al = _split2(a)
    bh, bl = _split2(b)
    return _dot_nt(ah, bh) + (_dot_nt(ah, bl) + _dot_nt(al, bh))


def _sigmoid(x):
    return 1.0 / (1.0 + jnp.exp(-x))


def _gelu_tanh(x):
    c = np.float32(np.sqrt(2.0 / np.pi))
    return 0.5 * x * (1.0 + jnp.tanh(c * (x + np.float32(0.044715) * (x * x * x))))


def _norm_mod(x, g, shift, scale):
    ms = jnp.mean(x * x, axis=-1, keepdims=True)
    return (x * lax.rsqrt(ms + NORM_EPS) * g) * (1.0 + scale) + shift


def _tile_tables(groups, tm):
    tb, tp = [], []
    b0 = 0
    for (B, S) in groups:
        assert S % tm == 0
        for b in range(B):
            for k in range(S // tm):
                tb.append(b0 + b)
                tp.append(k)
        b0 += B
    return jnp.asarray(np.array(tb, np.int32)), jnp.asarray(np.array(tp, np.int32))


def _mod_kernel(c_ref, w_ref, b_ref, o_ref):
    c = c_ref[...]
    o_ref[...] = _dot3(c * _sigmoid(c), w_ref[...]) + b_ref[...]


def _modulation(c_pad, w_mod, b_mod):
    depth, _, n = w_mod.shape
    tn = 2048
    rows = c_pad.shape[0]
    return pl.pallas_call(
        _mod_kernel,
        grid=(depth, n // tn),
        in_specs=[
            pl.BlockSpec((rows, D), lambda l, j: (0, 0)),
            pl.BlockSpec((None, D, tn), lambda l, j: (l, 0, j)),
            pl.BlockSpec((None, 1, tn), lambda l, j: (l, 0, j)),
        ],
        out_specs=pl.BlockSpec((None, rows, tn), lambda l, j: (l, 0, j)),
        out_shape=jax.ShapeDtypeStruct((depth, rows, n), F32),
        compiler_params=_cparams(("parallel", "parallel")),
        name="modulation",
    )(c_pad, w_mod, b_mod.reshape(depth, 1, n))


def _permute_rows(src, slab, dst, r):
    tm, cols = src.shape
    n = tm // r
    for c in range(cols // LANES):
        slab[c] = src[:, c * LANES:(c + 1) * LANES]
    for res in range(r):
        for c in range(cols // LANES):
            dst[res * n:(res + 1) * n, c * LANES:(c + 1) * LANES] = slab[c, pl.ds(res, n, stride=r), :]


def _two_group_specs(tm, n_first_tiles):
    first = pl.BlockSpec((tm, D), lambda i, *_: (jnp.minimum(i, n_first_tiles - 1), 0))
    second = pl.BlockSpec((tm, D), lambda i, *_: (jnp.maximum(i - n_first_tiles, 0), 0))
    return [first, second]


def _qkv_kernel(tb_ref, tp_ref, xa_ref, xb_ref, mod_ref, g_ref, w_ref, cos_ref, sin_ref, o_ref, h_scr, *scr,
                r, n_first_tiles):
    m = mod_ref[...]
    x = jnp.where(pl.program_id(0) < n_first_tiles, xa_ref[...], xb_ref[...])
    if r > 1:
        slab, xp = scr
        _permute_rows(x, slab, xp, r)
        x = xp[...]
    h_scr[...] = _norm_mod(x, g_ref[...], m[0:1], m[1:2]).astype(BF16)

    n = h_scr.shape[0] // r
    nw = 2 * LANES
    cos = cos_ref[...]
    sin = sin_ref[...]
    qscale = np.float32(HEAD_DIM ** -0.5)
    tables = ((cos * qscale, sin * qscale), (cos, sin), None)
    for sec, table in enumerate(tables):
        for cw in range(D // nw):
            col0 = sec * D + cw * nw
            acc = _dot(h_scr[...], w_ref[:, col0:col0 + nw])
            for c in range(nw // LANES):
                xc = acc[:, c * LANES:(c + 1) * LANES]
                if table is not None:
                    xc = xc * table[0] + pltpu.roll(xc, LANES // 2, 1) * table[1]
                cs = slice(col0 + c * LANES, col0 + (c + 1) * LANES)
                o_ref[:, :, cs] = xc.reshape(r, n, LANES).astype(BF16)


def _qkv_proj(xa, xb, mod, g, w_bf, cos_p, sin_p, groups, r):
    M = xa.shape[0] + xb.shape[0]
    tm = TM_PROJ
    n_first_tiles = xa.shape[0] // tm
    tb, tp = _tile_tables(groups, tm)
    scratch = [pltpu.VMEM((tm, D), BF16)]
    if r > 1:
        scratch += [pltpu.VMEM((D // LANES, tm, LANES), F32), pltpu.VMEM((tm, D), F32)]
    grid_spec = pltpu.PrefetchScalarGridSpec(
        num_scalar_prefetch=2,
        grid=(M // tm,),
        in_specs=_two_group_specs(tm, n_first_tiles) + [
            pl.BlockSpec((None, 6, D), lambda i, tb, tp: (tb[i], 0, 0)),
            pl.BlockSpec((1, D), lambda i, tb, tp: (0, 0)),
            pl.BlockSpec((D, 3 * D), lambda i, tb, tp: (0, 0), pipeline_mode=pl.Buffered(1)),
            pl.BlockSpec((tm, LANES), lambda i, tb, tp: (tp[i], 0)),
            pl.BlockSpec((tm, LANES), lambda i, tb, tp: (tp[i], 0)),
        ],
        out_specs=pl.BlockSpec((r, tm // r, 3 * D), lambda i, tb, tp: (0, i, 0)),
        scratch_shapes=scratch,
    )
    return pl.pallas_call(
        functools.partial(_qkv_kernel, r=r, n_first_tiles=n_first_tiles),
        grid_spec=grid_spec,
        out_shape=jax.ShapeDtypeStruct((r, M // r, 3 * D), BF16),
        compiler_params=_cparams(("parallel",)),
        name=f"qkv_rope_d{r}",
    )(tb, tp, xa, xb, mod, g, w_bf, cos_p, sin_p)


def _rope_tables(s_max):
    half = HEAD_DIM // 2
    inv = ROPE_THETA ** (-jnp.arange(0, HEAD_DIM, 2, dtype=F32) / HEAD_DIM)
    ang = jnp.arange(s_max, dtype=F32)[:, None] * inv[None, :]
    cos, sin = jnp.cos(ang), jnp.sin(ang)
    cos_t = jnp.tile(cos, (1, LANES // half))
    sin_t = jnp.concatenate([-sin, -sin, sin, sin], axis=1)
    return cos_t, sin_t


def _pair_heads(w):
    half = HEAD_DIM // 2
    perm = np.concatenate([np.arange(0, half), np.arange(HEAD_DIM, HEAD_DIM + half),
                           np.arange(half, HEAD_DIM), np.arange(HEAD_DIM + half, LANES)])
    k = w.shape[0]
    qk = w[:, :2 * D].reshape(k, 2 * D // LANES, LANES)[:, :, perm].reshape(k, 2 * D)
    return jnp.concatenate([qk, w[:, 2 * D:]], axis=1)


def _permute_table(t, tm, r):
    s = t.shape[0]
    return t.reshape(s // tm, tm // r, r, LANES).transpose(0, 2, 1, 3).reshape(s, LANES)


def _attn_kernel(q_ref, kp_ref, kc_ref, kn_ref, vp_ref, vc_ref, vn_ref, o_ref, l_ref,
                 kw, vw, s_scr, p_scr, *, seq_blocks):
    i = pl.program_id(1)
    (n1, nb1), (n2, nb2) = seq_blocks
    first_grp = i < n1 * nb1
    pos = jnp.where(first_grp, i % nb1, (i - n1 * nb1) % nb2)
    nb = jnp.where(first_grp, nb1, nb2)
    has_prev = pos > 0
    has_next = pos < nb - 1

    W = RADIUS
    kw[0:W] = kp_ref[...]
    kw[W:W + TQ] = kc_ref[...]
    kw[W + TQ:W + TQ + W] = kn_ref[...]
    vw[0:W] = vp_ref[...]
    vw[W:W + TQ] = vc_ref[...]
    vw[W + TQ:W + TQ + W] = vn_ref[...]

    lane = lax.broadcasted_iota(jnp.int32, (UQ, LANES), 1)
    low_half = lane < HEAD_DIM
    head_a = (lane % HEAD_DIM) < (HEAD_DIM // 2)
    nu = TQ // UQ
    for u in range(nu):
        for hp in range(N_HEADS // 2):
            cs = slice(hp * LANES, (hp + 1) * LANES)
            q2 = q_ref[u * UQ:(u + 1) * UQ, cs]
            kwin = kw[u * UQ:u * UQ + 2 * UQ, cs]
            zero = jnp.zeros_like(q2)
            s_scr[u, 2 * hp] = _dot_nt(jnp.where(head_a, q2, zero), kwin)
            s_scr[u, 2 * hp + 1] = _dot_nt(jnp.where(head_a, zero, q2), kwin)

    rr = lax.broadcasted_iota(jnp.int32, (UQ, 2 * UQ), 0)
    cc = lax.broadcasted_iota(jnp.int32, (UQ, 2 * UQ), 1)
    band = jnp.abs((cc - W) - rr) <= RADIUS
    l_ref[...] = jnp.zeros(l_ref.shape, F32)
    for u in range(nu):
        valid = band
        if u == 0:
            valid = valid & ((cc >= W) | has_prev)
        if u == nu - 1:
            valid = valid & ((cc < 2 * UQ - W) | has_next)
        s = jnp.where(valid[None], s_scr[u], NEG_BIG)
        m = jnp.max(s, axis=-1, keepdims=True)
        p = jnp.exp(s - m)
        den = jnp.sum(p, axis=-1, keepdims=True)
        p_scr[u] = p.astype(BF16)
        rden = 1.0 / den
        lse = m + jnp.log(den)
        rows = slice(u * UQ, (u + 1) * UQ)
        for hp in range(N_HEADS // 2):
            cs = slice(hp * LANES, (hp + 1) * LANES)
            vwin = vw[u * UQ:u * UQ + 2 * UQ, cs]
            oa = _dot(p_scr[u, 2 * hp], vwin) * rden[2 * hp]
            ob = _dot(p_scr[u, 2 * hp + 1], vwin) * rden[2 * hp + 1]
            o_ref[rows, cs] = jnp.where(low_half, oa, ob).astype(BF16)
        for h in range(N_HEADS):
            l_ref[rows, h:h + 1] = lse[h]


def _attention_group(qkv_g, groups):
    r, rows, _ = qkv_g.shape
    W = RADIUS
    for (B, S) in groups:
        assert S % (r * TQ) == 0
    seq_blocks = tuple((B, S // r // TQ) for (B, S) in groups)
    q64 = TQ // W
    last64 = rows // W - 1

    def big(col):
        return pl.BlockSpec((None, TQ, D), lambda res, i: (res, i, col))

    def prv(col):
        return pl.BlockSpec((None, W, D), lambda res, i: (res, jnp.maximum(i * q64 - 1, 0), col))

    def nxt(col):
        return pl.BlockSpec((None, W, D), lambda res, i: (res, jnp.minimum((i + 1) * q64, last64), col))

    return pl.pallas_call(
        functools.partial(_attn_kernel, seq_blocks=seq_blocks),
        grid=(r, rows // TQ),
        in_specs=[big(0), prv(1), big(1), nxt(1), prv(2), big(2), nxt(2)],
        out_specs=[pl.BlockSpec((None, TQ, D), lambda res, i: (res, i, 0)),
                   pl.BlockSpec((None, TQ, LANES), lambda res, i: (res, i, 0))],
        out_shape=[jax.ShapeDtypeStruct((r, rows, D), BF16),
                   jax.ShapeDtypeStruct((r, rows, LANES), F32)],
        scratch_shapes=[pltpu.VMEM((TQ + 2 * W, D), BF16), pltpu.VMEM((TQ + 2 * W, D), BF16),
                        pltpu.VMEM((TQ // UQ, N_HEADS, UQ, 2 * UQ), F32),
                        pltpu.VMEM((TQ // UQ, N_HEADS, UQ, 2 * UQ), BF16)],
        compiler_params=_cparams(("parallel", "parallel")),
        name=f"attn_d{r}",
    )(*([qkv_g] * 7))


def _router(x1, m, g_ffn, wr_t):
    h2 = _norm_mod(x1, g_ffn, m[3:4], m[4:5])
    logits_t = _dot3_nt(wr_t, h2)
    mx = jnp.max(logits_t, axis=0, keepdims=True)
    e = jnp.exp(logits_t - mx)
    return h2, e / jnp.sum(e, axis=0, keepdims=True)


def _oproj_kernel(tb_ref, o1_ref, o4_ref, o16_ref, l1_ref, l4_ref, l16_ref, xa_ref, xb_ref, mod_ref, w_ref,
                  gf_ref, wr_ref, x1_ref, h2_ref, aff_ref, oslab, lslab, om_scr, *, n_first_tiles):
    tm = xa_ref.shape[0]
    nc = D // LANES
    for g, (r, o_ref, l_ref) in enumerate(zip(DILATIONS, (o1_ref, o4_ref, o16_ref),
                                              (l1_ref, l4_ref, l16_ref))):
        n = tm // r
        for res in range(r):
            rows = slice(None) if r == 1 else pl.ds(res, n, stride=r)
            lslab[g, 0, rows, :] = l_ref[res]
            for c in range(nc):
                oslab[g, c, rows, :] = o_ref[res, :, c * LANES:(c + 1) * LANES].astype(F32)
    ls = [lslab[g, 0] for g in range(N_GROUPS)]
    mx = jnp.maximum(jnp.maximum(ls[0], ls[1]), ls[2])
    es = [jnp.exp(l - mx) for l in ls]
    tot = es[0] + es[1] + es[2]
    hrow = lax.broadcasted_iota(jnp.int32, (LANES, D), 0)
    hcol = lax.broadcasted_iota(jnp.int32, (LANES, D), 1)
    expand = jnp.where((hrow < 2 * N_HEADS) & (hcol // HEAD_DIM == hrow % N_HEADS), 1.0, 0.0).astype(BF16)
    lane = lax.broadcasted_iota(jnp.int32, (tm, LANES), 1)
    wexp = []
    for g in range(N_GROUPS):
        w = es[g] / tot
        hi = w.astype(BF16).astype(F32)
        packed = jnp.where(lane < N_HEADS, hi, pltpu.roll(w - hi, N_HEADS, 1))
        wexp.append(_dot(packed.astype(BF16), expand))
    for c in range(nc):
        cs = slice(c * LANES, (c + 1) * LANES)
        om = wexp[0][:, cs] * oslab[0, c] + wexp[1][:, cs] * oslab[1, c] + wexp[2][:, cs] * oslab[2, c]
        om_scr[:, cs] = om.astype(BF16)
    m = mod_ref[...]
    y = _dot(om_scr[...], w_ref[...])
    x = jnp.where(pl.program_id(0) < n_first_tiles, xa_ref[...], xb_ref[...])
    x1 = x + m[2:3] * y
    x1_ref[...] = x1
    h2, aff_t = _router(x1, m, gf_ref[...], wr_ref[...])
    h2_ref[...] = h2
    aff_ref[...] = aff_t


def _out_proj(os_, ls_, xa, xb, mod, w_bf, g_ffn, wr_t, groups):
    M = xa.shape[0] + xb.shape[0]
    tm = TM_PROJ
    n_first_tiles = xa.shape[0] // tm
    tb, _ = _tile_tables(groups, tm)
    o_specs = [pl.BlockSpec((r, tm // r, D), lambda i, tb: (0, i, 0)) for r in DILATIONS]
    l_specs = [pl.BlockSpec((r, tm // r, LANES), lambda i, tb: (0, i, 0)) for r in DILATIONS]
    grid_spec = pltpu.PrefetchScalarGridSpec(
        num_scalar_prefetch=1,
        grid=(M // tm,),
        in_specs=o_specs + l_specs + _two_group_specs(tm, n_first_tiles) + [
            pl.BlockSpec((None, 6, D), lambda i, tb: (tb[i], 0, 0)),
            pl.BlockSpec((D, D), lambda i, tb: (0, 0)),
            pl.BlockSpec((1, D), lambda i, tb: (0, 0)),
            pl.BlockSpec((N_EXPERTS, D), lambda i, tb: (0, 0)),
        ],
        out_specs=[
            pl.BlockSpec((tm, D), lambda i, tb: (i, 0)),
            pl.BlockSpec((tm, D), lambda i, tb: (i, 0)),
            pl.BlockSpec((N_EXPERTS, tm), lambda i, tb: (0, i)),
        ],
        scratch_shapes=[pltpu.VMEM((N_GROUPS, D // LANES, tm, LANES), F32),
                        pltpu.VMEM((N_GROUPS, 1, tm, LANES), F32),
                        pltpu.VMEM((tm, D), BF16)],
    )
    return pl.pallas_call(
        functools.partial(_oproj_kernel, n_first_tiles=n_first_tiles),
        grid_spec=grid_spec,
        out_shape=[jax.ShapeDtypeStruct((M, D), F32), jax.ShapeDtypeStruct((M, D), F32),
                   jax.ShapeDtypeStruct((N_EXPERTS, M), F32)],
        compiler_params=_cparams(("parallel",)),
        name="attn_out_router",
    )(tb, *os_, *ls_, xa, xb, mod, w_bf, g_ffn, wr_t)


def _sgu_kernel(tb_ref, x_ref, mod_ref, g_ref, win_ref, lng_ref, lnb_ref, ws_ref, bs_ref, wout_ref,
                gf_ref, wr_ref, x1_ref, h2_ref, aff_ref, h_scr, v_scr, vn_scr, u_scr, g_scr):
    m = mod_ref[...]
    h_scr[...] = _norm_mod(x_ref[...], g_ref[...], m[0:1], m[1:2]).astype(BF16)
    tm = x_ref.shape[0]
    nc = 2 * LANES
    nchunks = SGU_HALF // nc
    s1 = jnp.zeros((tm, 1), F32)
    s2 = jnp.zeros((tm, 1), F32)
    for c in range(nchunks):
        z = _gelu_tanh(_dot(h_scr[...], win_ref[:, SGU_HALF + c * nc:SGU_HALF + (c + 1) * nc]))
        v_scr[:, c * nc:(c + 1) * nc] = z
        s1 = s1 + jnp.sum(z, axis=-1, keepdims=True)
        s2 = s2 + jnp.sum(z * z, axis=-1, keepdims=True)
    mu = s1 * np.float32(1.0 / SGU_HALF)
    var = s2 * np.float32(1.0 / SGU_HALF) - mu * mu
    rstd = lax.rsqrt(var + NORM_EPS)
    for c in range(nchunks):
        cs = slice(c * nc, (c + 1) * nc)
        vn_scr[:, cs] = ((v_scr[:, cs] - mu) * rstd * lng_ref[:, cs] + lnb_ref[:, cs]).astype(BF16)
        u_scr[:, cs] = _gelu_tanh(_dot(h_scr[...], win_ref[:, cs])).astype(BF16)
    for hd in range(SGU_HEADS):
        cs = slice(hd * SGU_HEAD_CH, (hd + 1) * SGU_HEAD_CH)
        for k in range(tm // CHUNK):
            rs = slice(k * CHUNK, (k + 1) * CHUNK)
            v_scr[rs, cs] = _dot(ws_ref[hd], vn_scr[rs, cs]) + bs_ref[hd]
    for c in range(nchunks):
        cs = slice(c * nc, (c + 1) * nc)
        g_scr[:, cs] = (u_scr[:, cs].astype(F32) * v_scr[:, cs]).astype(BF16)
    y = _dot(g_scr[...], wout_ref[...])
    x1 = x_ref[...] + m[2:3] * y
    x1_ref[...] = x1
    h2, aff_t = _router(x1, m, gf_ref[...], wr_ref[...])
    h2_ref[...] = h2
    aff_ref[...] = aff_t


def _sgu_layer(x, mod, g_mix, w_in_bf, ln_g, ln_b, w_s_bf, b_s_col, w_out_bf, g_ffn, wr_t, groups):
    M = x.shape[0]
    tm = TM_SGU
    tb, _ = _tile_tables(groups, tm)
    const = lambda *shape: pl.BlockSpec(shape, lambda i, tb: (0,) * len(shape),
                                        pipeline_mode=pl.Buffered(1))
    grid_spec = pltpu.PrefetchScalarGridSpec(
        num_scalar_prefetch=1,
        grid=(M // tm,),
        in_specs=[
            pl.BlockSpec((tm, D), lambda i, tb: (i, 0)),
            pl.BlockSpec((None, 6, D), lambda i, tb: (tb[i], 0, 0)),
            const(1, D),
            const(D, 2 * SGU_HALF),
            const(1, SGU_HALF),
            const(1, SGU_HALF),
            const(SGU_HEADS, CHUNK, CHUNK),
            const(SGU_HEADS, CHUNK, 1),
            const(SGU_HALF, D),
            const(1, D),
            const(N_EXPERTS, D),
        ],
        out_specs=[
            pl.BlockSpec((tm, D), lambda i, tb: (i, 0)),
            pl.BlockSpec((tm, D), lambda i, tb: (i, 0)),
            pl.BlockSpec((N_EXPERTS, tm), lambda i, tb: (0, i)),
        ],
        scratch_shapes=[pltpu.VMEM((tm, D), BF16), pltpu.VMEM((tm, SGU_HALF), F32),
                        pltpu.VMEM((tm, SGU_HALF), BF16), pltpu.VMEM((tm, SGU_HALF), BF16),
                        pltpu.VMEM((tm, SGU_HALF), BF16)],
    )
    return pl.pallas_call(
        _sgu_kernel,
        grid_spec=grid_spec,
        out_shape=[jax.ShapeDtypeStruct((M, D), F32), jax.ShapeDtypeStruct((M, D), F32),
                   jax.ShapeDtypeStruct((N_EXPERTS, M), F32)],
        compiler_params=_cparams(("parallel",)),
        name="sgu_router",
    )(tb, x, mod, g_mix, w_in_bf, ln_g, ln_b, w_s_bf, b_s_col, w_out_bf, g_ffn, wr_t)


def _topk_kernel(aff_ref, idx_ref, gate_ref, pos_ref, bst_ref, incl_scr, tot_scr, bex_scr, *, nb, cap,
                 tok_base, slot_base):
    E = N_EXPERTS
    a = aff_ref[...]
    bits = pltpu.bitcast(a, jnp.int32)

    def count(mask):
        c = jnp.sum(jnp.where(mask, 1.0, 0.0), axis=1, keepdims=True)
        return jnp.sum(c, axis=2, keepdims=True)

    def search(k, thr):
        cand = thr | jnp.left_shift(jnp.int32(1), 30 - k)
        return jnp.where(count(bits >= cand) >= cap, cand, thr)

    thr = lax.fori_loop(0, 31, search, jnp.zeros((E, 1, 1), jnp.int32))
    gt = bits > thr
    eq = bits == thr
    need = cap - count(gt)

    row = lax.broadcasted_iota(jnp.int32, (LANES, LANES), 0)
    col = lax.broadcasted_iota(jnp.int32, (LANES, LANES), 1)
    upper = jnp.where(row <= col, 1.0, 0.0).astype(BF16)
    ones = jnp.ones((LANES, LANES), BF16)
    brow = lax.broadcasted_iota(jnp.int32, (nb, nb), 0)
    bcol = lax.broadcasted_iota(jnp.int32, (nb, nb), 1)
    lower_strict = jnp.where(bcol < brow, 1.0, 0.0).astype(BF16)

    def prefix(mask):
        xb = jnp.where(mask, 1.0, 0.0).astype(BF16).reshape(E * nb, LANES)
        incl_scr[...] = _dot(xb, upper).reshape(E, nb, LANES)
        tot_scr[...] = _dot(xb, ones).reshape(E, nb, LANES)
        for e in range(E):
            bex_scr[e] = _dot(lower_strict, tot_scr[e].astype(BF16))

    prefix(eq)
    rank_eq = bex_scr[...] + incl_scr[...] - jnp.where(eq, 1.0, 0.0)
    sel = gt | (eq & (rank_eq < need))
    prefix(sel)
    self32 = jnp.where(sel, 1.0, 0.0)
    pos = bex_scr[...] + incl_scr[...] - self32
    pos_ref[...] = jnp.where(sel, pos + np.float32(slot_base), -1.0)
    bst_ref[...] = bex_scr[...] + np.float32(slot_base)

    sc = SLOT_CHUNK
    bidx = lax.broadcasted_iota(jnp.int32, (nb, sc), 0).astype(F32)
    jidx = lax.broadcasted_iota(jnp.int32, (LANES, sc), 0).astype(F32)
    for e in range(E):
        tot_e = tot_scr[e][:, 0:1]
        binc_e = bex_scr[e][:, 0:1] + tot_e
        incl_t = jnp.transpose(incl_scr[e]).astype(BF16)
        aff_parts = _split3(jnp.transpose(a[e]))
        for c in range(cap // sc):
            s = (lax.broadcasted_iota(jnp.int32, (nb, sc), 1) + c * sc).astype(F32)
            before = binc_e <= s
            blk = jnp.sum(jnp.where(before, 1.0, 0.0), axis=0, keepdims=True)
            base = jnp.sum(jnp.where(before, tot_e, 0.0), axis=0, keepdims=True)
            local = s[0:1] - base
            onehot = jnp.where(bidx == blk, 1.0, 0.0).astype(BF16)
            rows = _dot(incl_t, onehot)
            cnt = jnp.sum(jnp.where(rows <= local, 1.0, 0.0), axis=0, keepdims=True)
            tok = blk * np.float32(LANES) + cnt + np.float32(tok_base)
            idx_ref[e:e + 1, c * sc:(c + 1) * sc] = tok.astype(jnp.int32)
            arow = (_dot(aff_parts[0], onehot) + _dot(aff_parts[1], onehot)) + _dot(aff_parts[2], onehot)
            gate = jnp.sum(jnp.where(jidx == cnt, arow, 0.0), axis=0, keepdims=True)
            gate_col = jnp.transpose(jnp.broadcast_to(gate, (8, sc)))[:, 0:1]
            gate_ref[e, c * sc:(c + 1) * sc, :] = jnp.broadcast_to(gate_col, (sc, LANES))


def _topk_group(aff_t, tok_base, n, slot_base):
    E = N_EXPERTS
    M = aff_t.shape[1]
    nb = n // LANES
    cap = CAPACITY_FACTOR * n // E
    assert n % LANES == 0 and nb % 8 == 0 and cap % SLOT_CHUNK == 0 and tok_base % n == 0
    a3 = aff_t.reshape(E, M // LANES, LANES)
    blk = tok_base // n
    return pl.pallas_call(
        functools.partial(_topk_kernel, nb=nb, cap=cap, tok_base=tok_base, slot_base=slot_base),
        grid=(1,),
        in_specs=[pl.BlockSpec((E, nb, LANES), lambda i: (0, blk, 0))],
        out_specs=[pl.BlockSpec((E, cap), lambda i: (0, 0)),
                   pl.BlockSpec((E, cap, LANES), lambda i: (0, 0, 0)),
                   pl.BlockSpec((E, nb, LANES), lambda i: (0, 0, 0)),
                   pl.BlockSpec((E, nb, LANES), lambda i: (0, 0, 0))],
        out_shape=[jax.ShapeDtypeStruct((E, cap), jnp.int32),
                   jax.ShapeDtypeStruct((E, cap, LANES), F32),
                   jax.ShapeDtypeStruct((E, nb, LANES), F32),
                   jax.ShapeDtypeStruct((E, nb, LANES), F32)],
        scratch_shapes=[pltpu.VMEM((E, nb, LANES), F32)] * 3,
        compiler_params=_cparams(("arbitrary",)),
        name=f"expert_choice_topk_{n}",
    )(a3)


def _ffn_kernel(idx_ref, h_hbm, gate_ref, wg_hbm, wu_hbm, wd_hbm, ye_ref, xbuf, x_scr, wg_buf, wu_buf, wd_buf,
                sem, wsem, *, layer, tiles_per_expert, n_steps):
    e = pl.program_id(0)
    j = pl.program_id(1)
    step = e * tiles_per_expert + j
    slot = step % 2
    wslot = e % 2

    def weight_copies(expert, ws):
        return (pltpu.make_async_copy(wg_hbm.at[layer, expert], wg_buf.at[ws], wsem.at[ws, 0]),
                pltpu.make_async_copy(wu_hbm.at[layer, expert], wu_buf.at[ws], wsem.at[ws, 1]),
                pltpu.make_async_copy(wd_hbm.at[layer, expert], wd_buf.at[ws], wsem.at[ws, 2]))

    def row_copy(tile, k, slot_):
        t = idx_ref[tile * FFN_TILE + k]
        return pltpu.make_async_copy(h_hbm.at[pl.ds(t, 1), :], xbuf.at[slot_, pl.ds(k, 1), :],
                                     sem.at[slot_])

    @pl.when(step == 0)
    def _():
        for cp in weight_copies(0, 0):
            cp.start()

        def issue(k, carry):
            row_copy(0, k, 0).start()
            return carry
        lax.fori_loop(0, FFN_TILE, issue, 0, unroll=8)

    @pl.when(j == 0)
    def _():
        for cp in weight_copies(e, wslot):
            cp.wait()

        @pl.when(e + 1 < pl.num_programs(0))
        def _():
            for cp in weight_copies(e + 1, 1 - wslot):
                cp.start()

    for k in range(FFN_TILE):
        row_copy(step, k, slot).wait()
    nxt = jnp.minimum(step + 1, n_steps - 1)

    def compute(ws):
        x_scr[...] = xbuf[slot].astype(BF16)
        for k in range(FFN_TILE):
            row_copy(nxt, k, 1 - slot).start()
        acc = jnp.zeros((FFN_TILE, D), F32)
        for c in range(D_EXPERT // F_CHUNK):
            cs = slice(c * F_CHUNK, (c + 1) * F_CHUNK)
            hg = _dot(x_scr[...], wg_buf[ws, :, cs])
            hu = _dot(x_scr[...], wu_buf[ws, :, cs])
            act = (hg * _sigmoid(hg) * hu).astype(BF16)
            acc = acc + _dot(act, wd_buf[ws, cs, :])
        gate = gate_ref[...]
        for c in range(D // LANES):
            cs = slice(c * LANES, (c + 1) * LANES)
            ye_ref[:, cs] = (acc[:, cs] * gate).astype(BF16)

    for ws in range(2):
        @pl.when(wslot == ws)
        def _():
            compute(ws)

    @pl.when(step == n_steps - 1)
    def _():
        for k in range(FFN_TILE):
            row_copy(nxt, k, 1 - slot).wait()


def _expert_ffn(idx_flat, h2, gate_all, wg_bf, wu_bf, wd_bf, layer, slots):
    E = N_EXPERTS
    tiles = slots // FFN_TILE
    grid_spec = pltpu.PrefetchScalarGridSpec(
        num_scalar_prefetch=1,
        grid=(E, tiles),
        in_specs=[
            pl.BlockSpec(memory_space=pl.ANY),
            pl.BlockSpec((FFN_TILE, LANES), lambda e, j, idx: (e * tiles + j, 0)),
            pl.BlockSpec(memory_space=pl.ANY),
            pl.BlockSpec(memory_space=pl.ANY),
            pl.BlockSpec(memory_space=pl.ANY),
        ],
        out_specs=pl.BlockSpec((None, FFN_TILE, D), lambda e, j, idx: (e, j, 0)),
        scratch_shapes=[pltpu.VMEM((2, FFN_TILE, D), F32), pltpu.VMEM((FFN_TILE, D), BF16),
                        pltpu.VMEM((2, D, D_EXPERT), BF16), pltpu.VMEM((2, D, D_EXPERT), BF16),
                        pltpu.VMEM((2, D_EXPERT, D), BF16),
                        pltpu.SemaphoreType.DMA((2,)), pltpu.SemaphoreType.DMA((2, 3))],
    )
    return pl.pallas_call(
        functools.partial(_ffn_kernel, layer=layer, tiles_per_expert=tiles, n_steps=E * tiles),
        grid_spec=grid_spec,
        out_shape=jax.ShapeDtypeStruct((E, slots, D), BF16),
        compiler_params=_cparams(("arbitrary", "arbitrary")),
        name="expert_ffn",
    )(idx_flat, h2, gate_all.reshape(E * slots, LANES), wg_bf, wu_bf, wd_bf)


def _combine_kernel(tb_ref, bst_ref, x_ref, pos_ref, mod_ref, gfin_ref, ye_hbm, *rest, slots, blocks,
                    n_tiles, final_norm, split):
    if split is None:
        (o_ref, win, winx, acc, sem, semx) = rest
    else:
        (o_ref, o2_ref, win, winx, acc, sem, semx) = rest
    E = N_EXPERTS
    i = pl.program_id(0)
    bpt = TM_COMB // LANES
    stride = blocks + 1

    def tile_start(tile, e):
        return (bst_ref[e * stride + tile * bpt] // 16) * 16

    def window_copy(e, lo, buf, sem_):
        w0 = pl.multiple_of(jnp.minimum(lo, slots - WIN), 16)
        return pltpu.make_async_copy(ye_hbm.at[e, pl.ds(w0, WIN), :], buf.at[pl.ds(e * WIN, WIN), :],
                                     sem_.at[e])

    slot = i % 2

    @pl.when(i == 0)
    def _():
        for e in range(E):
            window_copy(e, tile_start(0, e), win.at[0], sem.at[0]).start()

    starts = [tile_start(i, e) for e in range(E)]
    npass = jnp.int32(1)
    for e in range(E):
        end = bst_ref[e * stride + (i + 1) * bpt]
        npass = jnp.maximum(npass, (end - starts[e] + (WIN - 1)) // WIN)

    for e in range(E):
        window_copy(e, starts[e], win.at[slot], sem.at[slot]).wait()
    nxt = jnp.minimum(i + 1, n_tiles - 1)
    for e in range(E):
        window_copy(e, tile_start(nxt, e), win.at[1 - slot], sem.at[1 - slot]).start()

    lane_e = lax.broadcasted_iota(jnp.int32, (1, E), 1)
    erow = lax.broadcasted_iota(jnp.int32, (E, E * WIN), 0)
    ecol = lax.broadcasted_iota(jnp.int32, (E, E * WIN), 1)
    expand = jnp.where(ecol // WIN == erow, 1.0, 0.0).astype(BF16)
    sprime = (lax.broadcasted_iota(jnp.int32, (1, E * WIN), 1) % WIN).astype(F32)
    pos = pos_ref[...]

    def scatter_rows(p, wbuf):
        lo_vec = jnp.zeros((1, E), F32)
        w0_vec = jnp.zeros((1, E), F32)
        for e in range(E):
            lo = starts[e] + p * WIN
            w0 = jnp.minimum(lo, slots - WIN)
            lo_vec = jnp.where(lane_e == e, lo.astype(F32), lo_vec)
            w0_vec = jnp.where(lane_e == e, w0.astype(F32), w0_vec)
        rel = jnp.where(pos >= lo_vec, pos - w0_vec, -1.0)
        rel = jnp.clip(rel, -1.0, np.float32(WIN))
        relx = _dot(rel.astype(BF16), expand)
        onehot = jnp.where(relx == sprime, 1.0, 0.0).astype(BF16)
        return _dot(onehot, wbuf[...])

    acc[...] = scatter_rows(0, win.at[slot])

    def extra_pass(p, carry):
        copies = [window_copy(e, starts[e] + p * WIN, winx, semx) for e in range(E)]
        for cp in copies:
            cp.start()
        for cp in copies:
            cp.wait()
        acc[...] += scatter_rows(p, winx)
        return carry

    lax.fori_loop(1, npass, extra_pass, 0)

    m = mod_ref[...]
    y = x_ref[...] + m[5:6] * acc[...]
    if final_norm:
        ms = jnp.mean(y * y, axis=-1, keepdims=True)
        y = y * lax.rsqrt(ms + NORM_EPS) * gfin_ref[...]
    if split is None:
        o_ref[...] = y
    else:
        @pl.when(i < split)
        def _():
            o_ref[...] = y

        @pl.when(i >= split)
        def _():
            o2_ref[...] = y

    @pl.when(i == n_tiles - 1)
    def _():
        for e in range(E):
            window_copy(e, tile_start(nxt, e), win.at[1 - slot], sem.at[1 - slot]).wait()


def _combine(x1, pos_t, mod, g_final, ye, bst_flat, groups, final_norm):
    M = x1.shape[0]
    tm = TM_COMB
    E = N_EXPERTS
    slots = ye.shape[1]
    tb, _ = _tile_tables(groups, tm)
    n_tiles = M // tm
    n1 = groups[0][0] * groups[0][1]
    if final_norm:
        split = n1 // tm
        out_specs = [pl.BlockSpec((tm, D), lambda i, tb, bst: (jnp.minimum(i, split - 1), 0)),
                     pl.BlockSpec((tm, D), lambda i, tb, bst: (jnp.maximum(i - split, 0), 0))]
        out_shape = [jax.ShapeDtypeStruct((n1, D), F32), jax.ShapeDtypeStruct((M - n1, D), F32)]
    else:
        split = None
        out_specs = pl.BlockSpec((tm, D), lambda i, tb, bst: (i, 0))
        out_shape = jax.ShapeDtypeStruct((M, D), F32)
    grid_spec = pltpu.PrefetchScalarGridSpec(
        num_scalar_prefetch=2,
        grid=(n_tiles,),
        in_specs=[
            pl.BlockSpec((tm, D), lambda i, tb, bst: (i, 0)),
            pl.BlockSpec((tm, E), lambda i, tb, bst: (i, 0)),
            pl.BlockSpec((None, 6, D), lambda i, tb, bst: (tb[i], 0, 0)),
            pl.BlockSpec((1, D), lambda i, tb, bst: (0, 0)),
            pl.BlockSpec(memory_space=pl.ANY),
        ],
        out_specs=out_specs,
        scratch_shapes=[pltpu.VMEM((2, E * WIN, D), BF16), pltpu.VMEM((E * WIN, D), BF16),
                        pltpu.VMEM((tm, D), F32), pltpu.SemaphoreType.DMA((2, E)),
                        pltpu.SemaphoreType.DMA((E,))],
    )
    return pl.pallas_call(
        functools.partial(_combine_kernel, slots=slots, blocks=M // LANES, n_tiles=n_tiles,
                          final_norm=final_norm, split=split),
        grid_spec=grid_spec,
        out_shape=out_shape,
        compiler_params=_cparams(("arbitrary",)),
        name="moe_combine",
    )(tb, bst_flat, x1, pos_t, mod, g_final, ye)


def _moe(x1, h2, aff_t, mod, g_final, wg_bf, wu_bf, wd_bf, layer, groups, final_norm):
    E = N_EXPERTS
    idxs, gates, poss, bsts = [], [], [], []
    tok_base = 0
    slot_base = 0
    for (B, S) in groups:
        n = B * S
        idx, gate, pos, bst = _topk_group(aff_t, tok_base, n, slot_base)
        idxs.append(idx)
        gates.append(gate)
        poss.append(pos.reshape(E, n))
        bsts.append(bst[:, :, 0])
        tok_base += n
        slot_base += CAPACITY_FACTOR * n // E
    slots = slot_base
    idx_flat = jnp.concatenate(idxs, axis=1).reshape(-1)
    ye = _expert_ffn(idx_flat, h2, jnp.concatenate(gates, axis=1), wg_bf, wu_bf, wd_bf, layer, slots)
    pos_t = jnp.concatenate(poss, axis=1).T
    bst = jnp.concatenate(bsts + [jnp.full((E, 1), slots, F32)], axis=1)
    bst_flat = bst.astype(jnp.int32).reshape(-1)
    return _combine(x1, pos_t, mod, g_final, ye, bst_flat, groups, final_norm)


def kernel(x_prompt, x_sample, c_prompt, c_sample, w_mod, b_mod, g_mix, g_ffn, a_w_in, a_w_out, b_w_in,
           b_ln_g, b_ln_b, b_w_s, b_b_s, b_w_out, moe_w_router, moe_w_gate, moe_w_up, moe_w_down, g_final):
    groups = (x_prompt.shape[:2], x_sample.shape[:2])
    assert x_prompt.shape[2] == D and x_sample.shape[2] == D
    xa, xb = x_prompt.reshape(-1, D), x_sample.reshape(-1, D)
    n_seq = groups[0][0] + groups[1][0]
    rows = -(-n_seq // 8) * 8
    c_all = jnp.concatenate([c_prompt, c_sample, jnp.zeros((rows - n_seq, D), F32)], axis=0)
    mod = _modulation(c_all, w_mod, b_mod).reshape(w_mod.shape[0], rows, 6, D)
    cos_t, sin_t = _rope_tables(max(groups[0][1], groups[1][1]))
    wr_t = jnp.swapaxes(moe_w_router, 1, 2)
    gfin = g_final.reshape(1, D)

    os_, ls_ = [], []
    for g, r in enumerate(DILATIONS):
        w_g = _pair_heads(a_w_in[0][:, g * 3 * D:(g + 1) * 3 * D]).astype(BF16)
        qkv_g = _qkv_proj(xa, xb, mod[0], g_mix[0].reshape(1, D), w_g, _permute_table(cos_t, TM_PROJ, r),
                          _permute_table(sin_t, TM_PROJ, r), groups, r)
        o_g, l_g = _attention_group(qkv_g, groups)
        os_.append(o_g)
        ls_.append(l_g)
    x1, h2, aff_t = _out_proj(os_, ls_, xa, xb, mod[0], a_w_out[0].astype(BF16), g_ffn[0].reshape(1, D),
                              wr_t[0], groups)
    wg_bf, wu_bf, wd_bf = moe_w_gate.astype(BF16), moe_w_up.astype(BF16), moe_w_down.astype(BF16)
    x = _moe(x1, h2, aff_t, mod[0], gfin, wg_bf, wu_bf, wd_bf, 0, groups, final_norm=False)

    x1, h2, aff_t = _sgu_layer(x, mod[1], g_mix[1].reshape(1, D), b_w_in[0].astype(BF16),
                               b_ln_g[0].reshape(1, SGU_HALF), b_ln_b[0].reshape(1, SGU_HALF),
                               b_w_s[0].astype(BF16), b_b_s[0].reshape(SGU_HEADS, CHUNK, 1),
                               b_w_out[0].astype(BF16), g_ffn[1].reshape(1, D), wr_t[1], groups)
    y1, y2 = _moe(x1, h2, aff_t, mod[1], gfin, wg_bf, wu_bf, wd_bf, 1, groups, final_norm=True)

    return (y1.reshape(x_prompt.shape), y2.reshape(x_sample.shape))
```

```python
import functools

import numpy as np
import jax
import jax.numpy as jnp
from jax import lax
from jax.experimental import pallas as pl
from jax.experimental.pallas import tpu as pltpu

F32 = jnp.float32
BF16 = jnp.bfloat16

D = 1024
N_HEADS = 16
HEAD_DIM = 64
DILATIONS = (1, 4, 16)
WINDOWS = (128, 512, 2048)
RADIUS = 64
N_GROUPS = 3
QKV_COLS = N_GROUPS * 3 * D
ROPE_THETA = 10000.0
CHUNK = 128
SGU_HALF = 3 * D
SGU_HEADS = 16
SGU_HEAD_CH = SGU_HALF // SGU_HEADS
N_EXPERTS = 16
CAPACITY_FACTOR = 2
D_EXPERT = 2 * D
NORM_EPS = 1e-6
NEG_BIG = -1e30

LANES = 128
VMEM_LIMIT = 56 * 1024 * 1024

TM_PROJ = 512
TM_SGU = 256
TM_COMB = 256
TQ = 256
UQ = 128
FFN_TILE = 256
F_CHUNK = 1024
WIN = 64
SLOT_CHUNK = 512


def _cparams(sem):
    return pltpu.CompilerParams(dimension_semantics=sem, vmem_limit_bytes=VMEM_LIMIT)


def _split2(x):
    hi = x.astype(BF16)
    lo = (x - hi.astype(F32)).astype(BF16)
    return hi, lo


def _split3(x):
    hi = x.astype(BF16)
    r1 = x - hi.astype(F32)
    mid = r1.astype(BF16)
    lo = (r1 - mid.astype(F32)).astype(BF16)
    return hi, mid, lo


def _dot(a, b):
    return jnp.dot(a, b, preferred_element_type=F32)


def _dot_nt(a, b):
    return lax.dot_general(a, b, (((1,), (1,)), ((), ())), preferred_element_type=F32)


def _dot3(a, b):
    ah, al = _split2(a)
    bh, bl = _split2(b)
    return _dot(ah, bh) + (_dot(ah, bl) + _dot(al, bh))


def _dot3_nt(a, b):
    ah, al = _split2(a)
    bh, bl = _split2(b)
    return _dot_nt(ah, bh) + (_dot_nt(ah, bl) + _dot_nt(al, bh))


def _sigmoid(x):
    return 1.0 / (1.0 + jnp.exp(-x))


def _gelu_tanh(x):
    c = np.float32(np.sqrt(2.0 / np.pi))
    return 0.5 * x * (1.0 + jnp.tanh(c * (x + np.float32(0.044715) * (x * x * x))))


def _norm_mod(x, g, shift, scale):
    ms = jnp.mean(x * x, axis=-1, keepdims=True)
    return (x * lax.rsqrt(ms + NORM_EPS) * g) * (1.0 + scale) + shift


def _tile_tables(groups, tm):
    tb, tp = [], []
    b0 = 0
    for (B, S) in groups:
        assert S % tm == 0
        for b in range(B):
            for k in range(S // tm):
                tb.append(b0 + b)
                tp.append(k)
        b0 += B
    return jnp.asarray(np.array(tb, np.int32)), jnp.asarray(np.array(tp, np.int32))


def _mod_kernel(c_ref, w_ref, b_ref, o_ref):
    c = c_ref[...]
    o_ref[...] = _dot3(c * _sigmoid(c), w_ref[...]) + b_ref[...]


def _modulation(c_pad, w_mod, b_mod):
    depth, _, n = w_mod.shape
    tn = 2048
    rows = c_pad.shape[0]
    return pl.pallas_call(
        _mod_kernel,
        grid=(depth, n // tn),
        in_specs=[
            pl.BlockSpec((rows, D), lambda l, j: (0, 0)),
            pl.BlockSpec((None, D, tn), lambda l, j: (l, 0, j)),
            pl.BlockSpec((None, 1, tn), lambda l, j: (l, 0, j)),
        ],
        out_specs=pl.BlockSpec((None, rows, tn), lambda l, j: (l, 0, j)),
        out_shape=jax.ShapeDtypeStruct((depth, rows, n), F32),
        compiler_params=_cparams(("parallel", "parallel")),
        name="modulation",
    )(c_pad, w_mod, b_mod.reshape(depth, 1, n))


def _permute_rows(src, slab, dst, r):
    tm, cols = src.shape
    n = tm // r
    for c in range(cols // LANES):
        slab[c] = src[:, c * LANES:(c + 1) * LANES]
    for res in range(r):
        for c in range(cols // LANES):
            dst[res * n:(res + 1) * n, c * LANES:(c + 1) * LANES] = slab[c, pl.ds(res, n, stride=r), :]


def _two_group_specs(tm, n_first_tiles):
    first = pl.BlockSpec((tm, D), lambda i, *_: (jnp.minimum(i, n_first_tiles - 1), 0))
    second = pl.BlockSpec((tm, D), lambda i, *_: (jnp.maximum(i - n_first_tiles, 0), 0))
    return [first, second]


def _qkv_kernel(tb_ref, tp_ref, xa_ref, xb_ref, mod_ref, g_ref, w_ref, cos_ref, sin_ref, o_ref, h_scr, *scr,
                r, n_first_tiles):
    m = mod_ref[...]
    x = jnp.where(pl.program_id(0) < n_first_tiles, xa_ref[...], xb_ref[...])
    if r > 1:
        slab, xp = scr
        _permute_rows(x, slab, xp, r)
        x = xp[...]
    h_scr[...] = _norm_mod(x, g_ref[...], m[0:1], m[1:2]).astype(BF16)

    n = h_scr.shape[0] // r
    nw = 2 * LANES
    cos = cos_ref[...]
    sin = sin_ref[...]
    qscale = np.float32(HEAD_DIM ** -0.5)
    tables = ((cos * qscale, sin * qscale), (cos, sin), None)
    for sec, table in enumerate(tables):
        for cw in range(D // nw):
            col0 = sec * D + cw * nw
            acc = _dot(h_scr[...], w_ref[:, col0:col0 + nw])
            for c in range(nw // LANES):
                xc = acc[:, c * LANES:(c + 1) * LANES]
                if table is not None:
                    xc = xc * table[0] + pltpu.roll(xc, LANES // 2, 1) * table[1]
                cs = slice(col0 + c * LANES, col0 + (c + 1) * LANES)
                o_ref[:, :, cs] = xc.reshape(r, n, LANES).astype(BF16)


def _qkv_proj(xa, xb, mod, g, w_bf, cos_p, sin_p, groups, r):
    M = xa.shape[0] + xb.shape[0]
    tm = TM_PROJ
    n_first_tiles = xa.shape[0] // tm
    tb, tp = _tile_tables(groups, tm)
    scratch = [pltpu.VMEM((tm, D), BF16)]
    if r > 1:
        scratch += [pltpu.VMEM((D // LANES, tm, LANES), F32), pltpu.VMEM((tm, D), F32)]
    grid_spec = pltpu.PrefetchScalarGridSpec(
        num_scalar_prefetch=2,
        grid=(M // tm,),
        in_specs=_two_group_specs(tm, n_first_tiles) + [
            pl.BlockSpec((None, 6, D), lambda i, tb, tp: (tb[i], 0, 0)),
            pl.BlockSpec((1, D), lambda i, tb, tp: (0, 0)),
            pl.BlockSpec((D, 3 * D), lambda i, tb, tp: (0, 0), pipeline_mode=pl.Buffered(1)),
            pl.BlockSpec((tm, LANES), lambda i, tb, tp: (tp[i], 0)),
            pl.BlockSpec((tm, LANES), lambda i, tb, tp: (tp[i], 0)),
        ],
        out_specs=pl.BlockSpec((r, tm // r, 3 * D), lambda i, tb, tp: (0, i, 0)),
        scratch_shapes=scratch,
    )
    return pl.pallas_call(
        functools.partial(_qkv_kernel, r=r, n_first_tiles=n_first_tiles),
        grid_spec=grid_spec,
        out_shape=jax.ShapeDtypeStruct((r, M // r, 3 * D), BF16),
        compiler_params=_cparams(("parallel",)),
        name=f"qkv_rope_d{r}",
    )(tb, tp, xa, xb, mod, g, w_bf, cos_p, sin_p)


def _rope_tables(s_max):
    half = HEAD_DIM // 2
    inv = ROPE_THETA ** (-jnp.arange(0, HEAD_DIM, 2, dtype=F32) / HEAD_DIM)
    ang = jnp.arange(s_max, dtype=F32)[:, None] * inv[None, :]
    cos, sin = jnp.cos(ang), jnp.sin(ang)
    cos_t = jnp.tile(cos, (1, LANES // half))
    sin_t = jnp.concatenate([-sin, -sin, sin, sin], axis=1)
    return cos_t, sin_t


def _pair_heads(w):
    half = HEAD_DIM // 2
    perm = np.concatenate([np.arange(0, half), np.arange(HEAD_DIM, HEAD_DIM + half),
                           np.arange(half, HEAD_DIM), np.arange(HEAD_DIM + half, LANES)])
    k = w.shape[0]
    qk = w[:, :2 * D].reshape(k, 2 * D // LANES, LANES)[:, :, perm].reshape(k, 2 * D)
    return jnp.concatenate([qk, w[:, 2 * D:]], axis=1)


def _permute_table(t, tm, r):
    s = t.shape[0]
    return t.reshape(s // tm, tm // r, r, LANES).transpose(0, 2, 1, 3).reshape(s, LANES)


def _attn_kernel(q_ref, kp_ref, kc_ref, kn_ref, vp_ref, vc_ref, vn_ref, o_ref, l_ref,
                 kw, vw, s_scr, p_scr, *, seq_blocks):
    i = pl.program_id(1)
    (n1, nb1), (n2, nb2) = seq_blocks
    first_grp = i < n1 * nb1
    pos = jnp.where(first_grp, i % nb1, (i - n1 * nb1) % nb2)
    nb = jnp.where(first_grp, nb1, nb2)
    has_prev = pos > 0
    has_next = pos < nb - 1

    W = RADIUS
    kw[0:W] = kp_ref[...]
    kw[W:W + TQ] = kc_ref[...]
    kw[W + TQ:W + TQ + W] = kn_ref[...]
    vw[0:W] = vp_ref[...]
    vw[W:W + TQ] = vc_ref[...]
    vw[W + TQ:W + TQ + W] = vn_ref[...]

    lane = lax.broadcasted_iota(jnp.int32, (UQ, LANES), 1)
    low_half = lane < HEAD_DIM
    head_a = (lane % HEAD_DIM) < (HEAD_DIM // 2)
    nu = TQ // UQ
    for u in range(nu):
        for hp in range(N_HEADS // 2):
            cs = slice(hp * LANES, (hp + 1) * LANES)
            q2 = q_ref[u * UQ:(u + 1) * UQ, cs]
            kwin = kw[u * UQ:u * UQ + 2 * UQ, cs]
            zero = jnp.zeros_like(q2)
            s_scr[u, 2 * hp] = _dot_nt(jnp.where(head_a, q2, zero), kwin)
            s_scr[u, 2 * hp + 1] = _dot_nt(jnp.where(head_a, zero, q2), kwin)

    rr = lax.broadcasted_iota(jnp.int32, (UQ, 2 * UQ), 0)
    cc = lax.broadcasted_iota(jnp.int32, (UQ, 2 * UQ), 1)
    band = jnp.abs((cc - W) - rr) <= RADIUS
    l_ref[...] = jnp.zeros(l_ref.shape, F32)
    for u in range(nu):
        valid = band
        if u == 0:
            valid = valid & ((cc >= W) | has_prev)
        if u == nu - 1:
            valid = valid & ((cc < 2 * UQ - W) | has_next)
        s = jnp.where(valid[None], s_scr[u], NEG_BIG)
        m = jnp.max(s, axis=-1, keepdims=True)
        p = jnp.exp(s - m)
        den = jnp.sum(p, axis=-1, keepdims=True)
        p_scr[u] = p.astype(BF16)
        rden = 1.0 / den
        lse = m + jnp.log(den)
        rows = slice(u * UQ, (u + 1) * UQ)
        for hp in range(N_HEADS // 2):
            cs = slice(hp * LANES, (hp + 1) * LANES)
            vwin = vw[u * UQ:u * UQ + 2 * UQ, cs]
            oa = _dot(p_scr[u, 2 * hp], vwin) * rden[2 * hp]
            ob = _dot(p_scr[u, 2 * hp + 1], vwin) * rden[2 * hp + 1]
            o_ref[rows, cs] = jnp.where(low_half, oa, ob).astype(BF16)
        for h in range(N_HEADS):
            l_ref[rows, h:h + 1] = lse[h]


def _attention_group(qkv_g, groups):
    r, rows, _ = qkv_g.shape
    W = RADIUS
    for (B, S) in groups:
        assert S % (r * TQ) == 0
    seq_blocks = tuple((B, S // r // TQ) for (B, S) in groups)
    q64 = TQ // W
    last64 = rows // W - 1

    def big(col):
        return pl.BlockSpec((None, TQ, D), lambda res, i: (res, i, col))

    def prv(col):
        return pl.BlockSpec((None, W, D), lambda res, i: (res, jnp.maximum(i * q64 - 1, 0), col))

    def nxt(col):
        return pl.BlockSpec((None, W, D), lambda res, i: (res, jnp.minimum((i + 1) * q64, last64), col))

    return pl.pallas_call(
        functools.partial(_attn_kernel, seq_blocks=seq_blocks),
        grid=(r, rows // TQ),
        in_specs=[big(0), prv(1), big(1), nxt(1), prv(2), big(2), nxt(2)],
        out_specs=[pl.BlockSpec((None, TQ, D), lambda res, i: (res, i, 0)),
                   pl.BlockSpec((None, TQ, LANES), lambda res, i: (res, i, 0))],
        out_shape=[jax.ShapeDtypeStruct((r, rows, D), BF16),
                   jax.ShapeDtypeStruct((r, rows, LANES), F32)],
        scratch_shapes=[pltpu.VMEM((TQ + 2 * W, D), BF16), pltpu.VMEM((TQ + 2 * W, D), BF16),
                        pltpu.VMEM((TQ // UQ, N_HEADS, UQ, 2 * UQ), F32),
                        pltpu.VMEM((TQ // UQ, N_HEADS, UQ, 2 * UQ), BF16)],
        compiler_params=_cparams(("parallel", "parallel")),
        name=f"attn_d{r}",
    )(*([qkv_g] * 7))


def _router(x1, m, g_ffn, wr_t):
    h2 = _norm_mod(x1, g_ffn, m[3:4], m[4:5])
    logits_t = _dot3_nt(wr_t, h2)
    mx = jnp.max(logits_t, axis=0, keepdims=True)
    e = jnp.exp(logits_t - mx)
    return h2, e / jnp.sum(e, axis=0, keepdims=True)


def _oproj_kernel(tb_ref, o1_ref, o4_ref, o16_ref, l1_ref, l4_ref, l16_ref, xa_ref, xb_ref, mod_ref, w_ref,
                  gf_ref, wr_ref, x1_ref, h2_ref, aff_ref, oslab, lslab, om_scr, *, n_first_tiles):
    tm = xa_ref.shape[0]
    nc = D // LANES
    for g, (r, o_ref, l_ref) in enumerate(zip(DILATIONS, (o1_ref, o4_ref, o16_ref),
                                              (l1_ref, l4_ref, l16_ref))):
        n = tm // r
        for res in range(r):
            rows = slice(None) if r == 1 else pl.ds(res, n, stride=r)
            lslab[g, 0, rows, :] = l_ref[res]
            for c in range(nc):
                oslab[g, c, rows, :] = o_ref[res, :, c * LANES:(c + 1) * LANES].astype(F32)
    ls = [lslab[g, 0] for g in range(N_GROUPS)]
    mx = jnp.maximum(jnp.maximum(ls[0], ls[1]), ls[2])
    es = [jnp.exp(l - mx) for l in ls]
    tot = es[0] + es[1] + es[2]
    hrow = lax.broadcasted_iota(jnp.int32, (LANES, D), 0)
    hcol = lax.broadcasted_iota(jnp.int32, (LANES, D), 1)
    expand = jnp.where((hrow < 2 * N_HEADS) & (hcol // HEAD_DIM == hrow % N_HEADS), 1.0, 0.0).astype(BF16)
    lane = lax.broadcasted_iota(jnp.int32, (tm, LANES), 1)
    wexp = []
    for g in range(N_GROUPS):
        w = es[g] / tot
        hi = w.astype(BF16).astype(F32)
        packed = jnp.where(lane < N_HEADS, hi, pltpu.roll(w - hi, N_HEADS, 1))
        wexp.append(_dot(packed.astype(BF16), expand))
    for c in range(nc):
        cs = slice(c * LANES, (c + 1) * LANES)
        om = wexp[0][:, cs] * oslab[0, c] + wexp[1][:, cs] * oslab[1, c] + wexp[2][:, cs] * oslab[2, c]
        om_scr[:, cs] = om.astype(BF16)
    m = mod_ref[...]
    y = _dot(om_scr[...], w_ref[...])
    x = jnp.where(pl.program_id(0) < n_first_tiles, xa_ref[...], xb_ref[...])
    x1 = x + m[2:3] * y
    x1_ref[...] = x1
    h2, aff_t = _router(x1, m, gf_ref[...], wr_ref[...])
    h2_ref[...] = h2
    aff_ref[...] = aff_t


def _out_proj(os_, ls_, xa, xb, mod, w_bf, g_ffn, wr_t, groups):
    M = xa.shape[0] + xb.shape[0]
    tm = TM_PROJ
    n_first_tiles = xa.shape[0] // tm
    tb, _ = _tile_tables(groups, tm)
    o_specs = [pl.BlockSpec((r, tm // r, D), lambda i, tb: (0, i, 0)) for r in DILATIONS]
    l_specs = [pl.BlockSpec((r, tm // r, LANES), lambda i, tb: (0, i, 0)) for r in DILATIONS]
    grid_spec = pltpu.PrefetchScalarGridSpec(
        num_scalar_prefetch=1,
        grid=(M // tm,),
        in_specs=o_specs + l_specs + _two_group_specs(tm, n_first_tiles) + [
            pl.BlockSpec((None, 6, D), lambda i, tb: (tb[i], 0, 0)),
            pl.BlockSpec((D, D), lambda i, tb: (0, 0)),
            pl.BlockSpec((1, D), lambda i, tb: (0, 0)),
            pl.BlockSpec((N_EXPERTS, D), lambda i, tb: (0, 0)),
        ],
        out_specs=[
            pl.BlockSpec((tm, D), lambda i, tb: (i, 0)),
            pl.BlockSpec((tm, D), lambda i, tb: (i, 0)),
            pl.BlockSpec((N_EXPERTS, tm), lambda i, tb: (0, i)),
        ],
        scratch_shapes=[pltpu.VMEM((N_GROUPS, D // LANES, tm, LANES), F32),
                        pltpu.VMEM((N_GROUPS, 1, tm, LANES), F32),
                        pltpu.VMEM((tm, D), BF16)],
    )
    return pl.pallas_call(
        functools.partial(_oproj_kernel, n_first_tiles=n_first_tiles),
        grid_spec=grid_spec,
        out_shape=[jax.ShapeDtypeStruct((M, D), F32), jax.ShapeDtypeStruct((M, D), F32),
                   jax.ShapeDtypeStruct((N_EXPERTS, M), F32)],
        compiler_params=_cparams(("parallel",)),
        name="attn_out_router",
    )(tb, *os_, *ls_, xa, xb, mod, w_bf, g_ffn, wr_t)


def _sgu_kernel(tb_ref, x_ref, mod_ref, g_ref, win_ref, lng_ref, lnb_ref, ws_ref, bs_ref, wout_ref,
                gf_ref, wr_ref, x1_ref, h2_ref, aff_ref, h_scr, v_scr, vn_scr, u_scr, g_scr):
    m = mod_ref[...]
    h_scr[...] = _norm_mod(x_ref[...], g_ref[...], m[0:1], m[1:2]).astype(BF16)
    tm = x_ref.shape[0]
    nc = 2 * LANES
    nchunks = SGU_HALF // nc
    s1 = jnp.zeros((tm, 1), F32)
    s2 = jnp.zeros((tm, 1), F32)
    for c in range(nchunks):
        z = _gelu_tanh(_dot(h_scr[...], win_ref[:, SGU_HALF + c * nc:SGU_HALF + (c + 1) * nc]))
        v_scr[:, c * nc:(c + 1) * nc] = z
        s1 = s1 + jnp.sum(z, axis=-1, keepdims=True)
        s2 = s2 + jnp.sum(z * z, axis=-1, keepdims=True)
    mu = s1 * np.float32(1.0 / SGU_HALF)
    var = s2 * np.float32(1.0 / SGU_HALF) - mu * mu
    rstd = lax.rsqrt(var + NORM_EPS)
    for c in range(nchunks):
        cs = slice(c * nc, (c + 1) * nc)
        vn_scr[:, cs] = ((v_scr[:, cs] - mu) * rstd * lng_ref[:, cs] + lnb_ref[:, cs]).astype(BF16)
        u_scr[:, cs] = _gelu_tanh(_dot(h_scr[...], win_ref[:, cs])).astype(BF16)
    for hd in range(SGU_HEADS):
        cs = slice(hd * SGU_HEAD_CH, (hd + 1) * SGU_HEAD_CH)
        for k in range(tm // CHUNK):
            rs = slice(k * CHUNK, (k + 1) * CHUNK)
            v_scr[rs, cs] = _dot(ws_ref[hd], vn_scr[rs, cs]) + bs_ref[hd]
    for c in range(nchunks):
        cs = slice(c * nc, (c + 1) * nc)
        g_scr[:, cs] = (u_scr[:, cs].astype(F32) * v_scr[:, cs]).astype(BF16)
    y = _dot(g_scr[...], wout_ref[...])
    x1 = x_ref[...] + m[2:3] * y
    x1_ref[...] = x1
    h2, aff_t = _router(x1, m, gf_ref[...], wr_ref[...])
    h2_ref[...] = h2
    aff_ref[...] = aff_t


def _sgu_layer(x, mod, g_mix, w_in_bf, ln_g, ln_b, w_s_bf, b_s_col, w_out_bf, g_ffn, wr_t, groups):
    M = x.shape[0]
    tm = TM_SGU
    tb, _ = _tile_tables(groups, tm)
    const = lambda *shape: pl.BlockSpec(shape, lambda i, tb: (0,) * len(shape),
                                        pipeline_mode=pl.Buffered(1))
    grid_spec = pltpu.PrefetchScalarGridSpec(
        num_scalar_prefetch=1,
        grid=(M // tm,),
        in_specs=[
            pl.BlockSpec((tm, D), lambda i, tb: (i, 0)),
            pl.BlockSpec((None, 6, D), lambda i, tb: (tb[i], 0, 0)),
            const(1, D),
            const(D, 2 * SGU_HALF),
            const(1, SGU_HALF),
            const(1, SGU_HALF),
            const(SGU_HEADS, CHUNK, CHUNK),
            const(SGU_HEADS, CHUNK, 1),
            const(SGU_HALF, D),
            const(1, D),
            const(N_EXPERTS, D),
        ],
        out_specs=[
            pl.BlockSpec((tm, D), lambda i, tb: (i, 0)),
            pl.BlockSpec((tm, D), lambda i, tb: (i, 0)),
            pl.BlockSpec((N_EXPERTS, tm), lambda i, tb: (0, i)),
        ],
        scratch_shapes=[pltpu.VMEM((tm, D), BF16), pltpu.VMEM((tm, SGU_HALF), F32),
                        pltpu.VMEM((tm, SGU_HALF), BF16), pltpu.VMEM((tm, SGU_HALF), BF16),
                        pltpu.VMEM((tm, SGU_HALF), BF16)],
    )
    return pl.pallas_call(
        _sgu_kernel,
        grid_spec=grid_spec,
        out_shape=[jax.ShapeDtypeStruct((M, D), F32), jax.ShapeDtypeStruct((M, D), F32),
                   jax.ShapeDtypeStruct((N_EXPERTS, M), F32)],
        compiler_params=_cparams(("parallel",)),
        name="sgu_router",
    )(tb, x, mod, g_mix, w_in_bf, ln_g, ln_b, w_s_bf, b_s_col, w_out_bf, g_ffn, wr_t)


def _topk_kernel(aff_ref, idx_ref, gate_ref, pos_ref, bst_ref, incl_scr, tot_scr, bex_scr, *, nb, cap,
                 tok_base, slot_base):
    E = N_EXPERTS
    a = aff_ref[...]
    bits = pltpu.bitcast(a, jnp.int32)

    def count(mask):
        c = jnp.sum(jnp.where(mask, 1.0, 0.0), axis=1, keepdims=True)
        return jnp.sum(c, axis=2, keepdims=True)

    def search(k, thr):
        cand = thr | jnp.left_shift(jnp.int32(1), 30 - k)
        return jnp.where(count(bits >= cand) >= cap, cand, thr)

    thr = lax.fori_loop(0, 31, search, jnp.zeros((E, 1, 1), jnp.int32))
    gt = bits > thr
    eq = bits == thr
    need = cap - count(gt)

    row = lax.broadcasted_iota(jnp.int32, (LANES, LANES), 0)
    col = lax.broadcasted_iota(jnp.int32, (LANES, LANES), 1)
    upper = jnp.where(row <= col, 1.0, 0.0).astype(BF16)
    ones = jnp.ones((LANES, LANES), BF16)
    brow = lax.broadcasted_iota(jnp.int32, (nb, nb), 0)
    bcol = lax.broadcasted_iota(jnp.int32, (nb, nb), 1)
    lower_strict = jnp.where(bcol < brow, 1.0, 0.0).astype(BF16)

    def prefix(mask):
        xb = jnp.where(mask, 1.0, 0.0).astype(BF16).reshape(E * nb, LANES)
        incl_scr[...] = _dot(xb, upper).reshape(E, nb, LANES)
        tot_scr[...] = _dot(xb, ones).reshape(E, nb, LANES)
        for e in range(E):
            bex_scr[e] = _dot(lower_strict, tot_scr[e].astype(BF16))

    prefix(eq)
    rank_eq = bex_scr[...] + incl_scr[...] - jnp.where(eq, 1.0, 0.0)
    sel = gt | (eq & (rank_eq < need))
    prefix(sel)
    self32 = jnp.where(sel, 1.0, 0.0)
    pos = bex_scr[...] + incl_scr[...] - self32
    pos_ref[...] = jnp.where(sel, pos + np.float32(slot_base), -1.0)
    bst_ref[...] = bex_scr[...] + np.float32(slot_base)

    sc = SLOT_CHUNK
    bidx = lax.broadcasted_iota(jnp.int32, (nb, sc), 0).astype(F32)
    jidx = lax.broadcasted_iota(jnp.int32, (LANES, sc), 0).astype(F32)
    for e in range(E):
        tot_e = tot_scr[e][:, 0:1]
        binc_e = bex_scr[e][:, 0:1] + tot_e
        incl_t = jnp.transpose(incl_scr[e]).astype(BF16)
        aff_parts = _split3(jnp.transpose(a[e]))
        for c in range(cap // sc):
            s = (lax.broadcasted_iota(jnp.int32, (nb, sc), 1) + c * sc).astype(F32)
            before = binc_e <= s
            blk = jnp.sum(jnp.where(before, 1.0, 0.0), axis=0, keepdims=True)
            base = jnp.sum(jnp.where(before, tot_e, 0.0), axis=0, keepdims=True)
            local = s[0:1] - base
            onehot = jnp.where(bidx == blk, 1.0, 0.0).astype(BF16)
            rows = _dot(incl_t, onehot)
            cnt = jnp.sum(jnp.where(rows <= local, 1.0, 0.0), axis=0, keepdims=True)
            tok = blk * np.float32(LANES) + cnt + np.float32(tok_base)
            idx_ref[e:e + 1, c * sc:(c + 1) * sc] = tok.astype(jnp.int32)
            arow = (_dot(aff_parts[0], onehot) + _dot(aff_parts[1], onehot)) + _dot(aff_parts[2], onehot)
            gate = jnp.sum(jnp.where(jidx == cnt, arow, 0.0), axis=0, keepdims=True)
            gate_col = jnp.transpose(jnp.broadcast_to(gate, (8, sc)))[:, 0:1]
            gate_ref[e, c * sc:(c + 1) * sc, :] = jnp.broadcast_to(gate_col, (sc, LANES))


def _topk_group(aff_t, tok_base, n, slot_base):
    E = N_EXPERTS
    M = aff_t.shape[1]
    nb = n // LANES
    cap = CAPACITY_FACTOR * n // E
    assert n % LANES == 0 and nb % 8 == 0 and cap % SLOT_CHUNK == 0 and tok_base % n == 0
    a3 = aff_t.reshape(E, M // LANES, LANES)
    blk = tok_base // n
    return pl.pallas_call(
        functools.partial(_topk_kernel, nb=nb, cap=cap, tok_base=tok_base, slot_base=slot_base),
        grid=(1,),
        in_specs=[pl.BlockSpec((E, nb, LANES), lambda i: (0, blk, 0))],
        out_specs=[pl.BlockSpec((E, cap), lambda i: (0, 0)),
                   pl.BlockSpec((E, cap, LANES), lambda i: (0, 0, 0)),
                   pl.BlockSpec((E, nb, LANES), lambda i: (0, 0, 0)),
                   pl.BlockSpec((E, nb, LANES), lambda i: (0, 0, 0))],
        out_shape=[jax.ShapeDtypeStruct((E, cap), jnp.int32),
                   jax.ShapeDtypeStruct((E, cap, LANES), F32),
                   jax.ShapeDtypeStruct((E, nb, LANES), F32),
                   jax.ShapeDtypeStruct((E, nb, LANES), F32)],
        scratch_shapes=[pltpu.VMEM((E, nb, LANES), F32)] * 3,
        compiler_params=_cparams(("arbitrary",)),
        name=f"expert_choice_topk_{n}",
    )(a3)


def _ffn_kernel(idx_ref, h_hbm, gate_ref, wg_hbm, wu_hbm, wd_hbm, ye_ref, xbuf, x_scr, wg_buf, wu_buf, wd_buf,
                sem, wsem, *, layer, tiles_per_expert, n_steps):
    e = pl.program_id(0)
    j = pl.program_id(1)
    step = e * tiles_per_expert + j
    slot = step % 2
    wslot = e % 2

    def weight_copies(expert, ws):
        return (pltpu.make_async_copy(wg_hbm.at[layer, expert], wg_buf.at[ws], wsem.at[ws, 0]),
                pltpu.make_async_copy(wu_hbm.at[layer, expert], wu_buf.at[ws], wsem.at[ws, 1]),
                pltpu.make_async_copy(wd_hbm.at[layer, expert], wd_buf.at[ws], wsem.at[ws, 2]))

    def row_copy(tile, k, slot_):
        t = idx_ref[tile * FFN_TILE + k]
        return pltpu.make_async_copy(h_hbm.at[pl.ds(t, 1), :], xbuf.at[slot_, pl.ds(k, 1), :],
                                     sem.at[slot_])

    @pl.when(step == 0)
    def _():
        for cp in weight_copies(0, 0):
            cp.start()

        def issue(k, carry):
            row_copy(0, k, 0).start()
            return carry
        lax.fori_loop(0, FFN_TILE, issue, 0, unroll=8)

    @pl.when(j == 0)
    def _():
        for cp in weight_copies(e, wslot):
            cp.wait()

        @pl.when(e + 1 < pl.num_programs(0))
        def _():
            for cp in weight_copies(e + 1, 1 - wslot):
                cp.start()

    for k in range(FFN_TILE):
        row_copy(step, k, slot).wait()
    nxt = jnp.minimum(step + 1, n_steps - 1)

    def compute(ws):
        x_scr[...] = xbuf[slot].astype(BF16)
        for k in range(FFN_TILE):
            row_copy(nxt, k, 1 - slot).start()
        acc = jnp.zeros((FFN_TILE, D), F32)
        for c in range(D_EXPERT // F_CHUNK):
            cs = slice(c * F_CHUNK, (c + 1) * F_CHUNK)
            hg = _dot(x_scr[...], wg_buf[ws, :, cs])
            hu = _dot(x_scr[...], wu_buf[ws, :, cs])
            act = (hg * _sigmoid(hg) * hu).astype(BF16)
            acc = acc + _dot(act, wd_buf[ws, cs, :])
        gate = gate_ref[...]
        for c in range(D // LANES):
            cs = slice(c * LANES, (c + 1) * LANES)
            ye_ref[:, cs] = (acc[:, cs] * gate).astype(BF16)

    for ws in range(2):
        @pl.when(wslot == ws)
        def _():
            compute(ws)

    @pl.when(step == n_steps - 1)
    def _():
        for k in range(FFN_TILE):
            row_copy(nxt, k, 1 - slot).wait()


def _expert_ffn(idx_flat, h2, gate_all, wg_bf, wu_bf, wd_bf, layer, slots):
    E = N_EXPERTS
    tiles = slots // FFN_TILE
    grid_spec = pltpu.PrefetchScalarGridSpec(
        num_scalar_prefetch=1,
        grid=(E, tiles),
        in_specs=[
            pl.BlockSpec(memory_space=pl.ANY),
            pl.BlockSpec((FFN_TILE, LANES), lambda e, j, idx: (e * tiles + j, 0)),
            pl.BlockSpec(memory_space=pl.ANY),
            pl.BlockSpec(memory_space=pl.ANY),
            pl.BlockSpec(memory_space=pl.ANY),
        ],
        out_specs=pl.BlockSpec((None, FFN_TILE, D), lambda e, j, idx: (e, j, 0)),
        scratch_shapes=[pltpu.VMEM((2, FFN_TILE, D), F32), pltpu.VMEM((FFN_TILE, D), BF16),
                        pltpu.VMEM((2, D, D_EXPERT), BF16), pltpu.VMEM((2, D, D_EXPERT), BF16),
                        pltpu.VMEM((2, D_EXPERT, D), BF16),
                        pltpu.SemaphoreType.DMA((2,)), pltpu.SemaphoreType.DMA((2, 3))],
    )
    return pl.pallas_call(
        functools.partial(_ffn_kernel, layer=layer, tiles_per_expert=tiles, n_steps=E * tiles),
        grid_spec=grid_spec,
        out_shape=jax.ShapeDtypeStruct((E, slots, D), BF16),
        compiler_params=_cparams(("arbitrary", "arbitrary")),
        name="expert_ffn",
    )(idx_flat, h2, gate_all.reshape(E * slots, LANES), wg_bf, wu_bf, wd_bf)


def _combine_kernel(tb_ref, bst_ref, x_ref, pos_ref, mod_ref, gfin_ref, ye_hbm, *rest, slots, blocks,
                    n_tiles, final_norm, split):
    if split is None:
        (o_ref, win, winx, acc, sem, semx) = rest
    else:
        (o_ref, o2_ref, win, winx, acc, sem, semx) = rest
    E = N_EXPERTS
    i = pl.program_id(0)
    bpt = TM_COMB // LANES
    stride = blocks + 1

    def tile_start(tile, e):
        return (bst_ref[e * stride + tile * bpt] // 16) * 16

    def window_copy(e, lo, buf, sem_):
        w0 = pl.multiple_of(jnp.minimum(lo, slots - WIN), 16)
        return pltpu.make_async_copy(ye_hbm.at[e, pl.ds(w0, WIN), :], buf.at[pl.ds(e * WIN, WIN), :],
                                     sem_.at[e])

    slot = i % 2

    @pl.when(i == 0)
    def _():
        for e in range(E):
            window_copy(e, tile_start(0, e), win.at[0], sem.at[0]).start()

    starts = [tile_start(i, e) for e in range(E)]
    npass = jnp.int32(1)
    for e in range(E):
        end = bst_ref[e * stride + (i + 1) * bpt]
        npass = jnp.maximum(npass, (end - starts[e] + (WIN - 1)) // WIN)

    for e in range(E):
        window_copy(e, starts[e], win.at[slot], sem.at[slot]).wait()
    nxt = jnp.minimum(i + 1, n_tiles - 1)
    for e in range(E):
        window_copy(e, tile_start(nxt, e), win.at[1 - slot], sem.at[1 - slot]).start()

    lane_e = lax.broadcasted_iota(jnp.int32, (1, E), 1)
    erow = lax.broadcasted_iota(jnp.int32, (E, E * WIN), 0)
    ecol = lax.broadcasted_iota(jnp.int32, (E, E * WIN), 1)
    expand = jnp.where(ecol // WIN == erow, 1.0, 0.0).astype(BF16)
    sprime = (lax.broadcasted_iota(jnp.int32, (1, E * WIN), 1) % WIN).astype(F32)
    pos = pos_ref[...]

    def scatter_rows(p, wbuf):
        lo_vec = jnp.zeros((1, E), F32)
        w0_vec = jnp.zeros((1, E), F32)
        for e in range(E):
            lo = starts[e] + p * WIN
            w0 = jnp.minimum(lo, slots - WIN)
            lo_vec = jnp.where(lane_e == e, lo.astype(F32), lo_vec)
            w0_vec = jnp.where(lane_e == e, w0.astype(F32), w0_vec)
        rel = jnp.where(pos >= lo_vec, pos - w0_vec, -1.0)
        rel = jnp.clip(rel, -1.0, np.float32(WIN))
        relx = _dot(rel.astype(BF16), expand)
        onehot = jnp.where(relx == sprime, 1.0, 0.0).astype(BF16)
        return _dot(onehot, wbuf[...])

    acc[...] = scatter_rows(0, win.at[slot])

    def extra_pass(p, carry):
        copies = [window_copy(e, starts[e] + p * WIN, winx, semx) for e in range(E)]
        for cp in copies:
            cp.start()
        for cp in copies:
            cp.wait()
        acc[...] += scatter_rows(p, winx)
        return carry

    lax.fori_loop(1, npass, extra_pass, 0)

    m = mod_ref[...]
    y = x_ref[...] + m[5:6] * acc[...]
    if final_norm:
        ms = jnp.mean(y * y, axis=-1, keepdims=True)
        y = y * lax.rsqrt(ms + NORM_EPS) * gfin_ref[...]
    if split is None:
        o_ref[...] = y
    else:
        @pl.when(i < split)
        def _():
            o_ref[...] = y

        @pl.when(i >= split)
        def _():
            o2_ref[...] = y

    @pl.when(i == n_tiles - 1)
    def _():
        for e in range(E):
            window_copy(e, tile_start(nxt, e), win.at[1 - slot], sem.at[1 - slot]).wait()


def _combine(x1, pos_t, mod, g_final, ye, bst_flat, groups, final_norm):
    M = x1.shape[0]
    tm = TM_COMB
    E = N_EXPERTS
    slots = ye.shape[1]
    tb, _ = _tile_tables(groups, tm)
    n_tiles = M // tm
    n1 = groups[0][0] * groups[0][1]
    if final_norm:
        split = n1 // tm
        out_specs = [pl.BlockSpec((tm, D), lambda i, tb, bst: (jnp.minimum(i, split - 1), 0)),
                     pl.BlockSpec((tm, D), lambda i, tb, bst: (jnp.maximum(i - split, 0), 0))]
        out_shape = [jax.ShapeDtypeStruct((n1, D), F32), jax.ShapeDtypeStruct((M - n1, D), F32)]
    else:
        split = None
        out_specs = pl.BlockSpec((tm, D), lambda i, tb, bst: (i, 0))
        out_shape = jax.ShapeDtypeStruct((M, D), F32)
    grid_spec = pltpu.PrefetchScalarGridSpec(
        num_scalar_prefetch=2,
        grid=(n_tiles,),
        in_specs=[
            pl.BlockSpec((tm, D), lambda i, tb, bst: (i, 0)),
            pl.BlockSpec((tm, E), lambda i, tb, bst: (i, 0)),
            pl.BlockSpec((None, 6, D), lambda i, tb, bst: (tb[i], 0, 0)),
            pl.BlockSpec((1, D), lambda i, tb, bst: (0, 0)),
            pl.BlockSpec(memory_space=pl.ANY),
        ],
        out_specs=out_specs,
        scratch_shapes=[pltpu.VMEM((2, E * WIN, D), BF16), pltpu.VMEM((E * WIN, D), BF16),
                        pltpu.VMEM((tm, D), F32), pltpu.SemaphoreType.DMA((2, E)),
                        pltpu.SemaphoreType.DMA((E,))],
    )
    return pl.pallas_call(
        functools.partial(_combine_kernel, slots=slots, blocks=M // LANES, n_tiles=n_tiles,
                          final_norm=final_norm, split=split),
        grid_spec=grid_spec,
        out_shape=out_shape,
        compiler_params=_cparams(("arbitrary",)),
        name="moe_combine",
    )(tb, bst_flat, x1, pos_t, mod, g_final, ye)


def _moe(x1, h2, aff_t, mod, g_final, wg_bf, wu_bf, wd_bf, layer, groups, final_norm):
    E = N_EXPERTS
    idxs, gates, poss, bsts = [], [], [], []
    tok_base = 0
    slot_base = 0
    for (B, S) in groups:
        n = B * S
        idx, gate, pos, bst = _topk_group(aff_t, tok_base, n, slot_base)
        idxs.append(idx)
        gates.append(gate)
        poss.append(pos.reshape(E, n))
        bsts.append(bst[:, :, 0])
        tok_base += n
        slot_base += CAPACITY_FACTOR * n // E
    slots = slot_base
    idx_flat = jnp.concatenate(idxs, axis=1).reshape(-1)
    ye = _expert_ffn(idx_flat, h2, jnp.concatenate(gates, axis=1), wg_bf, wu_bf, wd_bf, layer, slots)
    pos_t = jnp.concatenate(poss, axis=1).T
    bst = jnp.concatenate(bsts + [jnp.full((E, 1), slots, F32)], axis=1)
    bst_flat = bst.astype(jnp.int32).reshape(-1)
    return _combine(x1, pos_t, mod, g_final, ye, bst_flat, groups, final_norm)


def kernel(x_prompt, x_sample, c_prompt, c_sample, w_mod, b_mod, g_mix, g_ffn, a_w_in, a_w_out, b_w_in,
           b_ln_g, b_ln_b, b_w_s, b_b_s, b_w_out, moe_w_router, moe_w_gate, moe_w_up, moe_w_down, g_final):
    groups = (x_prompt.shape[:2], x_sample.shape[:2])
    assert x_prompt.shape[2] == D and x_sample.shape[2] == D
    xa, xb = x_prompt.reshape(-1, D), x_sample.reshape(-1, D)
    n_seq = groups[0][0] + groups[1][0]
    rows = -(-n_seq // 8) * 8
    c_all = jnp.concatenate([c_prompt, c_sample, jnp.zeros((rows - n_seq, D), F32)], axis=0)
    mod = _modulation(c_all, w_mod, b_mod).reshape(w_mod.shape[0], rows, 6, D)
    cos_t, sin_t = _rope_tables(max(groups[0][1], groups[1][1]))
    wr_t = jnp.swapaxes(moe_w_router, 1, 2)
    gfin = g_final.reshape(1, D)

    os_, ls_ = [], []
    for g, r in enumerate(DILATIONS):
        w_g = _pair_heads(a_w_in[0][:, g * 3 * D:(g + 1) * 3 * D]).astype(BF16)
        qkv_g = _qkv_proj(xa, xb, mod[0], g_mix[0].reshape(1, D), w_g, _permute_table(cos_t, TM_PROJ, r),
                          _permute_table(sin_t, TM_PROJ, r), groups, r)
        o_g, l_g = _attention_group(qkv_g, groups)
        os_.append(o_g)
        ls_.append(l_g)
    x1, h2, aff_t = _out_proj(os_, ls_, xa, xb, mod[0], a_w_out[0].astype(BF16), g_ffn[0].reshape(1, D),
                              wr_t[0], groups)
    wg_bf, wu_bf, wd_bf = moe_w_gate.astype(BF16), moe_w_up.astype(BF16), moe_w_down.astype(BF16)
    x = _moe(x1, h2, aff_t, mod[0], gfin, wg_bf, wu_bf, wd_bf, 0, groups, final_norm=False)

    x1, h2, aff_t = _sgu_layer(x, mod[1], g_mix[1].reshape(1, D), b_w_in[0].astype(BF16),
                               b_ln_g[0].reshape(1, SGU_HALF), b_ln_b[0].reshape(1, SGU_HALF),
                               b_w_s[0].astype(BF16), b_b_s[0].reshape(SGU_HEADS, CHUNK, 1),
                               b_w_out[0].astype(BF16), g_ffn[1].reshape(1, D), wr_t[1], groups)
    y1, y2 = _moe(x1, h2, aff_t, mod[1], gfin, wg_bf, wu_bf, wd_bf, 1, groups, final_norm=True)

    return (y1.reshape(x_prompt.shape), y2.reshape(x_sample.shape))
```

```python
import functools

import numpy as np
import jax
import jax.numpy as jnp
from jax import lax
from jax.experimental import pallas as pl
from jax.experimental.pallas import tpu as pltpu

F32 = jnp.float32
BF16 = jnp.bfloat16

D = 1024
N_HEADS = 16
HEAD_DIM = 64
DILATIONS = (1, 4, 16)
WINDOWS = (128, 512, 2048)
RADIUS = 64
N_GROUPS = 3
QKV_COLS = N_GROUPS * 3 * D
ROPE_THETA = 10000.0
CHUNK = 128
SGU_HALF = 3 * D
SGU_HEADS = 16
SGU_HEAD_CH = SGU_HALF // SGU_HEADS
N_EXPERTS = 16
CAPACITY_FACTOR = 2
D_EXPERT = 2 * D
NORM_EPS = 1e-6
NEG_BIG = -1e30

LANES = 128
VMEM_LIMIT = 56 * 1024 * 1024

TM_PROJ = 512
TM_SGU = 512
TM_COMB = 256
TQ = 256
UQ = 128
FFN_TILE = 256
F_CHUNK = 1024
WIN = 64
SLOT_CHUNK = 512


def _cparams(sem):
    return pltpu.CompilerParams(dimension_semantics=sem, vmem_limit_bytes=VMEM_LIMIT)


def _split2(x):
    hi = x.astype(BF16)
    lo = (x - hi.astype(F32)).astype(BF16)
    return hi, lo


def _split3(x):
    hi = x.astype(BF16)
    r1 = x - hi.astype(F32)
    mid = r1.astype(BF16)
    lo = (r1 - mid.astype(F32)).astype(BF16)
    return hi, mid, lo


def _dot(a, b):
    return jnp.dot(a, b, preferred_element_type=F32)


def _dot_nt(a, b):
    return lax.dot_general(a, b, (((1,), (1,)), ((), ())), preferred_element_type=F32)


def _dot3(a, b):
    ah, al = _split2(a)
    bh, bl = _split2(b)
    return _dot(ah, bh) + (_dot(ah, bl) + _dot(al, bh))


def _dot3_nt(a, b):
    ah, al = _split2(a)
    bh, bl = _split2(b)
    return _dot_nt(ah, bh) + (_dot_nt(ah, bl) + _dot_nt(al, bh))


def _sigmoid(x):
    return 1.0 / (1.0 + jnp.exp(-x))


def _gelu_tanh(x):
    c = np.float32(np.sqrt(2.0 / np.pi))
    return 0.5 * x * (1.0 + jnp.tanh(c * (x + np.float32(0.044715) * (x * x * x))))


def _norm_mod(x, g, shift, scale):
    ms = jnp.mean(x * x, axis=-1, keepdims=True)
    return (x * lax.rsqrt(ms + NORM_EPS) * g) * (1.0 + scale) + shift


def _tile_tables(groups, tm):
    tb, tp = [], []
    b0 = 0
    for (B, S) in groups:
        assert S % tm == 0
        for b in range(B):
            for k in range(S // tm):
                tb.append(b0 + b)
                tp.append(k)
        b0 += B
    return jnp.asarray(np.array(tb, np.int32)), jnp.asarray(np.array(tp, np.int32))


def _mod_kernel(c_ref, w_ref, b_ref, o_ref):
    c = c_ref[...]
    o_ref[...] = _dot3(c * _sigmoid(c), w_ref[...]) + b_ref[...]


def _modulation(c_pad, w_mod, b_mod):
    depth, _, n = w_mod.shape
    tn = 2048
    rows = c_pad.shape[0]
    return pl.pallas_call(
        _mod_kernel,
        grid=(depth, n // tn),
        in_specs=[
            pl.BlockSpec((rows, D), lambda l, j: (0, 0)),
            pl.BlockSpec((None, D, tn), lambda l, j: (l, 0, j)),
            pl.BlockSpec((None, 1, tn), lambda l, j: (l, 0, j)),
        ],
        out_specs=pl.BlockSpec((None, rows, tn), lambda l, j: (l, 0, j)),
        out_shape=jax.ShapeDtypeStruct((depth, rows, n), F32),
        compiler_params=_cparams(("parallel", "parallel")),
        name="modulation",
    )(c_pad, w_mod, b_mod.reshape(depth, 1, n))


def _permute_rows(src, slab, dst, r):
    tm, cols = src.shape
    n = tm // r
    for c in range(cols // LANES):
        slab[c] = src[:, c * LANES:(c + 1) * LANES]
    for res in range(r):
        for c in range(cols // LANES):
            dst[res * n:(res + 1) * n, c * LANES:(c + 1) * LANES] = slab[c, pl.ds(res, n, stride=r), :]


def _two_group_specs(tm, n_first_tiles):
    first = pl.BlockSpec((tm, D), lambda i, *_: (jnp.minimum(i, n_first_tiles - 1), 0))
    second = pl.BlockSpec((tm, D), lambda i, *_: (jnp.maximum(i - n_first_tiles, 0), 0))
    return [first, second]


def _qkv_kernel(tb_ref, tp_ref, xa_ref, xb_ref, mod_ref, g_ref, w_ref, cos_ref, sin_ref, o_ref, h_scr, *scr,
                r, n_first_tiles):
    m = mod_ref[...]
    x = jnp.where(pl.program_id(0) < n_first_tiles, xa_ref[...], xb_ref[...])
    if r > 1:
        slab, xp = scr
        _permute_rows(x, slab, xp, r)
        x = xp[...]
    h_scr[...] = _norm_mod(x, g_ref[...], m[0:1], m[1:2]).astype(BF16)

    n = h_scr.shape[0] // r
    nw = 2 * LANES
    cos = cos_ref[...]
    sin = sin_ref[...]
    qscale = np.float32(HEAD_DIM ** -0.5)
    tables = ((cos * qscale, sin * qscale), (cos, sin), None)
    for sec, table in enumerate(tables):
        for cw in range(D // nw):
            col0 = sec * D + cw * nw
            acc = _dot(h_scr[...], w_ref[:, col0:col0 + nw])
            for c in range(nw // LANES):
                xc = acc[:, c * LANES:(c + 1) * LANES]
                if table is not None:
                    xc = xc * table[0] + pltpu.roll(xc, LANES // 2, 1) * table[1]
                cs = slice(col0 + c * LANES, col0 + (c + 1) * LANES)
                o_ref[:, :, cs] = xc.reshape(r, n, LANES).astype(BF16)


def _qkv_proj(xa, xb, mod, g, w_bf, cos_p, sin_p, groups, r):
    M = xa.shape[0] + xb.shape[0]
    tm = TM_PROJ
    n_first_tiles = xa.shape[0] // tm
    tb, tp = _tile_tables(groups, tm)
    scratch = [pltpu.VMEM((tm, D), BF16)]
    if r > 1:
        scratch += [pltpu.VMEM((D // LANES, tm, LANES), F32), pltpu.VMEM((tm, D), F32)]
    grid_spec = pltpu.PrefetchScalarGridSpec(
        num_scalar_prefetch=2,
        grid=(M // tm,),
        in_specs=_two_group_specs(tm, n_first_tiles) + [
            pl.BlockSpec((None, 6, D), lambda i, tb, tp: (tb[i], 0, 0)),
            pl.BlockSpec((1, D), lambda i, tb, tp: (0, 0)),
            pl.BlockSpec((D, 3 * D), lambda i, tb, tp: (0, 0), pipeline_mode=pl.Buffered(1)),
            pl.BlockSpec((tm, LANES), lambda i, tb, tp: (tp[i], 0)),
            pl.BlockSpec((tm, LANES), lambda i, tb, tp: (tp[i], 0)),
        ],
        out_specs=pl.BlockSpec((r, tm // r, 3 * D), lambda i, tb, tp: (0, i, 0)),
        scratch_shapes=scratch,
    )
    return pl.pallas_call(
        functools.partial(_qkv_kernel, r=r, n_first_tiles=n_first_tiles),
        grid_spec=grid_spec,
        out_shape=jax.ShapeDtypeStruct((r, M // r, 3 * D), BF16),
        compiler_params=_cparams(("parallel",)),
        name=f"qkv_rope_d{r}",
    )(tb, tp, xa, xb, mod, g, w_bf, cos_p, sin_p)


def _rope_tables(s_max):
    half = HEAD_DIM // 2
    inv = ROPE_THETA ** (-jnp.arange(0, HEAD_DIM, 2, dtype=F32) / HEAD_DIM)
    ang = jnp.arange(s_max, dtype=F32)[:, None] * inv[None, :]
    cos, sin = jnp.cos(ang), jnp.sin(ang)
    cos_t = jnp.tile(cos, (1, LANES // half))
    sin_t = jnp.concatenate([-sin, -sin, sin, sin], axis=1)
    return cos_t, sin_t


def _pair_heads(w):
    half = HEAD_DIM // 2
    perm = np.concatenate([np.arange(0, half), np.arange(HEAD_DIM, HEAD_DIM + half),
                           np.arange(half, HEAD_DIM), np.arange(HEAD_DIM + half, LANES)])
    k = w.shape[0]
    qk = w[:, :2 * D].reshape(k, 2 * D // LANES, LANES)[:, :, perm].reshape(k, 2 * D)
    return jnp.concatenate([qk, w[:, 2 * D:]], axis=1)


def _permute_table(t, tm, r):
    s = t.shape[0]
    return t.reshape(s // tm, tm // r, r, LANES).transpose(0, 2, 1, 3).reshape(s, LANES)


def _attn_kernel(q_ref, kp_ref, kc_ref, kn_ref, vp_ref, vc_ref, vn_ref, o_ref, l_ref,
                 kw, vw, s_scr, p_scr, *, seq_blocks):
    i = pl.program_id(1)
    (n1, nb1), (n2, nb2) = seq_blocks
    first_grp = i < n1 * nb1
    pos = jnp.where(first_grp, i % nb1, (i - n1 * nb1) % nb2)
    nb = jnp.where(first_grp, nb1, nb2)
    has_prev = pos > 0
    has_next = pos < nb - 1

    W = RADIUS
    kw[0:W] = kp_ref[...]
    kw[W:W + TQ] = kc_ref[...]
    kw[W + TQ:W + TQ + W] = kn_ref[...]
    vw[0:W] = vp_ref[...]
    vw[W:W + TQ] = vc_ref[...]
    vw[W + TQ:W + TQ + W] = vn_ref[...]

    lane = lax.broadcasted_iota(jnp.int32, (UQ, LANES), 1)
    low_half = lane < HEAD_DIM
    head_a = (lane % HEAD_DIM) < (HEAD_DIM // 2)
    nu = TQ // UQ
    for u in range(nu):
        for hp in range(N_HEADS // 2):
            cs = slice(hp * LANES, (hp + 1) * LANES)
            q2 = q_ref[u * UQ:(u + 1) * UQ, cs]
            kwin = kw[u * UQ:u * UQ + 2 * UQ, cs]
            zero = jnp.zeros_like(q2)
            s_scr[u, 2 * hp] = _dot_nt(jnp.where(head_a, q2, zero), kwin)
            s_scr[u, 2 * hp + 1] = _dot_nt(jnp.where(head_a, zero, q2), kwin)

    rr = lax.broadcasted_iota(jnp.int32, (UQ, 2 * UQ), 0)
    cc = lax.broadcasted_iota(jnp.int32, (UQ, 2 * UQ), 1)
    band = jnp.abs((cc - W) - rr) <= RADIUS
    l_ref[...] = jnp.zeros(l_ref.shape, F32)
    for u in range(nu):
        valid = band
        if u == 0:
            valid = valid & ((cc >= W) | has_prev)
        if u == nu - 1:
            valid = valid & ((cc < 2 * UQ - W) | has_next)
        s = jnp.where(valid[None], s_scr[u], NEG_BIG)
        m = jnp.max(s, axis=-1, keepdims=True)
        p = jnp.exp(s - m)
        den = jnp.sum(p, axis=-1, keepdims=True)
        p_scr[u] = p.astype(BF16)
        rden = 1.0 / den
        lse = m + jnp.log(den)
        rows = slice(u * UQ, (u + 1) * UQ)
        for hp in range(N_HEADS // 2):
            cs = slice(hp * LANES, (hp + 1) * LANES)
            vwin = vw[u * UQ:u * UQ + 2 * UQ, cs]
            oa = _dot(p_scr[u, 2 * hp], vwin) * rden[2 * hp]
            ob = _dot(p_scr[u, 2 * hp + 1], vwin) * rden[2 * hp + 1]
            o_ref[rows, cs] = jnp.where(low_half, oa, ob).astype(BF16)
        for h in range(N_HEADS):
            l_ref[rows, h:h + 1] = lse[h]


def _attention_group(qkv_g, groups):
    r, rows, _ = qkv_g.shape
    W = RADIUS
    for (B, S) in groups:
        assert S % (r * TQ) == 0
    seq_blocks = tuple((B, S // r // TQ) for (B, S) in groups)
    q64 = TQ // W
    last64 = rows // W - 1

    def big(col):
        return pl.BlockSpec((None, TQ, D), lambda res, i: (res, i, col))

    def prv(col):
        return pl.BlockSpec((None, W, D), lambda res, i: (res, jnp.maximum(i * q64 - 1, 0), col))

    def nxt(col):
        return pl.BlockSpec((None, W, D), lambda res, i: (res, jnp.minimum((i + 1) * q64, last64), col))

    return pl.pallas_call(
        functools.partial(_attn_kernel, seq_blocks=seq_blocks),
        grid=(r, rows // TQ),
        in_specs=[big(0), prv(1), big(1), nxt(1), prv(2), big(2), nxt(2)],
        out_specs=[pl.BlockSpec((None, TQ, D), lambda res, i: (res, i, 0)),
                   pl.BlockSpec((None, TQ, LANES), lambda res, i: (res, i, 0))],
        out_shape=[jax.ShapeDtypeStruct((r, rows, D), BF16),
                   jax.ShapeDtypeStruct((r, rows, LANES), F32)],
        scratch_shapes=[pltpu.VMEM((TQ + 2 * W, D), BF16), pltpu.VMEM((TQ + 2 * W, D), BF16),
                        pltpu.VMEM((TQ // UQ, N_HEADS, UQ, 2 * UQ), F32),
                        pltpu.VMEM((TQ // UQ, N_HEADS, UQ, 2 * UQ), BF16)],
        compiler_params=_cparams(("parallel", "parallel")),
        name=f"attn_d{r}",
    )(*([qkv_g] * 7))


def _router(x1, m, g_ffn, wr_t):
    h2 = _norm_mod(x1, g_ffn, m[3:4], m[4:5])
    logits_t = _dot3_nt(wr_t, h2)
    mx = jnp.max(logits_t, axis=0, keepdims=True)
    e = jnp.exp(logits_t - mx)
    return h2, e / jnp.sum(e, axis=0, keepdims=True)


def _oproj_kernel(tb_ref, o1_ref, o4_ref, o16_ref, l1_ref, l4_ref, l16_ref, xa_ref, xb_ref, mod_ref, w_ref,
                  gf_ref, wr_ref, x1_ref, h2_ref, aff_ref, oslab, lslab, om_scr, *, n_first_tiles):
    tm = xa_ref.shape[0]
    nc = D // LANES
    for g, (r, o_ref, l_ref) in enumerate(zip(DILATIONS, (o1_ref, o4_ref, o16_ref),
                                              (l1_ref, l4_ref, l16_ref))):
        n = tm // r
        for res in range(r):
            rows = slice(None) if r == 1 else pl.ds(res, n, stride=r)
            lslab[g, 0, rows, :] = l_ref[res]
            for c in range(nc):
                oslab[g, c, rows, :] = o_ref[res, :, c * LANES:(c + 1) * LANES].astype(F32)
    ls = [lslab[g, 0] for g in range(N_GROUPS)]
    mx = jnp.maximum(jnp.maximum(ls[0], ls[1]), ls[2])
    es = [jnp.exp(l - mx) for l in ls]
    tot = es[0] + es[1] + es[2]
    hrow = lax.broadcasted_iota(jnp.int32, (LANES, D), 0)
    hcol = lax.broadcasted_iota(jnp.int32, (LANES, D), 1)
    expand = jnp.where((hrow < 2 * N_HEADS) & (hcol // HEAD_DIM == hrow % N_HEADS), 1.0, 0.0).astype(BF16)
    lane = lax.broadcasted_iota(jnp.int32, (tm, LANES), 1)
    wexp = []
    for g in range(N_GROUPS):
        w = es[g] / tot
        hi = w.astype(BF16).astype(F32)
        packed = jnp.where(lane < N_HEADS, hi, pltpu.roll(w - hi, N_HEADS, 1))
        wexp.append(_dot(packed.astype(BF16), expand))
    for c in range(nc):
        cs = slice(c * LANES, (c + 1) * LANES)
        om = wexp[0][:, cs] * oslab[0, c] + wexp[1][:, cs] * oslab[1, c] + wexp[2][:, cs] * oslab[2, c]
        om_scr[:, cs] = om.astype(BF16)
    m = mod_ref[...]
    y = _dot(om_scr[...], w_ref[...])
    x = jnp.where(pl.program_id(0) < n_first_tiles, xa_ref[...], xb_ref[...])
    x1 = x + m[2:3] * y
    x1_ref[...] = x1
    h2, aff_t = _router(x1, m, gf_ref[...], wr_ref[...])
    h2_ref[...] = h2
    aff_ref[...] = aff_t


def _out_proj(os_, ls_, xa, xb, mod, w_bf, g_ffn, wr_t, groups):
    M = xa.shape[0] + xb.shape[0]
    tm = TM_PROJ
    n_first_tiles = xa.shape[0] // tm
    tb, _ = _tile_tables(groups, tm)
    o_specs = [pl.BlockSpec((r, tm // r, D), lambda i, tb: (0, i, 0)) for r in DILATIONS]
    l_specs = [pl.BlockSpec((r, tm // r, LANES), lambda i, tb: (0, i, 0)) for r in DILATIONS]
    grid_spec = pltpu.PrefetchScalarGridSpec(
        num_scalar_prefetch=1,
        grid=(M // tm,),
        in_specs=o_specs + l_specs + _two_group_specs(tm, n_first_tiles) + [
            pl.BlockSpec((None, 6, D), lambda i, tb: (tb[i], 0, 0)),
            pl.BlockSpec((D, D), lambda i, tb: (0, 0)),
            pl.BlockSpec((1, D), lambda i, tb: (0, 0)),
            pl.BlockSpec((N_EXPERTS, D), lambda i, tb: (0, 0)),
        ],
        out_specs=[
            pl.BlockSpec((tm, D), lambda i, tb: (i, 0)),
            pl.BlockSpec((tm, D), lambda i, tb: (i, 0)),
            pl.BlockSpec((N_EXPERTS, tm), lambda i, tb: (0, i)),
        ],
        scratch_shapes=[pltpu.VMEM((N_GROUPS, D // LANES, tm, LANES), F32),
                        pltpu.VMEM((N_GROUPS, 1, tm, LANES), F32),
                        pltpu.VMEM((tm, D), BF16)],
    )
    return pl.pallas_call(
        functools.partial(_oproj_kernel, n_first_tiles=n_first_tiles),
        grid_spec=grid_spec,
        out_shape=[jax.ShapeDtypeStruct((M, D), F32), jax.ShapeDtypeStruct((M, D), F32),
                   jax.ShapeDtypeStruct((N_EXPERTS, M), F32)],
        compiler_params=_cparams(("parallel",)),
        name="attn_out_router",
    )(tb, *os_, *ls_, xa, xb, mod, w_bf, g_ffn, wr_t)


def _sgu_kernel(tb_ref, x_ref, mod_ref, g_ref, win_ref, lng_ref, lnb_ref, ws_ref, bs_ref, wout_ref,
                gf_ref, wr_ref, x1_ref, h2_ref, aff_ref, h_scr, v_scr, vn_scr, u_scr, g_scr):
    m = mod_ref[...]
    h_scr[...] = _norm_mod(x_ref[...], g_ref[...], m[0:1], m[1:2]).astype(BF16)
    tm = x_ref.shape[0]
    nc = 2 * LANES
    nchunks = SGU_HALF // nc
    s1 = jnp.zeros((tm, 1), F32)
    s2 = jnp.zeros((tm, 1), F32)
    for c in range(nchunks):
        z = _gelu_tanh(_dot(h_scr[...], win_ref[:, SGU_HALF + c * nc:SGU_HALF + (c + 1) * nc]))
        v_scr[:, c * nc:(c + 1) * nc] = z
        s1 = s1 + jnp.sum(z, axis=-1, keepdims=True)
        s2 = s2 + jnp.sum(z * z, axis=-1, keepdims=True)
    mu = s1 * np.float32(1.0 / SGU_HALF)
    var = s2 * np.float32(1.0 / SGU_HALF) - mu * mu
    rstd = lax.rsqrt(var + NORM_EPS)
    for c in range(nchunks):
        cs = slice(c * nc, (c + 1) * nc)
        vn_scr[:, cs] = ((v_scr[:, cs] - mu) * rstd * lng_ref[:, cs] + lnb_ref[:, cs]).astype(BF16)
        u_scr[:, cs] = _gelu_tanh(_dot(h_scr[...], win_ref[:, cs])).astype(BF16)
    for hd in range(SGU_HEADS):
        cs = slice(hd * SGU_HEAD_CH, (hd + 1) * SGU_HEAD_CH)
        for k in range(tm // CHUNK):
            rs = slice(k * CHUNK, (k + 1) * CHUNK)
            v_scr[rs, cs] = _dot(ws_ref[hd], vn_scr[rs, cs]) + bs_ref[hd]
    for c in range(nchunks):
        cs = slice(c * nc, (c + 1) * nc)
        g_scr[:, cs] = (u_scr[:, cs].astype(F32) * v_scr[:, cs]).astype(BF16)
    y = _dot(g_scr[...], wout_ref[...])
    x1 = x_ref[...] + m[2:3] * y
    x1_ref[...] = x1
    h2, aff_t = _router(x1, m, gf_ref[...], wr_ref[...])
    h2_ref[...] = h2
    aff_ref[...] = aff_t


def _sgu_layer(x, mod, g_mix, w_in_bf, ln_g, ln_b, w_s_bf, b_s_col, w_out_bf, g_ffn, wr_t, groups):
    M = x.shape[0]
    tm = TM_SGU
    tb, _ = _tile_tables(groups, tm)
    const = lambda *shape: pl.BlockSpec(shape, lambda i, tb: (0,) * len(shape),
                                        pipeline_mode=pl.Buffered(1))
    grid_spec = pltpu.PrefetchScalarGridSpec(
        num_scalar_prefetch=1,
        grid=(M // tm,),
        in_specs=[
            pl.BlockSpec((tm, D), lambda i, tb: (i, 0)),
            pl.BlockSpec((None, 6, D), lambda i, tb: (tb[i], 0, 0)),
            const(1, D),
            const(D, 2 * SGU_HALF),
            const(1, SGU_HALF),
            const(1, SGU_HALF),
            const(SGU_HEADS, CHUNK, CHUNK),
            const(SGU_HEADS, CHUNK, 1),
            const(SGU_HALF, D),
            const(1, D),
            const(N_EXPERTS, D),
        ],
        out_specs=[
            pl.BlockSpec((tm, D), lambda i, tb: (i, 0)),
            pl.BlockSpec((tm, D), lambda i, tb: (i, 0)),
            pl.BlockSpec((N_EXPERTS, tm), lambda i, tb: (0, i)),
        ],
        scratch_shapes=[pltpu.VMEM((tm, D), BF16), pltpu.VMEM((tm, SGU_HALF), F32),
                        pltpu.VMEM((tm, SGU_HALF), BF16), pltpu.VMEM((tm, SGU_HALF), BF16),
                        pltpu.VMEM((tm, SGU_HALF), BF16)],
    )
    return pl.pallas_call(
        _sgu_kernel,
        grid_spec=grid_spec,
        out_shape=[jax.ShapeDtypeStruct((M, D), F32), jax.ShapeDtypeStruct((M, D), F32),
                   jax.ShapeDtypeStruct((N_EXPERTS, M), F32)],
        compiler_params=_cparams(("parallel",)),
        name="sgu_router",
    )(tb, x, mod, g_mix, w_in_bf, ln_g, ln_b, w_s_bf, b_s_col, w_out_bf, g_ffn, wr_t)


def _topk_kernel(aff_ref, idx_ref, gate_ref, pos_ref, bst_ref, incl_scr, tot_scr, bex_scr, *, nb, cap,
                 tok_base, slot_base):
    E = N_EXPERTS
    a = aff_ref[...]
    bits = pltpu.bitcast(a, jnp.int32)

    def count(mask):
        c = jnp.sum(jnp.where(mask, 1.0, 0.0), axis=1, keepdims=True)
        return jnp.sum(c, axis=2, keepdims=True)

    def search(k, thr):
        cand = thr | jnp.left_shift(jnp.int32(1), 30 - k)
        return jnp.where(count(bits >= cand) >= cap, cand, thr)

    thr = lax.fori_loop(0, 31, search, jnp.zeros((E, 1, 1), jnp.int32))
    gt = bits > thr
    eq = bits == thr
    need = cap - count(gt)

    row = lax.broadcasted_iota(jnp.int32, (LANES, LANES), 0)
    col = lax.broadcasted_iota(jnp.int32, (LANES, LANES), 1)
    upper = jnp.where(row <= col, 1.0, 0.0).astype(BF16)
    ones = jnp.ones((LANES, LANES), BF16)
    brow = lax.broadcasted_iota(jnp.int32, (nb, nb), 0)
    bcol = lax.broadcasted_iota(jnp.int32, (nb, nb), 1)
    lower_strict = jnp.where(bcol < brow, 1.0, 0.0).astype(BF16)

    def prefix(mask):
        xb = jnp.where(mask, 1.0, 0.0).astype(BF16).reshape(E * nb, LANES)
        incl_scr[...] = _dot(xb, upper).reshape(E, nb, LANES)
        tot_scr[...] = _dot(xb, ones).reshape(E, nb, LANES)
        for e in range(E):
            bex_scr[e] = _dot(lower_strict, tot_scr[e].astype(BF16))

    prefix(eq)
    rank_eq = bex_scr[...] + incl_scr[...] - jnp.where(eq, 1.0, 0.0)
    sel = gt | (eq & (rank_eq < need))
    prefix(sel)
    self32 = jnp.where(sel, 1.0, 0.0)
    pos = bex_scr[...] + incl_scr[...] - self32
    pos_ref[...] = jnp.where(sel, pos + np.float32(slot_base), -1.0)
    bst_ref[...] = bex_scr[...] + np.float32(slot_base)

    sc = SLOT_CHUNK
    bidx = lax.broadcasted_iota(jnp.int32, (nb, sc), 0).astype(F32)
    jidx = lax.broadcasted_iota(jnp.int32, (LANES, sc), 0).astype(F32)
    for e in range(E):
        tot_e = tot_scr[e][:, 0:1]
        binc_e = bex_scr[e][:, 0:1] + tot_e
        incl_t = jnp.transpose(incl_scr[e]).astype(BF16)
        aff_parts = _split3(jnp.transpose(a[e]))
        for c in range(cap // sc):
            s = (lax.broadcasted_iota(jnp.int32, (nb, sc), 1) + c * sc).astype(F32)
            before = binc_e <= s
            blk = jnp.sum(jnp.where(before, 1.0, 0.0), axis=0, keepdims=True)
            base = jnp.sum(jnp.where(before, tot_e, 0.0), axis=0, keepdims=True)
            local = s[0:1] - base
            onehot = jnp.where(bidx == blk, 1.0, 0.0).astype(BF16)
            rows = _dot(incl_t, onehot)
            cnt = jnp.sum(jnp.where(rows <= local, 1.0, 0.0), axis=0, keepdims=True)
            tok = blk * np.float32(LANES) + cnt + np.float32(tok_base)
            idx_ref[e:e + 1, c * sc:(c + 1) * sc] = tok.astype(jnp.int32)
            arow = (_dot(aff_parts[0], onehot) + _dot(aff_parts[1], onehot)) + _dot(aff_parts[2], onehot)
            gate = jnp.sum(jnp.where(jidx == cnt, arow, 0.0), axis=0, keepdims=True)
            gate_col = jnp.transpose(jnp.broadcast_to(gate, (8, sc)))[:, 0:1]
            gate_ref[e, c * sc:(c + 1) * sc, :] = jnp.broadcast_to(gate_col, (sc, LANES))


def _topk_group(aff_t, tok_base, n, slot_base):
    E = N_EXPERTS
    M = aff_t.shape[1]
    nb = n // LANES
    cap = CAPACITY_FACTOR * n // E
    assert n % LANES == 0 and nb % 8 == 0 and cap % SLOT_CHUNK == 0 and tok_base % n == 0
    a3 = aff_t.reshape(E, M // LANES, LANES)
    blk = tok_base // n
    return pl.pallas_call(
        functools.partial(_topk_kernel, nb=nb, cap=cap, tok_base=tok_base, slot_base=slot_base),
        grid=(1,),
        in_specs=[pl.BlockSpec((E, nb, LANES), lambda i: (0, blk, 0))],
        out_specs=[pl.BlockSpec((E, cap), lambda i: (0, 0)),
                   pl.BlockSpec((E, cap, LANES), lambda i: (0, 0, 0)),
                   pl.BlockSpec((E, nb, LANES), lambda i: (0, 0, 0)),
                   pl.BlockSpec((E, nb, LANES), lambda i: (0, 0, 0))],
        out_shape=[jax.ShapeDtypeStruct((E, cap), jnp.int32),
                   jax.ShapeDtypeStruct((E, cap, LANES), F32),
                   jax.ShapeDtypeStruct((E, nb, LANES), F32),
                   jax.ShapeDtypeStruct((E, nb, LANES), F32)],
        scratch_shapes=[pltpu.VMEM((E, nb, LANES), F32)] * 3,
        compiler_params=_cparams(("arbitrary",)),
        name=f"expert_choice_topk_{n}",
    )(a3)


def _ffn_kernel(idx_ref, h_hbm, gate_ref, wg_hbm, wu_hbm, wd_hbm, ye_ref, xbuf, x_scr, wg_buf, wu_buf, wd_buf,
                sem, wsem, *, layer, tiles_per_expert, n_steps):
    e = pl.program_id(0)
    j = pl.program_id(1)
    step = e * tiles_per_expert + j
    slot = step % 2
    wslot = e % 2

    def weight_copies(expert, ws):
        return (pltpu.make_async_copy(wg_hbm.at[layer, expert], wg_buf.at[ws], wsem.at[ws, 0]),
                pltpu.make_async_copy(wu_hbm.at[layer, expert], wu_buf.at[ws], wsem.at[ws, 1]),
                pltpu.make_async_copy(wd_hbm.at[layer, expert], wd_buf.at[ws], wsem.at[ws, 2]))

    def row_copy(tile, k, slot_):
        t = idx_ref[tile * FFN_TILE + k]
        return pltpu.make_async_copy(h_hbm.at[pl.ds(t, 1), :], xbuf.at[slot_, pl.ds(k, 1), :],
                                     sem.at[slot_])

    @pl.when(step == 0)
    def _():
        for cp in weight_copies(0, 0):
            cp.start()

        def issue(k, carry):
            row_copy(0, k, 0).start()
            return carry
        lax.fori_loop(0, FFN_TILE, issue, 0, unroll=8)

    @pl.when(j == 0)
    def _():
        for cp in weight_copies(e, wslot):
            cp.wait()

        @pl.when(e + 1 < pl.num_programs(0))
        def _():
            for cp in weight_copies(e + 1, 1 - wslot):
                cp.start()

    for k in range(FFN_TILE):
        row_copy(step, k, slot).wait()
    nxt = jnp.minimum(step + 1, n_steps - 1)

    def compute(ws):
        x_scr[...] = xbuf[slot].astype(BF16)
        for k in range(FFN_TILE):
            row_copy(nxt, k, 1 - slot).start()
        acc = jnp.zeros((FFN_TILE, D), F32)
        for c in range(D_EXPERT // F_CHUNK):
            cs = slice(c * F_CHUNK, (c + 1) * F_CHUNK)
            hg = _dot(x_scr[...], wg_buf[ws, :, cs])
            hu = _dot(x_scr[...], wu_buf[ws, :, cs])
            act = (hg * _sigmoid(hg) * hu).astype(BF16)
            acc = acc + _dot(act, wd_buf[ws, cs, :])
        gate = gate_ref[...]
        for c in range(D // LANES):
            cs = slice(c * LANES, (c + 1) * LANES)
            ye_ref[:, cs] = (acc[:, cs] * gate).astype(BF16)

    for ws in range(2):
        @pl.when(wslot == ws)
        def _():
            compute(ws)

    @pl.when(step == n_steps - 1)
    def _():
        for k in range(FFN_TILE):
            row_copy(nxt, k, 1 - slot).wait()


def _expert_ffn(idx_flat, h2, gate_all, wg_bf, wu_bf, wd_bf, layer, slots):
    E = N_EXPERTS
    tiles = slots // FFN_TILE
    grid_spec = pltpu.PrefetchScalarGridSpec(
        num_scalar_prefetch=1,
        grid=(E, tiles),
        in_specs=[
            pl.BlockSpec(memory_space=pl.ANY),
            pl.BlockSpec((FFN_TILE, LANES), lambda e, j, idx: (e * tiles + j, 0)),
            pl.BlockSpec(memory_space=pl.ANY),
            pl.BlockSpec(memory_space=pl.ANY),
            pl.BlockSpec(memory_space=pl.ANY),
        ],
        out_specs=pl.BlockSpec((None, FFN_TILE, D), lambda e, j, idx: (e, j, 0)),
        scratch_shapes=[pltpu.VMEM((2, FFN_TILE, D), F32), pltpu.VMEM((FFN_TILE, D), BF16),
                        pltpu.VMEM((2, D, D_EXPERT), BF16), pltpu.VMEM((2, D, D_EXPERT), BF16),
                        pltpu.VMEM((2, D_EXPERT, D), BF16),
                        pltpu.SemaphoreType.DMA((2,)), pltpu.SemaphoreType.DMA((2, 3))],
    )
    return pl.pallas_call(
        functools.partial(_ffn_kernel, layer=layer, tiles_per_expert=tiles, n_steps=E * tiles),
        grid_spec=grid_spec,
        out_shape=jax.ShapeDtypeStruct((E, slots, D), BF16),
        compiler_params=_cparams(("arbitrary", "arbitrary")),
        name="expert_ffn",
    )(idx_flat, h2, gate_all.reshape(E * slots, LANES), wg_bf, wu_bf, wd_bf)


def _combine_kernel(tb_ref, bst_ref, x_ref, pos_ref, mod_ref, gfin_ref, ye_hbm, *rest, slots, blocks,
                    n_tiles, final_norm, split):
    if split is None:
        (o_ref, win, winx, acc, sem, semx) = rest
    else:
        (o_ref, o2_ref, win, winx, acc, sem, semx) = rest
    E = N_EXPERTS
    i = pl.program_id(0)
    bpt = TM_COMB // LANES
    stride = blocks + 1

    def tile_start(tile, e):
        return (bst_ref[e * stride + tile * bpt] // 16) * 16

    def window_copy(e, lo, buf, sem_):
        w0 = pl.multiple_of(jnp.minimum(lo, slots - WIN), 16)
        return pltpu.make_async_copy(ye_hbm.at[e, pl.ds(w0, WIN), :], buf.at[pl.ds(e * WIN, WIN), :],
                                     sem_.at[e])

    slot = i % 2

    @pl.when(i == 0)
    def _():
        for e in range(E):
            window_copy(e, tile_start(0, e), win.at[0], sem.at[0]).start()

    starts = [tile_start(i, e) for e in range(E)]
    npass = jnp.int32(1)
    for e in range(E):
        end = bst_ref[e * stride + (i + 1) * bpt]
        npass = jnp.maximum(npass, (end - starts[e] + (WIN - 1)) // WIN)

    for e in range(E):
        window_copy(e, starts[e], win.at[slot], sem.at[slot]).wait()
    nxt = jnp.minimum(i + 1, n_tiles - 1)
    for e in range(E):
        window_copy(e, tile_start(nxt, e), win.at[1 - slot], sem.at[1 - slot]).start()

    lane_e = lax.broadcasted_iota(jnp.int32, (1, E), 1)
    erow = lax.broadcasted_iota(jnp.int32, (E, E * WIN), 0)
    ecol = lax.broadcasted_iota(jnp.int32, (E, E * WIN), 1)
    expand = jnp.where(ecol // WIN == erow, 1.0, 0.0).astype(BF16)
    sprime = (lax.broadcasted_iota(jnp.int32, (1, E * WIN), 1) % WIN).astype(F32)
    pos = pos_ref[...]

    def scatter_rows(p, wbuf):
        lo_vec = jnp.zeros((1, E), F32)
        w0_vec = jnp.zeros((1, E), F32)
        for e in range(E):
            lo = starts[e] + p * WIN
            w0 = jnp.minimum(lo, slots - WIN)
            lo_vec = jnp.where(lane_e == e, lo.astype(F32), lo_vec)
            w0_vec = jnp.where(lane_e == e, w0.astype(F32), w0_vec)
        rel = jnp.where(pos >= lo_vec, pos - w0_vec, -1.0)
        rel = jnp.clip(rel, -1.0, np.float32(WIN))
        relx = _dot(rel.astype(BF16), expand)
        onehot = jnp.where(relx == sprime, 1.0, 0.0).astype(BF16)
        return _dot(onehot, wbuf[...])

    acc[...] = scatter_rows(0, win.at[slot])

    def extra_pass(p, carry):
        copies = [window_copy(e, starts[e] + p * WIN, winx, semx) for e in range(E)]
        for cp in copies:
            cp.start()
        for cp in copies:
            cp.wait()
        acc[...] += scatter_rows(p, winx)
        return carry

    lax.fori_loop(1, npass, extra_pass, 0)

    m = mod_ref[...]
    y = x_ref[...] + m[5:6] * acc[...]
    if final_norm:
        ms = jnp.mean(y * y, axis=-1, keepdims=True)
        y = y * lax.rsqrt(ms + NORM_EPS) * gfin_ref[...]
    if split is None:
        o_ref[...] = y
    else:
        @pl.when(i < split)
        def _():
            o_ref[...] = y

        @pl.when(i >= split)
        def _():
            o2_ref[...] = y

    @pl.when(i == n_tiles - 1)
    def _():
        for e in range(E):
            window_copy(e, tile_start(nxt, e), win.at[1 - slot], sem.at[1 - slot]).wait()


def _combine(x1, pos_t, mod, g_final, ye, bst_flat, groups, final_norm):
    M = x1.shape[0]
    tm = TM_COMB
    E = N_EXPERTS
    slots = ye.shape[1]
    tb, _ = _tile_tables(groups, tm)
    n_tiles = M // tm
    n1 = groups[0][0] * groups[0][1]
    if final_norm:
        split = n1 // tm
        out_specs = [pl.BlockSpec((tm, D), lambda i, tb, bst: (jnp.minimum(i, split - 1), 0)),
                     pl.BlockSpec((tm, D), lambda i, tb, bst: (jnp.maximum(i - split, 0), 0))]
        out_shape = [jax.ShapeDtypeStruct((n1, D), F32), jax.ShapeDtypeStruct((M - n1, D), F32)]
    else:
        split = None
        out_specs = pl.BlockSpec((tm, D), lambda i, tb, bst: (i, 0))
        out_shape = jax.ShapeDtypeStruct((M, D), F32)
    grid_spec = pltpu.PrefetchScalarGridSpec(
        num_scalar_prefetch=2,
        grid=(n_tiles,),
        in_specs=[
            pl.BlockSpec((tm, D), lambda i, tb, bst: (i, 0)),
            pl.BlockSpec((tm, E), lambda i, tb, bst: (i, 0)),
            pl.BlockSpec((None, 6, D), lambda i, tb, bst: (tb[i], 0, 0)),
            pl.BlockSpec((1, D), lambda i, tb, bst: (0, 0)),
            pl.BlockSpec(memory_space=pl.ANY),
        ],
        out_specs=out_specs,
        scratch_shapes=[pltpu.VMEM((2, E * WIN, D), BF16), pltpu.VMEM((E * WIN, D), BF16),
                        pltpu.VMEM((tm, D), F32), pltpu.SemaphoreType.DMA((2, E)),
                        pltpu.SemaphoreType.DMA((E,))],
    )
    return pl.pallas_call(
        functools.partial(_combine_kernel, slots=slots, blocks=M // LANES, n_tiles=n_tiles,
                          final_norm=final_norm, split=split),
        grid_spec=grid_spec,
        out_shape=out_shape,
        compiler_params=_cparams(("arbitrary",)),
        name="moe_combine",
    )(tb, bst_flat, x1, pos_t, mod, g_final, ye)


def _moe(x1, h2, aff_t, mod, g_final, wg_bf, wu_bf, wd_bf, layer, groups, final_norm):
    E = N_EXPERTS
    idxs, gates, poss, bsts = [], [], [], []
    tok_base = 0
    slot_base = 0
    for (B, S) in groups:
        n = B * S
        idx, gate, pos, bst = _topk_group(aff_t, tok_base, n, slot_base)
        idxs.append(idx)
        gates.append(gate)
        poss.append(pos.reshape(E, n))
        bsts.append(bst[:, :, 0])
        tok_base += n
        slot_base += CAPACITY_FACTOR * n // E
    slots = slot_base
    idx_flat = jnp.concatenate(idxs, axis=1).reshape(-1)
    ye = _expert_ffn(idx_flat, h2, jnp.concatenate(gates, axis=1), wg_bf, wu_bf, wd_bf, layer, slots)
    pos_t = jnp.concatenate(poss, axis=1).T
    bst = jnp.concatenate(bsts + [jnp.full((E, 1), slots, F32)], axis=1)
    bst_flat = bst.astype(jnp.int32).reshape(-1)
    return _combine(x1, pos_t, mod, g_final, ye, bst_flat, groups, final_norm)


def kernel(x_prompt, x_sample, c_prompt, c_sample, w_mod, b_mod, g_mix, g_ffn, a_w_in, a_w_out, b_w_in,
           b_ln_g, b_ln_b, b_w_s, b_b_s, b_w_out, moe_w_router, moe_w_gate, moe_w_up, moe_w_down, g_final):
    groups = (x_prompt.shape[:2], x_sample.shape[:2])
    assert x_prompt.shape[2] == D and x_sample.shape[2] == D
    xa, xb = x_prompt.reshape(-1, D), x_sample.reshape(-1, D)
    n_seq = groups[0][0] + groups[1][0]
    rows = -(-n_seq // 8) * 8
    c_all = jnp.concatenate([c_prompt, c_sample, jnp.zeros((rows - n_seq, D), F32)], axis=0)
    mod = _modulation(c_all, w_mod, b_mod).reshape(w_mod.shape[0], rows, 6, D)
    cos_t, sin_t = _rope_tables(max(groups[0][1], groups[1][1]))
    wr_t = jnp.swapaxes(moe_w_router, 1, 2)
    gfin = g_final.reshape(1, D)

    os_, ls_ = [], []
    for g, r in enumerate(DILATIONS):
        w_g = _pair_heads(a_w_in[0][:, g * 3 * D:(g + 1) * 3 * D]).astype(BF16)
        qkv_g = _qkv_proj(xa, xb, mod[0], g_mix[0].reshape(1, D), w_g, _permute_table(cos_t, TM_PROJ, r),
                          _permute_table(sin_t, TM_PROJ, r), groups, r)
        o_g, l_g = _attention_group(qkv_g, groups)
        os_.append(o_g)
        ls_.append(l_g)
    x1, h2, aff_t = _out_proj(os_, ls_, xa, xb, mod[0], a_w_out[0].astype(BF16), g_ffn[0].reshape(1, D),
                              wr_t[0], groups)
    wg_bf, wu_bf, wd_bf = moe_w_gate.astype(BF16), moe_w_up.astype(BF16), moe_w_down.astype(BF16)
    x = _moe(x1, h2, aff_t, mod[0], gfin, wg_bf, wu_bf, wd_bf, 0, groups, final_norm=False)

    x1, h2, aff_t = _sgu_layer(x, mod[1], g_mix[1].reshape(1, D), b_w_in[0].astype(BF16),
                               b_ln_g[0].reshape(1, SGU_HALF), b_ln_b[0].reshape(1, SGU_HALF),
                               b_w_s[0].astype(BF16), b_b_s[0].reshape(SGU_HEADS, CHUNK, 1),
                               b_w_out[0].astype(BF16), g_ffn[1].reshape(1, D), wr_t[1], groups)
    y1, y2 = _moe(x1, h2, aff_t, mod[1], gfin, wg_bf, wu_bf, wd_bf, 1, groups, final_norm=True)

    return (y1.reshape(x_prompt.shape), y2.reshape(x_sample.shape))
```
